```python
import jax, jax.numpy as jnp
from jax import lax
import numpy as np

D_MODEL = 1024
BATCH = 4
SEQ = 4096
DEPTH = 4
DEC_BATCH = 2
DEC_SEQ = 16384
PAST_LEN = 128

N_BRANCHES = 4
BRANCH_WIDTH = D_MODEL // 2
HEAD_DIM = 128
N_HEADS_BRANCH = BRANCH_WIDTH // HEAD_DIM
DILATION_GROUPS = ((128, 1), (512, 4), (2048, 16))
N_DIL = len(DILATION_GROUPS)
A_HEADS_TOTAL = N_DIL * N_HEADS_BRANCH
A_QKV_WIDTH = A_HEADS_TOTAL * HEAD_DIM
REL_BUCKETS = 32
REL_MAX_DIST = 1024
FNET_GROUPS = 4
FNET_GROUP_WIDTH = BRANCH_WIDTH // FNET_GROUPS
MLSTM_CHUNK = 64
MLSTM_CONV = 5
MLSTM_GATES = 2 * 2 * N_HEADS_BRANCH
N_MEM = 256
IN_WIDTH = (3 * A_QKV_WIDTH + BRANCH_WIDTH + 4 * BRANCH_WIDTH + MLSTM_GATES + BRANCH_WIDTH
            + N_BRANCHES * BRANCH_WIDTH + N_BRANCHES * D_MODEL)
EPS = 1e-6
NEG = -1e30

kernel_name = "hybrid_dilated_fnet_mlstm_encoder"


def rms_norm(x, g):
    xf = x.astype(jnp.float32)
    y = xf * lax.rsqrt(jnp.mean(xf * xf, axis=-1, keepdims=True) + EPS)
    return (y * g.astype(jnp.float32)).astype(x.dtype)


def t5_bucket(rel):
    nb = REL_BUCKETS // 2
    max_exact = nb // 2
    ret = (rel > 0).astype(np.int32) * nb
    n = np.abs(rel)
    large = max_exact + (np.log(np.maximum(n, 1) / max_exact) / np.log(REL_MAX_DIST / max_exact)
                         * (nb - max_exact)).astype(np.int32)
    large = np.minimum(large, nb - 1)
    return (ret + np.where(n < max_exact, n, large)).astype(np.int32)


def band_bias(rel_bias_heads, dilation, half):
    rel = np.arange(3 * half)[None, :] - half - np.arange(half)[:, None]
    idx = t5_bucket(dilation * rel)
    return jnp.transpose(rel_bias_heads[idx], (2, 0, 1))


def dilated_window_attention(q, k, v, dilation, half, bias):
    B, L, H, C = q.shape
    M = L // dilation
    nb = -(-M // half)
    Mp = nb * half

    def to_sub(t):
        return t.reshape(B, M, dilation, H, C).transpose(0, 2, 3, 1, 4)

    def key_windows(t):
        tp = jnp.pad(t, ((0, 0), (0, 0), (0, 0), (half, Mp - M + half), (0, 0)))
        tp = tp.reshape(B, dilation, H, nb + 2, half, C)
        return jnp.concatenate([tp[:, :, :, 0:nb], tp[:, :, :, 1:nb + 1], tp[:, :, :, 2:nb + 2]], axis=4)

    qb = jnp.pad(to_sub(q), ((0, 0), (0, 0), (0, 0), (0, Mp - M), (0, 0))).reshape(B, dilation, H, nb, half, C)
    kw = key_windows(to_sub(k))
    vw = key_windows(to_sub(v))
    s = jnp.einsum('brhnqc,brhnkc->brhnqk', qb, kw).astype(jnp.float32) * (C ** -0.5)
    s = s + bias[None, None, :, None].astype(jnp.float32)
    rel = np.arange(3 * half)[None, :] - half - np.arange(half)[:, None]
    band = np.abs(rel) <= half
    kidx = np.arange(nb)[:, None] * half - half + np.arange(3 * half)[None, :]
    valid = (kidx >= 0) & (kidx < M)
    mask = jnp.asarray(band[None] & valid[:, None, :])
    s = jnp.where(mask, s, NEG)
    lse = jax.nn.logsumexp(s, axis=-1)
    p = jnp.exp(s - lse[..., None])
    o = jnp.einsum('brhnqk,brhnkc->brhnqc', p.astype(vw.dtype), vw)
    o = o.reshape(B, dilation, H, Mp, C)[:, :, :, :M].transpose(0, 3, 1, 2, 4).reshape(B, L, H, C)
    lse = lse.reshape(B, dilation, H, Mp)[..., :M].transpose(0, 3, 1, 2).reshape(B, L, H)
    return o, lse


def centred_conv(x, w):
    K = w.shape[0]
    pad = K // 2
    L = x.shape[1]
    xp = jnp.pad(x, ((0, 0), (pad, pad), (0, 0)))
    return sum(xp[:, j:j + L] * w[j] for j in range(K))


def mlstm_chunked(q, k, v, i_pre, f_pre):
    B, H, L, dh = q.shape
    T = MLSTM_CHUNK
    N = L // T
    q = q.reshape(B, H, N, T, dh) * (dh ** -0.5)
    k = k.reshape(B, H, N, T, dh)
    v = v.reshape(B, H, N, T, dh)
    ig = i_pre.reshape(B, H, N, T)
    b = jnp.cumsum(jax.nn.log_sigmoid(f_pre).reshape(B, H, N, T), axis=-1)
    g = b[..., -1]
    causal = jnp.tril(jnp.ones((T, T), dtype=bool))
    logd = jnp.where(causal, b[..., :, None] - b[..., None, :] + ig[..., None, :], NEG)
    a = g[..., None] - b + ig
    ma = jnp.max(a, axis=-1)
    wa = jnp.exp(a - ma[..., None])
    c_loc = jnp.einsum('bhnt,bhntd,bhnte->bhnde', wa, k, v)
    n_loc = jnp.einsum('bhnt,bhntd->bhnd', wa, k)

    def step(carry, inp):
        c, n, m = carry
        g_c, ma_c, c_l, n_l = inp
        m_new = jnp.maximum(g_c + m, ma_c)
        s_old = jnp.exp(g_c + m - m_new)
        s_new = jnp.exp(ma_c - m_new)
        c_new = s_old[..., None, None] * c + s_new[..., None, None] * c_l
        n_new = s_old[..., None] * n + s_new[..., None] * n_l
        return (c_new, n_new, m_new), (c, n, m)

    init = (jnp.zeros((B, H, dh, dh), jnp.float32), jnp.zeros((B, H, dh), jnp.float32),
            jnp.zeros((B, H), jnp.float32))
    xs = (jnp.moveaxis(g, 2, 0), jnp.moveaxis(ma, 2, 0), jnp.moveaxis(c_loc, 2, 0), jnp.moveaxis(n_loc, 2, 0))
    _, (c_prev, n_prev, m_prev) = lax.scan(step, init, xs)
    c_prev = jnp.moveaxis(c_prev, 0, 2)
    n_prev = jnp.moveaxis(n_prev, 0, 2)
    m_prev = jnp.moveaxis(m_prev, 0, 2)
    inter = b + m_prev[..., None]
    m_t = jnp.maximum(inter, jnp.max(logd, axis=-1))
    w_intra = jnp.exp(logd - m_t[..., None])
    w_inter = jnp.exp(inter - m_t)
    sqk = w_intra * jnp.einsum('bhntd,bhnsd->bhnts', q, k)
    num = (w_inter[..., None] * jnp.einsum('bhntd,bhnde->bhnte', q, c_prev)
           + jnp.einsum('bhnts,bhnse->bhnte', sqk, v))
    den = w_inter * jnp.einsum('bhntd,bhnd->bhnt', q, n_prev) + jnp.sum(sqk, axis=-1)
    h = num / jnp.maximum(jnp.abs(den), jnp.exp(-m_t))[..., None]
    return h.reshape(B, H, L, dh)


def encoder_layer(x, mem, rel_bias, norm_pre, w_in, conv_qk, gate_bias, head_gain, mem_norm,
                  w_mem_kv, w_branch, w_out, norm_post):
    B, L, _ = x.shape
    H, dh, W = N_HEADS_BRANCH, HEAD_DIM, BRANCH_WIDTH
    h = rms_norm(x, norm_pre)
    z = h @ w_in
    widths = [A_QKV_WIDTH] * 3 + [W] + [W] * 4 + [MLSTM_GATES, W, N_BRANCHES * W, N_BRANCHES * D_MODEL]
    splits = np.cumsum(widths)[:-1].tolist()
    qa, ka, va, xb, qc, kc, vc, oc, gc, qd, gp, mg = jnp.split(z, splits, axis=-1)

    qa = qa.reshape(B, L, N_DIL, H, dh)
    ka = ka.reshape(B, L, N_DIL, H, dh)
    va = va.reshape(B, L, N_DIL, H, dh)
    outs, lses = [], []
    for gi, (window, dilation) in enumerate(DILATION_GROUPS):
        half = window // (2 * dilation)
        bias = band_bias(rel_bias[:, gi * H:(gi + 1) * H], dilation, half)
        o, l = dilated_window_attention(qa[:, :, gi], ka[:, :, gi], va[:, :, gi], dilation, half, bias)
        outs.append(o.astype(jnp.float32))
        lses.append(l)
    wgt = jax.nn.softmax(jnp.stack(lses, axis=0), axis=0)
    ya = jnp.einsum('gblh,gblhc->blhc', wgt, jnp.stack(outs, axis=0)).reshape(B, L, W)

    xb4 = xb.reshape(B, L, FNET_GROUPS, FNET_GROUP_WIDTH).astype(jnp.float32)
    yb = jnp.fft.fft2(xb4, axes=(1, 3), norm="ortho").real.reshape(B, L, W)

    qm = jax.nn.silu(centred_conv(qc, conv_qk[:, :W])).astype(jnp.float32)
    km = jax.nn.silu(centred_conv(kc, conv_qk[:, W:])).astype(jnp.float32)
    to_heads = lambda t: t.astype(jnp.float32).reshape(B, L, H, dh).transpose(0, 2, 1, 3)
    qm, km, vm = to_heads(qm), to_heads(km), to_heads(vc)
    gates = gc.astype(jnp.float32).reshape(B, L, 2, 2, H) + gate_bias.astype(jnp.float32)
    gates = gates.transpose(2, 3, 0, 4, 1)
    h_fwd = mlstm_chunked(qm, km, vm, gates[0, 0], gates[0, 1])
    flip = lambda t: jnp.flip(t, axis=2)
    h_bwd = flip(mlstm_chunked(flip(qm), flip(km), flip(vm), flip(gates[1, 0]), flip(gates[1, 1])))
    hc = (h_fwd + h_bwd).transpose(0, 2, 1, 3)
    hc = jax.nn.sigmoid(oc.astype(jnp.float32)).reshape(B, L, H, dh) * hc
    mu = jnp.mean(hc, axis=-1, keepdims=True)
    var = jnp.mean(jnp.square(hc - mu), axis=-1, keepdims=True)
    yc = ((hc - mu) * lax.rsqrt(var + EPS) * head_gain.astype(jnp.float32).reshape(H, dh)).reshape(B, L, W)

    kv = (rms_norm(mem, mem_norm) @ w_mem_kv).reshape(B, N_MEM, 2, H, dh)
    qd4 = qd.reshape(B, L, H, dh)
    s = jnp.einsum('blhc,bmhc->bhlm', qd4, kv[:, :, 0]).astype(jnp.float32) * (dh ** -0.5)
    p = jax.nn.softmax(s, axis=-1)
    yd = jnp.einsum('bhlm,bmhc->blhc', p.astype(kv.dtype), kv[:, :, 1]).reshape(B, L, W)

    branches = jnp.stack([ya, yb, yc, yd], axis=2).astype(x.dtype)
    gated = branches * jax.nn.silu(gp).reshape(B, L, N_BRANCHES, W)
    proj = jnp.einsum('blgc,gcd->blgd', gated, w_branch)
    merged = jnp.einsum('blgd,blgd->bld', jax.nn.sigmoid(mg).reshape(B, L, N_BRANCHES, D_MODEL), proj)
    out = merged @ w_out
    return x + rms_norm(out, norm_post)


def encoder_trunk(x, mem, rel_bias, norm_pre, w_in, conv_qk, mlstm_gate_bias, mlstm_head_gain,
                  mem_norm, w_mem_kv, w_branch, w_out, norm_post):
    for layer in range(DEPTH):
        x = encoder_layer(x, mem, rel_bias, norm_pre[layer], w_in[layer], conv_qk[layer],
                          mlstm_gate_bias[layer], mlstm_head_gain[layer], mem_norm[layer],
                          w_mem_kv[layer], w_branch[layer], w_out[layer], norm_post[layer])
    return x


def setup_inputs(seed: int = 0) -> dict:
    key = jax.random.key(seed)
    ks = jax.random.split(key, 20)
    nrm = lambda k, shape, scale: jax.random.normal(k, shape, jnp.float32) * scale
    H = N_HEADS_BRANCH
    i_bias = nrm(ks[10], (DEPTH, 2, 1, H), 0.1)
    f_bias = jnp.linspace(3.0, 6.0, H, dtype=jnp.float32)[None, None, None, :] + nrm(ks[11], (DEPTH, 2, 1, H), 0.1)
    return {
        "x_prompt": nrm(ks[0], (BATCH, SEQ, D_MODEL), 1.0),
        "x_sample": nrm(ks[1], (DEC_BATCH, DEC_SEQ, D_MODEL), 1.0),
        "mem_prompt": nrm(ks[2], (BATCH, N_MEM, D_MODEL), 1.0),
        "mem_sample": nrm(ks[3], (DEC_BATCH, N_MEM, D_MODEL), 1.0),
        "rel_bias": nrm(ks[4], (REL_BUCKETS, A_HEADS_TOTAL), 0.5),
        "norm_pre": 1.0 + nrm(ks[5], (DEPTH, D_MODEL), 0.1),
        "w_in": nrm(ks[6], (DEPTH, D_MODEL, IN_WIDTH), D_MODEL ** -0.5),
        "conv_qk": nrm(ks[7], (DEPTH, MLSTM_CONV, 2 * BRANCH_WIDTH), MLSTM_CONV ** -0.5),
        "mlstm_gate_bias": jnp.concatenate([i_bias, f_bias], axis=2),
        "mlstm_head_gain": 1.0 + nrm(ks[8], (DEPTH, BRANCH_WIDTH), 0.1),
        "mem_norm": 1.0 + nrm(ks[9], (DEPTH, D_MODEL), 0.1),
        "w_mem_kv": nrm(ks[12], (DEPTH, D_MODEL, 2 * BRANCH_WIDTH), D_MODEL ** -0.5),
        "w_branch": nrm(ks[13], (DEPTH, N_BRANCHES, BRANCH_WIDTH, D_MODEL), BRANCH_WIDTH ** -0.5),
        "w_out": nrm(ks[14], (DEPTH, D_MODEL, D_MODEL), D_MODEL ** -0.5),
        "norm_post": 1.0 + nrm(ks[15], (DEPTH, D_MODEL), 0.1),
    }


def reference(x_prompt, x_sample, mem_prompt, mem_sample, rel_bias, norm_pre, w_in, conv_qk,
              mlstm_gate_bias, mlstm_head_gain, mem_norm, w_mem_kv, w_branch, w_out, norm_post):
    y_prompt = encoder_trunk(x_prompt, mem_prompt, rel_bias, norm_pre, w_in, conv_qk, mlstm_gate_bias,
                             mlstm_head_gain, mem_norm, w_mem_kv, w_branch, w_out, norm_post)
    y_sample = encoder_trunk(x_sample, mem_sample, rel_bias, norm_pre, w_in, conv_qk, mlstm_gate_bias,
                             mlstm_head_gain, mem_norm, w_mem_kv, w_branch, w_out, norm_post)
    return (y_prompt, y_sample)
```

```python
import functools
import math

import numpy as np
import jax
import jax.numpy as jnp
from jax import lax
from jax.experimental import pallas as pl
from jax.experimental.pallas import tpu as pltpu

D_MODEL = 1024
BRANCH_WIDTH = 512
HEAD_DIM = 128
N_HEADS = 4
N_BRANCHES = 4
DILATIONS = (1, 4, 16)
HALF_WINDOW = 64
A_QKV_WIDTH = len(DILATIONS) * BRANCH_WIDTH
REL_BUCKETS = 32
REL_MAX_DIST = 1024
FNET_GROUP_WIDTH = 128
MLSTM_CONV = 5
MLSTM_GATES = 16
N_MEM = 256
EPS = 1e-6
NEG = -1e30

F32 = jnp.float32
BF16 = jnp.bfloat16

VMEM_LIMIT_BYTES = 56 * 1024 * 1024
PERM_TILE = 1024
ROW_TILE = 512
ATTN_Q_BLOCK = 128
MLSTM_CHUNK = 256
FFT_LANE_TILE = 2048
CONV_HALO = 16


def _params(*sem):
    return pltpu.CompilerParams(dimension_semantics=sem, vmem_limit_bytes=VMEM_LIMIT_BYTES)


def _resident(shape):
    nd = len(shape)
    return pl.BlockSpec(shape, lambda *_: (0,) * nd, pipeline_mode=pl.Buffered(1))


def _sigmoid(x):
    return 1.0 / (1.0 + jnp.exp(-x))


def _log_sigmoid(x):
    return jnp.minimum(x, 0.0) - jnp.log(1.0 + jnp.exp(-jnp.abs(x)))


def _dot(a, b):
    return jnp.dot(a, b, preferred_element_type=F32)


def _dot_nt(a, b):
    return lax.dot_general(a, b, (((1,), (1,)), ((), ())), preferred_element_type=F32)


def _dot_tn(a, b):
    return lax.dot_general(a, b, (((0,), (0,)), ((), ())), preferred_element_type=F32)


def _prenorm_kernel(x16_ref, x4_ref, g_ref, h_ref, h4_ref, h16_ref):
    D = D_MODEL

    def norm(x):
        ms = jnp.mean(x * x, axis=-1, keepdims=True)
        return (x * lax.rsqrt(ms + EPS) * g_ref[...]).astype(BF16)

    for r in range(16):
        hn = norm(x16_ref[0, :, r * D:(r + 1) * D])
        h_ref[0, :, r * D:(r + 1) * D] = hn
        h16_ref[0, r] = hn
    for r in range(4):
        h4_ref[0, r] = norm(x4_ref[0, :, r * D:(r + 1) * D])


def _prenorm(x, gain):
    B, L, D = x.shape
    T = PERM_TILE
    h, h4, h16 = pl.pallas_call(
        _prenorm_kernel,
        grid=(B, L // T),
        in_specs=[pl.BlockSpec((1, T // 16, 16 * D), lambda b, i: (b, i, 0)),
                  pl.BlockSpec((1, T // 4, 4 * D), lambda b, i: (b, i, 0)),
                  pl.BlockSpec((1, D), lambda b, i: (0, 0))],
        out_specs=[pl.BlockSpec((1, T // 16, 16 * D), lambda b, i: (b, i, 0)),
                   pl.BlockSpec((1, 4, T // 4, D), lambda b, i: (b, 0, i, 0)),
                   pl.BlockSpec((1, 16, T // 16, D), lambda b, i: (b, 0, i, 0))],
        out_shape=[jax.ShapeDtypeStruct((B, L // 16, 16 * D), BF16),
                   jax.ShapeDtypeStruct((B, 4, L // 4, D), BF16),
                   jax.ShapeDtypeStruct((B, 16, L // 16, D), BF16)],
        compiler_params=_params("parallel", "parallel"),
    )(x.reshape(B, L // 16, 16 * D), x.reshape(B, L // 4, 4 * D), gain.reshape(1, D))
    return h.reshape(B, L, D), h4, h16


def _proj_a_kernel(h_ref, w_ref, q_ref, kv_ref):
    z = _dot(h_ref[...], w_ref[...])
    q_ref[...] = z[:, :BRANCH_WIDTH].astype(BF16)
    kv_ref[...] = z[:, BRANCH_WIDTH:].astype(BF16)


def _proj_a(h_rows, w_qkv):
    R, D = h_rows.shape
    T = ROW_TILE
    W = BRANCH_WIDTH
    return pl.pallas_call(
        _proj_a_kernel,
        grid=(R // T,),
        in_specs=[pl.BlockSpec((T, D), lambda i: (i, 0)), _resident((D, 3 * W))],
        out_specs=[pl.BlockSpec((T, W), lambda i: (i, 0)), pl.BlockSpec((T, 2 * W), lambda i: (i, 0))],
        out_shape=[jax.ShapeDtypeStruct((R, W), BF16), jax.ShapeDtypeStruct((R, 2 * W), BF16)],
        compiler_params=_params("parallel"),
    )(h_rows, w_qkv)


def _attn_kernel(q_ref, kv_ref, kvp_ref, kvn_ref, bias_ref, o_ref, lse_ref, win_ref, *, seq_len, tq):
    i = pl.program_id(1)
    hw, qb, W = HALF_WINDOW, ATTN_Q_BLOCK, BRANCH_WIDTH
    kb = qb + 2 * hw
    win_ref[0:hw] = kvp_ref[0]
    win_ref[hw:hw + tq] = kv_ref[0]
    win_ref[hw + tq:] = kvn_ref[0]
    scale = HEAD_DIM ** -0.5
    for j in range(tq // qb):
        a = j * qb
        kidx = i * tq + (a - hw) + lax.broadcasted_iota(jnp.int32, (1, kb), 1)
        valid = (kidx >= 0) & (kidx < seq_len)
        for h in range(N_HEADS):
            c0, c1 = h * HEAD_DIM, (h + 1) * HEAD_DIM
            q = q_ref[0, a:a + qb, c0:c1]
            k = win_ref[a:a + kb, c0:c1]
            v = win_ref[a:a + kb, W + c0:W + c1]
            s = _dot_nt(q, k) * scale + bias_ref[h]
            s = jnp.where(valid, s, NEG)
            m = jnp.max(s, axis=-1, keepdims=True)
            p = jnp.exp(s - m)
            l = jnp.sum(p, axis=-1, keepdims=True)
            o = _dot(p.astype(BF16), v) / l
            o_ref[0, a:a + qb, c0:c1] = o.astype(BF16)
            lse_ref[0, a:a + qb, h:h + 1] = m + jnp.log(l)


def _attention(q, kv, bias):
    S, M, W = q.shape
    hw = HALF_WINDOW
    tq = min(M, 512)
    nhb = M // hw
    step = tq // hw
    kern = functools.partial(_attn_kernel, seq_len=M, tq=tq)
    return pl.pallas_call(
        kern,
        grid=(S, M // tq),
        in_specs=[pl.BlockSpec((1, tq, W), lambda s, i: (s, i, 0)),
                  pl.BlockSpec((1, tq, 2 * W), lambda s, i: (s, i, 0)),
                  pl.BlockSpec((1, hw, 2 * W), lambda s, i: (s, jnp.maximum(i * step - 1, 0), 0)),
                  pl.BlockSpec((1, hw, 2 * W), lambda s, i: (s, jnp.minimum((i + 1) * step, nhb - 1), 0)),
                  _resident(bias.shape)],
        out_specs=[pl.BlockSpec((1, tq, W), lambda s, i: (s, i, 0)),
                   pl.BlockSpec((1, tq, N_HEADS), lambda s, i: (s, i, 0))],
        out_shape=[jax.ShapeDtypeStruct((S, M, W), BF16), jax.ShapeDtypeStruct((S, M, N_HEADS), F32)],
        scratch_shapes=[pltpu.VMEM((tq + 2 * hw, 2 * W), BF16)],
        compiler_params=_params("parallel", "parallel"),
    )(q, kv, kv, kv, bias)


def _combine_kernel(o1_ref, o4_ref, o16_ref, l1_ref, l4_ref, l16_ref, ya_ref):
    W, H = BRANCH_WIDTH, N_HEADS
    for r in range(16):
        q, r4 = divmod(r, 4)
        l1 = l1_ref[0, :, r * H:(r + 1) * H]
        l4 = l4_ref[0, r4, :, q * H:(q + 1) * H]
        l16 = l16_ref[0, r]
        m = jnp.maximum(jnp.maximum(l1, l4), l16)
        e1, e4, e16 = jnp.exp(l1 - m), jnp.exp(l4 - m), jnp.exp(l16 - m)
        inv = 1.0 / (e1 + e4 + e16)
        w1, w4, w16 = e1 * inv, e4 * inv, e16 * inv
        for h in range(H):
            c0, c1 = h * HEAD_DIM, (h + 1) * HEAD_DIM
            ya = (w1[:, h:h + 1] * o1_ref[0, :, r * W + c0:r * W + c1].astype(F32)
                  + w4[:, h:h + 1] * o4_ref[0, r4, :, q * W + c0:q * W + c1].astype(F32)
                  + w16[:, h:h + 1] * o16_ref[0, r, :, c0:c1].astype(F32))
            ya_ref[0, :, r * W + c0:r * W + c1] = ya.astype(BF16)


def _combine(o1, o4, o16, l1, l4, l16):
    B, _, L, W = o1.shape
    H = N_HEADS
    R = PERM_TILE // 16
    M = L // 16
    ya = pl.pallas_call(
        _combine_kernel,
        grid=(B, M // R),
        in_specs=[pl.BlockSpec((1, R, 16 * W), lambda b, i: (b, i, 0)),
                  pl.BlockSpec((1, 4, R, 4 * W), lambda b, i: (b, 0, i, 0)),
                  pl.BlockSpec((1, 16, R, W), lambda b, i: (b, 0, i, 0)),
                  pl.BlockSpec((1, R, 16 * H), lambda b, i: (b, i, 0)),
                  pl.BlockSpec((1, 4, R, 4 * H), lambda b, i: (b, 0, i, 0)),
                  pl.BlockSpec((1, 16, R, H), lambda b, i: (b, 0, i, 0))],
        out_specs=pl.BlockSpec((1, R, 16 * W), lambda b, i: (b, i, 0)),
        out_shape=jax.ShapeDtypeStruct((B, M, 16 * W), BF16),
        compiler_params=_params("parallel", "parallel"),
    )(o1.reshape(B, M, 16 * W), o4.reshape(B, 4, M, 4 * W), o16,
      l1.reshape(B, M, 16 * H), l4.reshape(B, 4, M, 4 * H), l16)
    return ya.reshape(B, L, W)


def _memkv_kernel(mem_ref, g_ref, w_ref, kv_ref):
    x = mem_ref[0]
    ms = jnp.mean(x * x, axis=-1, keepdims=True)
    xn = (x * lax.rsqrt(ms + EPS) * g_ref[...]).astype(BF16)
    kv_ref[0] = _dot(xn, w_ref[...]).astype(BF16)


def _memkv(mem, gain, w):
    B, N, D = mem.shape
    W2 = w.shape[1]
    return pl.pallas_call(
        _memkv_kernel,
        grid=(B,),
        in_specs=[pl.BlockSpec((1, N, D), lambda b: (b, 0, 0)), pl.BlockSpec((1, D), lambda b: (0, 0)),
                  _resident((D, W2))],
        out_specs=pl.BlockSpec((1, N, W2), lambda b: (b, 0, 0)),
        out_shape=jax.ShapeDtypeStruct((B, N, W2), BF16),
        compiler_params=_params("parallel"),
    )(mem, gain.reshape(1, D), w)


def _proj_bd_kernel(h_ref, w_ref, kv_ref, xb_ref, yd_ref):
    W = BRANCH_WIDTH
    z = _dot(h_ref[0], w_ref[...])
    xb_ref[0] = z[:, :W].astype(BF16)
    qd = z[:, W:].astype(BF16)
    scale = HEAD_DIM ** -0.5
    for h in range(N_HEADS):
        c0, c1 = h * HEAD_DIM, (h + 1) * HEAD_DIM
        s = _dot_nt(qd[:, c0:c1], kv_ref[0, :, c0:c1]) * scale
        m = jnp.max(s, axis=-1, keepdims=True)
        p = jnp.exp(s - m)
        l = jnp.sum(p, axis=-1, keepdims=True)
        o = _dot(p.astype(BF16), kv_ref[0, :, W + c0:W + c1]) / l
        yd_ref[0, :, c0:c1] = o.astype(BF16)


def _proj_bd(h, w_bd, kv):
    B, L, D = h.shape
    T, W = ROW_TILE, BRANCH_WIDTH
    return pl.pallas_call(
        _proj_bd_kernel,
        grid=(B, L // T),
        in_specs=[pl.BlockSpec((1, T, D), lambda b, i: (b, i, 0)), _resident((D, 2 * W)),
                  pl.BlockSpec((1, N_MEM, 2 * W), lambda b, i: (b, 0, 0))],
        out_specs=[pl.BlockSpec((1, T, W), lambda b, i: (b, i, 0)), pl.BlockSpec((1, T, W), lambda b, i: (b, i, 0))],
        out_shape=[jax.ShapeDtypeStruct((B, L, W), BF16), jax.ShapeDtypeStruct((B, L, W), BF16)],
        compiler_params=_params("parallel", "parallel"),
    )(h, w_bd, kv)


def _fft_factors(L):
    n1 = 1 << ((L.bit_length() - 1 + 1) // 2)
    return n1, L // n1


def _dft_cos_sin(n):
    ang = 2.0 * np.pi * ((np.arange(n)[:, None] * np.arange(n)[None, :]) % n) / n
    return np.cos(ang), np.sin(ang)


def _fft1_kernel(x_ref, f_ref, tc_ref, ts_ref, yr_ref, yi_ref, *, n1):
    y = _dot(f_ref[...], x_ref[0])
    yr, yi = y[:n1], y[n1:]
    reps = BRANCH_WIDTH // FNET_GROUP_WIDTH
    tc = jnp.concatenate([tc_ref[0]] * reps, axis=1)
    ts = jnp.concatenate([ts_ref[0]] * reps, axis=1)
    yr_ref[0, 0] = (yr * tc + yi * ts).astype(BF16)
    yi_ref[0, 0] = (yi * tc - yr * ts).astype(BF16)


def _fft2_kernel(yr_ref, yi_ref, g1_ref, g2_ref, cs_ref, out_ref, *, n2, norm):
    u = _dot(g1_ref[...], yr_ref[0]) + _dot(g2_ref[...], yi_ref[0])
    top, bot = u[:n2].astype(BF16), u[n2:].astype(BF16)
    gw = FNET_GROUP_WIDTH
    ng = FFT_LANE_TILE // gw
    lhs = jnp.concatenate(
        [jnp.concatenate([top[:, j * gw:(j + 1) * gw], bot[:, j * gw:(j + 1) * gw]], axis=1) for j in range(ng)],
        axis=0)
    res = _dot(lhs, cs_ref[...]) * norm
    for j in range(ng):
        out_ref[0, :, j * gw:(j + 1) * gw] = res[j * n2:(j + 1) * n2].astype(BF16)


def _fnet(xb, tables):
    B, L, W = xb.shape
    n1, n2 = _fft_factors(L)
    f1, twc, tws, g1, g2, cs = tables
    yr, yi = pl.pallas_call(
        functools.partial(_fft1_kernel, n1=n1),
        grid=(B, n2),
        in_specs=[pl.BlockSpec((1, n1, W), lambda b, j: (b, 0, j)), _resident((2 * n1, n1)),
                  pl.BlockSpec((1, n1, FNET_GROUP_WIDTH), lambda b, j: (j, 0, 0)),
                  pl.BlockSpec((1, n1, FNET_GROUP_WIDTH), lambda b, j: (j, 0, 0))],
        out_specs=[pl.BlockSpec((1, 1, n1, W), lambda b, j: (b, j, 0, 0))] * 2,
        out_shape=[jax.ShapeDtypeStruct((B, n2, n1, W), BF16)] * 2,
        compiler_params=_params("parallel", "parallel"),
    )(xb.reshape(B, n1, n2 * W), f1, twc, tws)
    tn = FFT_LANE_TILE
    out = pl.pallas_call(
        functools.partial(_fft2_kernel, n2=n2, norm=float(1.0 / math.sqrt(L * FNET_GROUP_WIDTH))),
        grid=(B, n1 * W // tn),
        in_specs=[pl.BlockSpec((1, n2, tn), lambda b, j: (b, 0, j))] * 2
        + [_resident((2 * n2, n2)), _resident((2 * n2, n2)), _resident((2 * FNET_GROUP_WIDTH, FNET_GROUP_WIDTH))],
        out_specs=pl.BlockSpec((1, n2, tn), lambda b, j: (b, 0, j)),
        out_shape=jax.ShapeDtypeStruct((B, n2, n1 * W), BF16),
        compiler_params=_params("parallel", "parallel"),
    )(yr.reshape(B, n2, n1 * W), yi.reshape(B, n2, n1 * W), g1, g2, cs)
    return out.reshape(B, L, W)


def _fnet_tables(L):
    n1, n2 = _fft_factors(L)
    c1, s1 = _dft_cos_sin(n1)
    c2, s2 = _dft_cos_sin(n2)
    cc, sc = _dft_cos_sin(FNET_GROUP_WIDTH)
    ang = 2.0 * np.pi * (np.arange(n2)[:, None] * np.arange(n1)[None, :]) / L
    bcast = lambda t: jnp.asarray(np.broadcast_to(t[:, :, None], (n2, n1, FNET_GROUP_WIDTH)), F32)
    return (jnp.asarray(np.concatenate([c1, -s1], axis=0), BF16), bcast(np.cos(ang)), bcast(np.sin(ang)),
            jnp.asarray(np.concatenate([c2, -s2], axis=0), BF16), jnp.asarray(np.concatenate([s2, c2], axis=0), BF16),
            jnp.asarray(np.concatenate([cc, sc], axis=0), BF16))


def _proj_c_kernel(h_ref, hp_ref, hn_ref, wqk_ref, wvo_ref, wg_ref, conv_ref, gb_ref,
                   q_ref, k_ref, v_ref, o_ref, g_ref, ext_ref):
    i = pl.program_id(1)
    T, W, halo = ROW_TILE, BRANCH_WIDTH, CONV_HALO
    h = h_ref[0]
    has_prev = (i > 0).astype(F32)
    has_next = (i < pl.num_programs(1) - 1).astype(F32)
    ext_ref[0:halo] = _dot(hp_ref[0], wqk_ref[...]) * has_prev
    ext_ref[halo:halo + T] = _dot(h, wqk_ref[...])
    ext_ref[halo + T:] = _dot(hn_ref[0], wqk_ref[...]) * has_next
    pad = MLSTM_CONV // 2
    acc = ext_ref[pl.ds(halo - pad, T), :] * conv_ref[0:1, :]
    for j in range(1, MLSTM_CONV):
        acc = acc + ext_ref[pl.ds(halo - pad + j, T), :] * conv_ref[j:j + 1, :]
    y = acc * _sigmoid(acc)
    q_ref[0] = (y[:, :W] * (HEAD_DIM ** -0.5)).astype(BF16)
    k_ref[0] = y[:, W:].astype(BF16)
    vo = _dot(h, wvo_ref[...])
    v_ref[0] = vo[:, :W].astype(BF16)
    o_ref[0] = vo[:, W:].astype(BF16)
    g_ref[0] = _dot(h, wg_ref[...]) + gb_ref[...]


def _proj_c(h, w_qk, w_vo, w_g, conv, gate_bias):
    B, L, D = h.shape
    T, W, halo = ROW_TILE, BRANCH_WIDTH, CONV_HALO
    step = T // halo
    nhb = L // halo
    tile = lambda c: pl.BlockSpec((1, T, c), lambda b, i: (b, i, 0))
    return pl.pallas_call(
        _proj_c_kernel,
        grid=(B, L // T),
        in_specs=[tile(D),
                  pl.BlockSpec((1, halo, D), lambda b, i: (b, jnp.maximum(i * step - 1, 0), 0)),
                  pl.BlockSpec((1, halo, D), lambda b, i: (b, jnp.minimum((i + 1) * step, nhb - 1), 0)),
                  _resident((D, 2 * W)), _resident((D, 2 * W)), _resident((D, MLSTM_GATES)),
                  _resident((MLSTM_CONV, 2 * W)), _resident((1, MLSTM_GATES))],
        out_specs=[tile(W), tile(W), tile(W), tile(W), tile(MLSTM_GATES)],
        out_shape=[jax.ShapeDtypeStruct((B, L, W), BF16)] * 4 + [jax.ShapeDtypeStruct((B, L, MLSTM_GATES), F32)],
        scratch_shapes=[pltpu.VMEM((T + 2 * halo, 2 * W), F32)],
        compiler_params=_params("parallel", "arbitrary"),
    )(h, h, h, w_qk, w_vo, w_g, conv, gate_bias.reshape(1, MLSTM_GATES))


def _mlstm_kernel(*refs, reverse, finalize):
    if finalize:
        (q_ref, k_ref, v_ref, g_ref, gt_ref, hf_ref, oc_ref, gain_ref, out_ref, c_ref, m_ref) = refs
    else:
        (q_ref, k_ref, v_ref, g_ref, gt_ref, out_ref, c_ref, m_ref) = refs
    T, dh, H = MLSTM_CHUNK, HEAD_DIM, N_HEADS

    @pl.when(pl.program_id(1) == 0)
    def _():
        c_ref[...] = jnp.zeros_like(c_ref)
        m_ref[...] = jnp.zeros_like(m_ref)

    gi0 = 8 if reverse else 0
    gcol = g_ref[0]
    grow = gt_ref[0]
    row = lax.broadcasted_iota(jnp.int32, (T, T), 0)
    col = lax.broadcasted_iota(jnp.int32, (T, T), 1)
    feeds = (col >= row) if reverse else (col <= row)
    tri = feeds.astype(F32)
    tri_t = ((row >= col) if reverse else (row <= col)).astype(F32)
    lf_cols = _log_sigmoid(gcol[:, gi0 + H:gi0 + 2 * H])
    lf_rows = _log_sigmoid(grow[gi0 + H:gi0 + 2 * H, :])
    b_cols = jnp.dot(tri, lf_cols, preferred_element_type=F32, precision=lax.Precision.HIGHEST)
    b_rows = jnp.dot(lf_rows, tri_t, preferred_element_type=F32, precision=lax.Precision.HIGHEST)
    lane = lax.broadcasted_iota(jnp.int32, (T, dh), 1)

    for h in range(H):
        c0, c1 = h * dh, (h + 1) * dh
        q, k, v = q_ref[0, :, c0:c1], k_ref[0, :, c0:c1], v_ref[0, :, c0:c1]
        bc, br = b_cols[:, h:h + 1], b_rows[h:h + 1, :]
        ic, ir = gcol[:, gi0 + h:gi0 + h + 1], grow[gi0 + h:gi0 + h + 1, :]
        g_tot = jnp.sum(lf_cols[:, h:h + 1], axis=0, keepdims=True)
        m_prev = m_ref[h][0:1, 0:1]
        state = c_ref[h]

        logd = jnp.where(feeds, bc - br + ir, NEG)
        inter = bc + m_prev
        m_t = jnp.maximum(inter, jnp.max(logd, axis=-1, keepdims=True))
        w_intra = jnp.exp(logd - m_t)
        w_inter = jnp.exp(inter - m_t)
        sqk = w_intra * _dot_nt(q, k)
        qs = _dot(q, state.astype(BF16))
        num = w_inter * qs[:, :dh] + _dot(sqk.astype(BF16), v)
        den = w_inter * qs[:, dh:dh + 1] + jnp.sum(sqk, axis=-1, keepdims=True)
        hout = num / jnp.maximum(jnp.abs(den), jnp.exp(-m_t))

        a = g_tot - bc + ic
        ma = jnp.max(a, axis=0, keepdims=True)
        m_new = jnp.maximum(g_tot + m_prev, ma)
        wa = jnp.exp(a - ma)
        rhs = jnp.concatenate([(v.astype(F32) * wa).astype(BF16),
                               jnp.where(lane == 0, wa, 0.0).astype(BF16)], axis=1)
        c_ref[h] = jnp.exp(g_tot + m_prev - m_new) * state + jnp.exp(ma - m_new) * _dot_tn(k, rhs)
        m_ref[h] = jnp.broadcast_to(m_new, m_ref.shape[1:])

        if finalize:
            hc = _sigmoid(oc_ref[0, :, c0:c1].astype(F32)) * (hout + hf_ref[0, :, c0:c1])
            mu = jnp.mean(hc, axis=-1, keepdims=True)
            cen = hc - mu
            var = jnp.mean(cen * cen, axis=-1, keepdims=True)
            out_ref[0, :, c0:c1] = (cen * lax.rsqrt(var + EPS) * gain_ref[:, c0:c1]).astype(BF16)
        else:
            out_ref[0, :, c0:c1] = hout


def _mlstm_pass(q, k, v, g, gt, extra, *, reverse):
    B, L, W = q.shape
    T = MLSTM_CHUNK
    n = L // T
    pos = (lambda i: n - 1 - i) if reverse else (lambda i: i)
    tile = lambda c: pl.BlockSpec((1, T, c), lambda b, i: (b, pos(i), 0))
    in_specs = [tile(W), tile(W), tile(W), tile(MLSTM_GATES),
                pl.BlockSpec((1, MLSTM_GATES, T), lambda b, i: (b, 0, pos(i)))]
    finalize = extra is not None
    args = [q, k, v, g, gt]
    if finalize:
        in_specs += [tile(W), tile(W), _resident((1, W))]
        args += list(extra)
    return pl.pallas_call(
        functools.partial(_mlstm_kernel, reverse=reverse, finalize=finalize),
        grid=(B, n),
        in_specs=in_specs,
        out_specs=tile(W),
        out_shape=jax.ShapeDtypeStruct((B, L, W), BF16 if finalize else F32),
        scratch_shapes=[pltpu.VMEM((N_HEADS, HEAD_DIM, 2 * HEAD_DIM), F32), pltpu.VMEM((N_HEADS, 8, 128), F32)],
        compiler_params=_params("parallel", "arbitrary"),
    )(*args)


def _merge_kernel(x_ref, h_ref, ya_ref, yb_ref, yc_ref, yd_ref, wgp_ref, wmg_ref, wbr_ref, wout_ref, g_ref, out_ref):
    W, D = BRANCH_WIDTH, D_MODEL
    h = h_ref[...]
    acc = None
    for g, br_ref in enumerate((ya_ref, yb_ref, yc_ref, yd_ref)):
        gp = _dot(h, wgp_ref[:, g * W:(g + 1) * W])
        gated = (br_ref[...].astype(F32) * (gp * _sigmoid(gp))).astype(BF16)
        proj = _dot(gated, wbr_ref[g])
        term = _sigmoid(_dot(h, wmg_ref[:, g * D:(g + 1) * D])) * proj
        acc = term if acc is None else acc + term
    out = _dot(acc.astype(BF16), wout_ref[...])
    ms = jnp.mean(out * out, axis=-1, keepdims=True)
    out_ref[...] = x_ref[...] + out * lax.rsqrt(ms + EPS) * g_ref[...]


def _merge(x_rows, h_rows, branches, w_gp, w_mg, w_br, w_out, gain):
    R, D = x_rows.shape
    T, W = ROW_TILE, BRANCH_WIDTH
    rows = lambda c: pl.BlockSpec((T, c), lambda i: (i, 0))
    return pl.pallas_call(
        _merge_kernel,
        grid=(R // T,),
        in_specs=[rows(D), rows(D)] + [rows(W)] * N_BRANCHES
        + [_resident(w_gp.shape), _resident(w_mg.shape), _resident(w_br.shape), _resident(w_out.shape),
           _resident((1, D))],
        out_specs=rows(D),
        out_shape=jax.ShapeDtypeStruct((R, D), F32),
        compiler_params=_params("parallel"),
    )(x_rows, h_rows, *branches, w_gp, w_mg, w_br, w_out, gain.reshape(1, D))


def _t5_bucket(rel):
    nb = REL_BUCKETS // 2
    max_exact = nb // 2
    ret = (rel > 0).astype(np.int32) * nb
    n = np.abs(rel)
    large = max_exact + (np.log(np.maximum(n, 1) / max_exact) / np.log(REL_MAX_DIST / max_exact)
                         * (nb - max_exact)).astype(np.int32)
    large = np.minimum(large, nb - 1)
    return (ret + np.where(n < max_exact, n, large)).astype(np.int32)


def _band_bias(rel_bias, group, dilation):
    qb, hw = ATTN_Q_BLOCK, HALF_WINDOW
    rel = np.arange(qb + 2 * hw)[None, :] - hw - np.arange(qb)[:, None]
    heads = rel_bias[:, group * N_HEADS:(group + 1) * N_HEADS]
    bias = jnp.transpose(heads[_t5_bucket(dilation * rel)], (2, 0, 1)).astype(F32)
    return jnp.where(jnp.asarray(np.abs(rel) <= hw)[None], bias, NEG)


def _split_weights(w_in, conv_qk, w_mem_kv, w_branch, w_out):
    W, D, A = BRANCH_WIDTH, D_MODEL, A_QKV_WIDTH
    off = np.cumsum([0, A, A, A, W, W, W, W, W, MLSTM_GATES, W, N_BRANCHES * W, N_BRANCHES * D])
    seg = lambda j: w_in[:, off[j]:off[j + 1]]
    qa, ka, va = seg(0), seg(1), seg(2)
    w_a = [jnp.concatenate([t[:, g * W:(g + 1) * W] for t in (qa, ka, va)], axis=1).astype(BF16)
           for g in range(len(DILATIONS))]
    return dict(
        a=w_a,
        bd=jnp.concatenate([seg(3), seg(9)], axis=1).astype(BF16),
        qk=jnp.concatenate([seg(4), seg(5)], axis=1).astype(BF16),
        vo=jnp.concatenate([seg(6), seg(7)], axis=1).astype(BF16),
        g=seg(8).astype(BF16),
        gp=seg(10).astype(BF16),
        mg=seg(11).astype(BF16),
        conv=conv_qk.astype(F32),
        memkv=w_mem_kv.astype(BF16),
        br=w_branch.astype(BF16),
        out=w_out.astype(BF16),
    )


def _layer(x, mem, biases, fft_tables, wts, norm_pre, gate_bias, head_gain, mem_norm, norm_post):
    B, L, D = x.shape
    W = BRANCH_WIDTH
    h, h4, h16 = _prenorm(x, norm_pre)

    outs, lses = [], []
    for g, (d, hp) in enumerate(zip(DILATIONS, (h, h4, h16))):
        q, kv = _proj_a(hp.reshape(B * L, D), wts["a"][g])
        o, lse = _attention(q.reshape(B * d, L // d, W), kv.reshape(B * d, L // d, 2 * W), biases[g])
        outs.append(o.reshape(B, d, L // d, W))
        lses.append(lse.reshape(B, d, L // d, N_HEADS))
    ya = _combine(*outs, *lses)

    kv_mem = _memkv(mem, mem_norm, wts["memkv"])
    xb, yd = _proj_bd(h, wts["bd"], kv_mem)
    yb = _fnet(xb, fft_tables)

    qm, km, vm, oc, gates = _proj_c(h, wts["qk"], wts["vo"], wts["g"], wts["conv"], gate_bias.reshape(-1))
    gates_t = jnp.swapaxes(gates, 1, 2)
    h_fwd = _mlstm_pass(qm, km, vm, gates, gates_t, None, reverse=False)
    yc = _mlstm_pass(qm, km, vm, gates, gates_t, (h_fwd, oc, head_gain.reshape(1, W).astype(F32)), reverse=True)

    rows = lambda t: t.reshape(B * L, t.shape[-1])
    y = _merge(rows(x), rows(h), [rows(ya), rows(yb), rows(yc), rows(yd)],
               wts["gp"], wts["mg"], wts["br"], wts["out"], norm_post)
    return y.reshape(B, L, D)


def kernel(x_prompt, x_sample, mem_prompt, mem_sample, rel_bias, norm_pre, w_in, conv_qk, mlstm_gate_bias,
           mlstm_head_gain, mem_norm, w_mem_kv, w_branch, w_out, norm_post):
    depth = w_in.shape[0]
    biases = [_band_bias(rel_bias, g, d) for g, d in enumerate(DILATIONS)]
    layer_wts = [_split_weights(w_in[l], conv_qk[l], w_mem_kv[l], w_branch[l], w_out[l]) for l in range(depth)]
    outs = []
    for x, mem in ((x_prompt, mem_prompt), (x_sample, mem_sample)):
        tables = _fnet_tables(x.shape[1])
        for l in range(depth):
            x = _layer(x, mem, biases, tables, layer_wts[l], norm_pre[l], mlstm_gate_bias[l],
                       mlstm_head_gain[l], mem_norm[l], norm_post[l])
        outs.append(x)
    return tuple(outs)
```

```python
import functools
import math

import numpy as np
import jax
import jax.numpy as jnp
from jax import lax
from jax.experimental import pallas as pl
from jax.experimental.pallas import tpu as pltpu

D_MODEL = 1024
BRANCH_WIDTH = 512
HEAD_DIM = 128
N_HEADS = 4
N_BRANCHES = 4
DILATIONS = (1, 4, 16)
HALF_WINDOW = 64
A_QKV_WIDTH = len(DILATIONS) * BRANCH_WIDTH
REL_BUCKETS = 32
REL_MAX_DIST = 1024
FNET_GROUP_WIDTH = 128
MLSTM_CONV = 5
MLSTM_GATES = 16
N_MEM = 256
EPS = 1e-6
NEG = -1e30

F32 = jnp.float32
BF16 = jnp.bfloat16
LANES = 128
LSE_LANES_PER_HEAD = LANES // N_HEADS

VMEM_LIMIT_BYTES = 56 * 1024 * 1024
PERM_TILE = 1024
ROW_TILE = 512
ATTN_Q_BLOCK = 128
MLSTM_CHUNK = 256
FFT_LANE_TILE = 2048
CONV_HALO = 16


def _params(*sem):
    return pltpu.CompilerParams(dimension_semantics=sem, vmem_limit_bytes=VMEM_LIMIT_BYTES)


def _resident(shape):
    nd = len(shape)
    return pl.BlockSpec(shape, lambda *_: (0,) * nd, pipeline_mode=pl.Buffered(1))


def _sigmoid(x):
    return 1.0 / (1.0 + jnp.exp(-x))


def _log_sigmoid(x):
    return jnp.minimum(x, 0.0) - jnp.log(1.0 + jnp.exp(-jnp.abs(x)))


def _dot(a, b):
    return jnp.dot(a, b, preferred_element_type=F32)


def _dot_nt(a, b):
    return lax.dot_general(a, b, (((1,), (1,)), ((), ())), preferred_element_type=F32)


def _dot_tn(a, b):
    return lax.dot_general(a, b, (((0,), (0,)), ((), ())), preferred_element_type=F32)


def _prenorm_kernel(x_ref, g_ref, h_ref, h4_ref, h16_ref, xs_ref):
    T = PERM_TILE
    x = x_ref[0]
    ms = jnp.mean(x * x, axis=-1, keepdims=True)
    xn = x * lax.rsqrt(ms + EPS) * g_ref[...]
    h_ref[0] = xn.astype(BF16)
    for c in range(D_MODEL // LANES):
        xs_ref[c] = xn[:, c * LANES:(c + 1) * LANES]
    for d, out_ref in ((4, h4_ref), (16, h16_ref)):
        for r in range(d):
            for c in range(D_MODEL // LANES):
                out_ref[0, r, :, c * LANES:(c + 1) * LANES] = xs_ref[c, pl.ds(r, T // d, stride=d), :].astype(BF16)


def _prenorm(x, gain):
    B, L, D = x.shape
    T = PERM_TILE
    return pl.pallas_call(
        _prenorm_kernel,
        name="prenorm",
        grid=(B, L // T),
        in_specs=[pl.BlockSpec((1, T, D), lambda b, i: (b, i, 0)),
                  pl.BlockSpec((1, D), lambda b, i: (0, 0))],
        out_specs=[pl.BlockSpec((1, T, D), lambda b, i: (b, i, 0)),
                   pl.BlockSpec((1, 4, T // 4, D), lambda b, i: (b, 0, i, 0)),
                   pl.BlockSpec((1, 16, T // 16, D), lambda b, i: (b, 0, i, 0))],
        out_shape=[jax.ShapeDtypeStruct((B, L, D), BF16),
                   jax.ShapeDtypeStruct((B, 4, L // 4, D), BF16),
                   jax.ShapeDtypeStruct((B, 16, L // 16, D), BF16)],
        scratch_shapes=[pltpu.VMEM((D // LANES, T, LANES), F32)],
        compiler_params=_params("parallel", "parallel"),
    )(x, gain.reshape(1, D))


def _proj_a_kernel(h_ref, w_ref, q_ref, kv_ref):
    z = _dot(h_ref[...], w_ref[...])
    q_ref[...] = z[:, :BRANCH_WIDTH].astype(BF16)
    kv_ref[...] = z[:, BRANCH_WIDTH:].astype(BF16)


def _proj_a(h_rows, w_qkv):
    R, D = h_rows.shape
    T = ROW_TILE
    W = BRANCH_WIDTH
    return pl.pallas_call(
        _proj_a_kernel,
        name="proj_a",
        grid=(R // T,),
        in_specs=[pl.BlockSpec((T, D), lambda i: (i, 0)), _resident((D, 3 * W))],
        out_specs=[pl.BlockSpec((T, W), lambda i: (i, 0)), pl.BlockSpec((T, 2 * W), lambda i: (i, 0))],
        out_shape=[jax.ShapeDtypeStruct((R, W), BF16), jax.ShapeDtypeStruct((R, 2 * W), BF16)],
        compiler_params=_params("parallel"),
    )(h_rows, w_qkv)


def _attn_kernel(q_ref, kv_ref, kvp_ref, kvn_ref, bias_ref, o_ref, lse_ref, win_ref, *, seq_len, tq):
    i = pl.program_id(1)
    hw, qb, W = HALF_WINDOW, ATTN_Q_BLOCK, BRANCH_WIDTH
    kb = qb + 2 * hw
    win_ref[0:hw] = kvp_ref[0]
    win_ref[hw:hw + tq] = kv_ref[0]
    win_ref[hw + tq:] = kvn_ref[0]
    scale = HEAD_DIM ** -0.5
    lane_head = lax.broadcasted_iota(jnp.int32, (qb, LANES), 1) // LSE_LANES_PER_HEAD
    for j in range(tq // qb):
        a = j * qb
        kidx = i * tq + (a - hw) + lax.broadcasted_iota(jnp.int32, (1, kb), 1)
        valid = (kidx >= 0) & (kidx < seq_len)
        lse_all = None
        for h in range(N_HEADS):
            c0, c1 = h * HEAD_DIM, (h + 1) * HEAD_DIM
            q = q_ref[0, a:a + qb, c0:c1]
            k = win_ref[a:a + kb, c0:c1]
            v = win_ref[a:a + kb, W + c0:W + c1]
            s = _dot_nt(q, k) * scale + bias_ref[h]
            s = jnp.where(valid, s, NEG)
            m = jnp.max(s, axis=-1, keepdims=True)
            p = jnp.exp(s - m)
            l = jnp.sum(p, axis=-1, keepdims=True)
            o = _dot(p.astype(BF16), v) / l
            o_ref[0, a:a + qb, c0:c1] = o.astype(BF16)
            lse = jnp.broadcast_to(m + jnp.log(l), (qb, LANES))
            lse_all = lse if h == 0 else jnp.where(lane_head == h, lse, lse_all)
        lse_ref[0, a:a + qb, :] = lse_all


def _attention(q, kv, bias):
    S, M, W = q.shape
    hw = HALF_WINDOW
    tq = min(M, 512)
    nhb = M // hw
    step = tq // hw
    kern = functools.partial(_attn_kernel, seq_len=M, tq=tq)
    return pl.pallas_call(
        kern,
        name="attn",
        grid=(S, M // tq),
        in_specs=[pl.BlockSpec((1, tq, W), lambda s, i: (s, i, 0)),
                  pl.BlockSpec((1, tq, 2 * W), lambda s, i: (s, i, 0)),
                  pl.BlockSpec((1, hw, 2 * W), lambda s, i: (s, jnp.maximum(i * step - 1, 0), 0)),
                  pl.BlockSpec((1, hw, 2 * W), lambda s, i: (s, jnp.minimum((i + 1) * step, nhb - 1), 0)),
                  _resident(bias.shape)],
        out_specs=[pl.BlockSpec((1, tq, W), lambda s, i: (s, i, 0)),
                   pl.BlockSpec((1, tq, LANES), lambda s, i: (s, i, 0))],
        out_shape=[jax.ShapeDtypeStruct((S, M, W), BF16), jax.ShapeDtypeStruct((S, M, LANES), F32)],
        scratch_shapes=[pltpu.VMEM((tq + 2 * hw, 2 * W), BF16)],
        compiler_params=_params("parallel", "parallel"),
    )(q, kv, kv, kv, bias)


def _combine_kernel(o1_ref, o4_ref, o16_ref, l1_ref, l4_ref, l16_ref, ya_ref, os_ref, ls_ref):
    T, nc = PERM_TILE, BRANCH_WIDTH // LANES
    for g, (d, o_ref, l_ref) in enumerate(((4, o4_ref, l4_ref), (16, o16_ref, l16_ref))):
        for r in range(d):
            ls_ref[g, pl.ds(r, T // d, stride=d), :] = l_ref[0, r]
            for c in range(nc):
                os_ref[g, c, pl.ds(r, T // d, stride=d), :] = o_ref[0, r, :, c * LANES:(c + 1) * LANES].astype(F32)
    l1, l4, l16 = l1_ref[0, 0], ls_ref[0], ls_ref[1]
    m = jnp.maximum(jnp.maximum(l1, l4), l16)
    e1, e4, e16 = jnp.exp(l1 - m), jnp.exp(l4 - m), jnp.exp(l16 - m)
    inv = 1.0 / (e1 + e4 + e16)
    w1, w4, w16 = e1 * inv, e4 * inv, e16 * inv
    for h in range(N_HEADS):
        c0, c1 = h * HEAD_DIM, (h + 1) * HEAD_DIM
        lh = h * LSE_LANES_PER_HEAD
        ya = (w1[:, lh:lh + 1] * o1_ref[0, 0, :, c0:c1].astype(F32)
              + w4[:, lh:lh + 1] * os_ref[0, h] + w16[:, lh:lh + 1] * os_ref[1, h])
        ya_ref[0, :, c0:c1] = ya.astype(BF16)


def _combine(o1, o4, o16, l1, l4, l16):
    B, _, L, W = o1.shape
    T = PERM_TILE

    def spec(d, c):
        return pl.BlockSpec((1, d, T // d, c), lambda b, i: (b, 0, i, 0))

    return pl.pallas_call(
        _combine_kernel,
        name="combine",
        grid=(B, L // T),
        in_specs=[spec(1, W), spec(4, W), spec(16, W), spec(1, LANES), spec(4, LANES), spec(16, LANES)],
        out_specs=pl.BlockSpec((1, T, W), lambda b, i: (b, i, 0)),
        out_shape=jax.ShapeDtypeStruct((B, L, W), BF16),
        scratch_shapes=[pltpu.VMEM((2, W // LANES, T, LANES), F32), pltpu.VMEM((2, T, LANES), F32)],
        compiler_params=_params("parallel", "parallel"),
    )(o1, o4, o16, l1, l4, l16)


def _memkv_kernel(mem_ref, g_ref, w_ref, kv_ref):
    x = mem_ref[0]
    ms = jnp.mean(x * x, axis=-1, keepdims=True)
    xn = (x * lax.rsqrt(ms + EPS) * g_ref[...]).astype(BF16)
    kv_ref[0] = _dot(xn, w_ref[...]).astype(BF16)


def _memkv(mem, gain, w):
    B, N, D = mem.shape
    W2 = w.shape[1]
    return pl.pallas_call(
        _memkv_kernel,
        name="memkv",
        grid=(B,),
        in_specs=[pl.BlockSpec((1, N, D), lambda b: (b, 0, 0)), pl.BlockSpec((1, D), lambda b: (0, 0)),
                  _resident((D, W2))],
        out_specs=pl.BlockSpec((1, N, W2), lambda b: (b, 0, 0)),
        out_shape=jax.ShapeDtypeStruct((B, N, W2), BF16),
        compiler_params=_params("parallel"),
    )(mem, gain.reshape(1, D), w)


def _proj_bd_kernel(h_ref, w_ref, kv_ref, xb_ref, yd_ref):
    W = BRANCH_WIDTH
    z = _dot(h_ref[0], w_ref[...])
    xb_ref[0] = z[:, :W].astype(BF16)
    qd = z[:, W:].astype(BF16)
    scale = HEAD_DIM ** -0.5
    for h in range(N_HEADS):
        c0, c1 = h * HEAD_DIM, (h + 1) * HEAD_DIM
        s = _dot_nt(qd[:, c0:c1], kv_ref[0, :, c0:c1]) * scale
        m = jnp.max(s, axis=-1, keepdims=True)
        p = jnp.exp(s - m)
        l = jnp.sum(p, axis=-1, keepdims=True)
        o = _dot(p.astype(BF16), kv_ref[0, :, W + c0:W + c1]) / l
        yd_ref[0, :, c0:c1] = o.astype(BF16)


def _proj_bd(h, w_bd, kv):
    B, L, D = h.shape
    T, W = ROW_TILE, BRANCH_WIDTH
    return pl.pallas_call(
        _proj_bd_kernel,
        name="proj_bd",
        grid=(B, L // T),
        in_specs=[pl.BlockSpec((1, T, D), lambda b, i: (b, i, 0)), _resident((D, 2 * W)),
                  pl.BlockSpec((1, N_MEM, 2 * W), lambda b, i: (b, 0, 0))],
        out_specs=[pl.BlockSpec((1, T, W), lambda b, i: (b, i, 0)), pl.BlockSpec((1, T, W), lambda b, i: (b, i, 0))],
        out_shape=[jax.ShapeDtypeStruct((B, L, W), BF16), jax.ShapeDtypeStruct((B, L, W), BF16)],
        compiler_params=_params("parallel", "parallel"),
    )(h, w_bd, kv)


def _fft_factors(L):
    n1 = 1 << ((L.bit_length() - 1 + 1) // 2)
    return n1, L // n1


def _dft_cos_sin(n):
    ang = 2.0 * np.pi * ((np.arange(n)[:, None] * np.arange(n)[None, :]) % n) / n
    return np.cos(ang), np.sin(ang)


def _fft1_kernel(x_ref, f_ref, tc_ref, ts_ref, yr_ref, yi_ref, *, n1):
    y = _dot(f_ref[...], x_ref[0])
    yr, yi = y[:n1], y[n1:]
    reps = BRANCH_WIDTH // FNET_GROUP_WIDTH
    tc = jnp.concatenate([tc_ref[0]] * reps, axis=1)
    ts = jnp.concatenate([ts_ref[0]] * reps, axis=1)
    yr_ref[0, 0] = (yr * tc + yi * ts).astype(BF16)
    yi_ref[0, 0] = (yi * tc - yr * ts).astype(BF16)


def _fft2_kernel(yr_ref, yi_ref, g1_ref, g2_ref, cs_ref, out_ref, *, n2, norm):
    u = _dot(g1_ref[...], yr_ref[0]) + _dot(g2_ref[...], yi_ref[0])
    top, bot = u[:n2].astype(BF16), u[n2:].astype(BF16)
    gw = FNET_GROUP_WIDTH
    ng = FFT_LANE_TILE // gw
    lhs = jnp.concatenate(
        [jnp.concatenate([top[:, j * gw:(j + 1) * gw], bot[:, j * gw:(j + 1) * gw]], axis=1) for j in range(ng)],
        axis=0)
    res = _dot(lhs, cs_ref[...]) * norm
    for j in range(ng):
        out_ref[0, :, j * gw:(j + 1) * gw] = res[j * n2:(j + 1) * n2].astype(BF16)


def _fnet(xb, tables):
    B, L, W = xb.shape
    n1, n2 = _fft_factors(L)
    f1, twc, tws, g1, g2, cs = tables
    yr, yi = pl.pallas_call(
        functools.partial(_fft1_kernel, n1=n1),
        name="fft1",
        grid=(B, n2),
        in_specs=[pl.BlockSpec((1, n1, W), lambda b, j: (b, 0, j)), _resident((2 * n1, n1)),
                  pl.BlockSpec((1, n1, FNET_GROUP_WIDTH), lambda b, j: (j, 0, 0)),
                  pl.BlockSpec((1, n1, FNET_GROUP_WIDTH), lambda b, j: (j, 0, 0))],
        out_specs=[pl.BlockSpec((1, 1, n1, W), lambda b, j: (b, j, 0, 0))] * 2,
        out_shape=[jax.ShapeDtypeStruct((B, n2, n1, W), BF16)] * 2,
        compiler_params=_params("parallel", "parallel"),
    )(xb.reshape(B, n1, n2 * W), f1, twc, tws)
    tn = FFT_LANE_TILE
    out = pl.pallas_call(
        functools.partial(_fft2_kernel, n2=n2, norm=float(1.0 / math.sqrt(L * FNET_GROUP_WIDTH))),
        name="fft2",
        grid=(B, n1 * W // tn),
        in_specs=[pl.BlockSpec((1, n2, tn), lambda b, j: (b, 0, j))] * 2
        + [_resident((2 * n2, n2)), _resident((2 * n2, n2)), _resident((2 * FNET_GROUP_WIDTH, FNET_GROUP_WIDTH))],
        out_specs=pl.BlockSpec((1, n2, tn), lambda b, j: (b, 0, j)),
        out_shape=jax.ShapeDtypeStruct((B, n2, n1 * W), BF16),
        compiler_params=_params("parallel", "parallel"),
    )(yr.reshape(B, n2, n1 * W), yi.reshape(B, n2, n1 * W), g1, g2, cs)
    return out.reshape(B, L, W)


def _fnet_tables(L):
    n1, n2 = _fft_factors(L)
    c1, s1 = _dft_cos_sin(n1)
    c2, s2 = _dft_cos_sin(n2)
    cc, sc = _dft_cos_sin(FNET_GROUP_WIDTH)
    ang = 2.0 * np.pi * (np.arange(n2)[:, None] * np.arange(n1)[None, :]) / L
    bcast = lambda t: jnp.asarray(np.broadcast_to(t[:, :, None], (n2, n1, FNET_GROUP_WIDTH)), F32)
    return (jnp.asarray(np.concatenate([c1, -s1], axis=0), BF16), bcast(np.cos(ang)), bcast(np.sin(ang)),
            jnp.asarray(np.concatenate([c2, -s2], axis=0), BF16), jnp.asarray(np.concatenate([s2, c2], axis=0), BF16),
            jnp.asarray(np.concatenate([cc, sc], axis=0), BF16))


def _proj_c_kernel(h_ref, hp_ref, hn_ref, wqk_ref, wvo_ref, wg_ref, conv_ref, gb_ref,
                   q_ref, k_ref, v_ref, o_ref, g_ref, ext_ref):
    i = pl.program_id(1)
    T, W, halo = ROW_TILE, BRANCH_WIDTH, CONV_HALO
    h = h_ref[0]
    has_prev = (i > 0).astype(F32)
    has_next = (i < pl.num_programs(1) - 1).astype(F32)
    ext_ref[0:halo] = _dot(hp_ref[0], wqk_ref[...]) * has_prev
    ext_ref[halo:halo + T] = _dot(h, wqk_ref[...])
    ext_ref[halo + T:] = _dot(hn_ref[0], wqk_ref[...]) * has_next
    pad = MLSTM_CONV // 2
    acc = ext_ref[pl.ds(halo - pad, T), :] * conv_ref[0:1, :]
    for j in range(1, MLSTM_CONV):
        acc = acc + ext_ref[pl.ds(halo - pad + j, T), :] * conv_ref[j:j + 1, :]
    y = acc * _sigmoid(acc)
    q_ref[0] = (y[:, :W] * (HEAD_DIM ** -0.5)).astype(BF16)
    k_ref[0] = y[:, W:].astype(BF16)
    vo = _dot(h, wvo_ref[...])
    v_ref[0] = vo[:, :W].astype(BF16)
    o_ref[0] = vo[:, W:].astype(BF16)
    g_ref[0] = _dot(h, wg_ref[...]) + gb_ref[...]


def _proj_c(h, w_qk, w_vo, w_g, conv, gate_bias):
    B, L, D = h.shape
    T, W, halo = ROW_TILE, BRANCH_WIDTH, CONV_HALO
    step = T // halo
    nhb = L // halo
    tile = lambda c: pl.BlockSpec((1, T, c), lambda b, i: (b, i, 0))
    return pl.pallas_call(
        _proj_c_kernel,
        name="proj_c",
        grid=(B, L // T),
        in_specs=[tile(D),
                  pl.BlockSpec((1, halo, D), lambda b, i: (b, jnp.maximum(i * step - 1, 0), 0)),
                  pl.BlockSpec((1, halo, D), lambda b, i: (b, jnp.minimum((i + 1) * step, nhb - 1), 0)),
                  _resident((D, 2 * W)), _resident((D, 2 * W)), _resident((D, MLSTM_GATES)),
                  _resident((MLSTM_CONV, 2 * W)), _resident((1, MLSTM_GATES))],
        out_specs=[tile(W), tile(W), tile(W), tile(W), tile(MLSTM_GATES)],
        out_shape=[jax.ShapeDtypeStruct((B, L, W), BF16)] * 4 + [jax.ShapeDtypeStruct((B, L, MLSTM_GATES), F32)],
        scratch_shapes=[pltpu.VMEM((T + 2 * halo, 2 * W), F32)],
        compiler_params=_params("parallel", "arbitrary"),
    )(h, h, h, w_qk, w_vo, w_g, conv, gate_bias.reshape(1, MLSTM_GATES))


def _mlstm_kernel(*refs, reverse, finalize):
    if finalize:
        (q_ref, k_ref, v_ref, g_ref, gt_ref, hf_ref, oc_ref, gain_ref, out_ref, c_ref, m_ref) = refs
    else:
        (q_ref, k_ref, v_ref, g_ref, gt_ref, out_ref, c_ref, m_ref) = refs
    T, dh, H = MLSTM_CHUNK, HEAD_DIM, N_HEADS

    @pl.when(pl.program_id(1) == 0)
    def _():
        c_ref[...] = jnp.zeros_like(c_ref)
        m_ref[...] = jnp.zeros_like(m_ref)

    gi0 = 8 if reverse else 0
    gcol = g_ref[0]
    grow = gt_ref[0]
    row = lax.broadcasted_iota(jnp.int32, (T, T), 0)
    col = lax.broadcasted_iota(jnp.int32, (T, T), 1)
    feeds = (col >= row) if reverse else (col <= row)
    tri = feeds.astype(F32)
    tri_t = ((row >= col) if reverse else (row <= col)).astype(F32)
    lf_cols = _log_sigmoid(gcol[:, gi0 + H:gi0 + 2 * H])
    lf_rows = _log_sigmoid(grow[gi0 + H:gi0 + 2 * H, :])
    b_cols = jnp.dot(tri, lf_cols, preferred_element_type=F32, precision=lax.Precision.HIGHEST)
    b_rows = jnp.dot(lf_rows, tri_t, preferred_element_type=F32, precision=lax.Precision.HIGHEST)
    lane = lax.broadcasted_iota(jnp.int32, (T, dh), 1)

    for h in range(H):
        c0, c1 = h * dh, (h + 1) * dh
        q, k, v = q_ref[0, :, c0:c1], k_ref[0, :, c0:c1], v_ref[0, :, c0:c1]
        bc, br = b_cols[:, h:h + 1], b_rows[h:h + 1, :]
        ic, ir = gcol[:, gi0 + h:gi0 + h + 1], grow[gi0 + h:gi0 + h + 1, :]
        g_tot = jnp.sum(lf_cols[:, h:h + 1], axis=0, keepdims=True)
        m_prev = m_ref[h][0:1, 0:1]
        state = c_ref[h]

        logd = jnp.where(feeds, bc - br + ir, NEG)
        inter = bc + m_prev
        m_t = jnp.maximum(inter, jnp.max(logd, axis=-1, keepdims=True))
        w_intra = jnp.exp(logd - m_t)
        w_inter = jnp.exp(inter - m_t)
        sqk = w_intra * _dot_nt(q, k)
        qs = _dot(q, state.astype(BF16))
        num = w_inter * qs[:, :dh] + _dot(sqk.astype(BF16), v)
        den = w_inter * qs[:, dh:dh + 1] + jnp.sum(sqk, axis=-1, keepdims=True)
        hout = num / jnp.maximum(jnp.abs(den), jnp.exp(-m_t))

        a = g_tot - bc + ic
        ma = jnp.max(a, axis=0, keepdims=True)
        m_new = jnp.maximum(g_tot + m_prev, ma)
        wa = jnp.exp(a - ma)
        rhs = jnp.concatenate([(v.astype(F32) * wa).astype(BF16),
                               jnp.where(lane == 0, wa, 0.0).astype(BF16)], axis=1)
        c_ref[h] = jnp.exp(g_tot + m_prev - m_new) * state + jnp.exp(ma - m_new) * _dot_tn(k, rhs)
        m_ref[h] = jnp.broadcast_to(m_new, m_ref.shape[1:])

        if finalize:
            hc = _sigmoid(oc_ref[0, :, c0:c1].astype(F32)) * (hout + hf_ref[0, :, c0:c1])
            mu = jnp.mean(hc, axis=-1, keepdims=True)
            cen = hc - mu
            var = jnp.mean(cen * cen, axis=-1, keepdims=True)
            out_ref[0, :, c0:c1] = (cen * lax.rsqrt(var + EPS) * gain_ref[:, c0:c1]).astype(BF16)
        else:
            out_ref[0, :, c0:c1] = hout


def _mlstm_pass(q, k, v, g, gt, extra, *, reverse):
    B, L, W = q.shape
    T = MLSTM_CHUNK
    n = L // T
    pos = (lambda i: n - 1 - i) if reverse else (lambda i: i)
    tile = lambda c: pl.BlockSpec((1, T, c), lambda b, i: (b, pos(i), 0))
    in_specs = [tile(W), tile(W), tile(W), tile(MLSTM_GATES),
                pl.BlockSpec((1, MLSTM_GATES, T), lambda b, i: (b, 0, pos(i)))]
    finalize = extra is not None
    args = [q, k, v, g, gt]
    if finalize:
        in_specs += [tile(W), tile(W), _resident((1, W))]
        args += list(extra)
    return pl.pallas_call(
        functools.partial(_mlstm_kernel, reverse=reverse, finalize=finalize),
        name="mlstm_bwd" if reverse else "mlstm_fwd",
        grid=(B, n),
        in_specs=in_specs,
        out_specs=tile(W),
        out_shape=jax.ShapeDtypeStruct((B, L, W), BF16 if finalize else F32),
        scratch_shapes=[pltpu.VMEM((N_HEADS, HEAD_DIM, 2 * HEAD_DIM), F32), pltpu.VMEM((N_HEADS, 8, 128), F32)],
        compiler_params=_params("parallel", "arbitrary"),
    )(*args)


def _merge_kernel(x_ref, h_ref, ya_ref, yb_ref, yc_ref, yd_ref, wgp_ref, wmg_ref, wbr_ref, wout_ref, g_ref, out_ref):
    W, D = BRANCH_WIDTH, D_MODEL
    h = h_ref[...]
    acc = None
    for g, br_ref in enumerate((ya_ref, yb_ref, yc_ref, yd_ref)):
        gp = _dot(h, wgp_ref[:, g * W:(g + 1) * W])
        gated = (br_ref[...].astype(F32) * (gp * _sigmoid(gp))).astype(BF16)
        proj = _dot(gated, wbr_ref[g])
        term = _sigmoid(_dot(h, wmg_ref[:, g * D:(g + 1) * D])) * proj
        acc = term if acc is None else acc + term
    out = _dot(acc.astype(BF16), wout_ref[...])
    ms = jnp.mean(out * out, axis=-1, keepdims=True)
    out_ref[...] = x_ref[...] + out * lax.rsqrt(ms + EPS) * g_ref[...]


def _merge(x_rows, h_rows, branches, w_gp, w_mg, w_br, w_out, gain):
    R, D = x_rows.shape
    T, W = ROW_TILE, BRANCH_WIDTH
    rows = lambda c: pl.BlockSpec((T, c), lambda i: (i, 0))
    return pl.pallas_call(
        _merge_kernel,
        name="merge",
        grid=(R // T,),
        in_specs=[rows(D), rows(D)] + [rows(W)] * N_BRANCHES
        + [_resident(w_gp.shape), _resident(w_mg.shape), _resident(w_br.shape), _resident(w_out.shape),
           _resident((1, D))],
        out_specs=rows(D),
        out_shape=jax.ShapeDtypeStruct((R, D), F32),
        compiler_params=_params("parallel"),
    )(x_rows, h_rows, *branches, w_gp, w_mg, w_br, w_out, gain.reshape(1, D))


def _t5_bucket(rel):
    nb = REL_BUCKETS // 2
    max_exact = nb // 2
    ret = (rel > 0).astype(np.int32) * nb
    n = np.abs(rel)
    large = max_exact + (np.log(np.maximum(n, 1) / max_exact) / np.log(REL_MAX_DIST / max_exact)
                         * (nb - max_exact)).astype(np.int32)
    large = np.minimum(large, nb - 1)
    return (ret + np.where(n < max_exact, n, large)).astype(np.int32)


def _band_bias(rel_bias, group, dilation):
    qb, hw = ATTN_Q_BLOCK, HALF_WINDOW
    rel = np.arange(qb + 2 * hw)[None, :] - hw - np.arange(qb)[:, None]
    heads = rel_bias[:, group * N_HEADS:(group + 1) * N_HEADS]
    bias = jnp.transpose(heads[_t5_bucket(dilation * rel)], (2, 0, 1)).astype(F32)
    return jnp.where(jnp.asarray(np.abs(rel) <= hw)[None], bias, NEG)


def _split_weights(w_in, conv_qk, w_mem_kv, w_branch, w_out):
    W, D, A = BRANCH_WIDTH, D_MODEL, A_QKV_WIDTH
    off = np.cumsum([0, A, A, A, W, W, W, W, W, MLSTM_GATES, W, N_BRANCHES * W, N_BRANCHES * D])
    seg = lambda j: w_in[:, off[j]:off[j + 1]]
    qa, ka, va = seg(0), seg(1), seg(2)
    w_a = [jnp.concatenate([t[:, g * W:(g + 1) * W] for t in (qa, ka, va)], axis=1).astype(BF16)
           for g in range(len(DILATIONS))]
    return dict(
        a=w_a,
        bd=jnp.concatenate([seg(3), seg(9)], axis=1).astype(BF16),
        qk=jnp.concatenate([seg(4), seg(5)], axis=1).astype(BF16),
        vo=jnp.concatenate([seg(6), seg(7)], axis=1).astype(BF16),
        g=seg(8).astype(BF16),
        gp=seg(10).astype(BF16),
        mg=seg(11).astype(BF16),
        conv=conv_qk.astype(F32),
        memkv=w_mem_kv.astype(BF16),
        br=w_branch.astype(BF16),
        out=w_out.astype(BF16),
    )


def _layer(x, mem, biases, fft_tables, wts, norm_pre, gate_bias, head_gain, mem_norm, norm_post):
    B, L, D = x.shape
    W = BRANCH_WIDTH
    h, h4, h16 = _prenorm(x, norm_pre)

    outs, lses = [], []
    for g, (d, hp) in enumerate(zip(DILATIONS, (h, h4, h16))):
        q, kv = _proj_a(hp.reshape(B * L, D), wts["a"][g])
        o, lse = _attention(q.reshape(B * d, L // d, W), kv.reshape(B * d, L // d, 2 * W), biases[g])
        outs.append(o.reshape(B, d, L // d, W))
        lses.append(lse.reshape(B, d, L // d, LANES))
    ya = _combine(*outs, *lses)

    kv_mem = _memkv(mem, mem_norm, wts["memkv"])
    xb, yd = _proj_bd(h, wts["bd"], kv_mem)
    yb = _fnet(xb, fft_tables)

    qm, km, vm, oc, gates = _proj_c(h, wts["qk"], wts["vo"], wts["g"], wts["conv"], gate_bias.reshape(-1))
    gates_t = jnp.swapaxes(gates, 1, 2)
    h_fwd = _mlstm_pass(qm, km, vm, gates, gates_t, None, reverse=False)
    yc = _mlstm_pass(qm, km, vm, gates, gates_t, (h_fwd, oc, head_gain.reshape(1, W).astype(F32)), reverse=True)

    rows = lambda t: t.reshape(B * L, t.shape[-1])
    y = _merge(rows(x), rows(h), [rows(ya), rows(yb), rows(yc), rows(yd)],
               wts["gp"], wts["mg"], wts["br"], wts["out"], norm_post)
    return y.reshape(B, L, D)


def kernel(x_prompt, x_sample, mem_prompt, mem_sample, rel_bias, norm_pre, w_in, conv_qk, mlstm_gate_bias,
           mlstm_head_gain, mem_norm, w_mem_kv, w_branch, w_out, norm_post):
    depth = w_in.shape[0]
    biases = [_band_bias(rel_bias, g, d) for g, d in enumerate(DILATIONS)]
    layer_wts = [_split_weights(w_in[l], conv_qk[l], w_mem_kv[l], w_branch[l], w_out[l]) for l in range(depth)]
    outs = []
    for x, mem in ((x_prompt, mem_prompt), (x_sample, mem_sample)):
        tables = _fnet_tables(x.shape[1])
        for l in range(depth):
            x = _layer(x, mem, biases, tables, layer_wts[l], norm_pre[l], mlstm_gate_bias[l],
                       mlstm_head_gain[l], mem_norm[l], norm_post[l])
        outs.append(x)
    return tuple(outs)
```

```python
import functools
import math

import numpy as np
import jax
import jax.numpy as jnp
from jax import lax
from jax.experimental import pallas as pl
from jax.experimental.pallas import tpu as pltpu

D_MODEL = 1024
BRANCH_WIDTH = 512
HEAD_DIM = 128
N_HEADS = 4
N_BRANCHES = 4
DILATIONS = (1, 4, 16)
HALF_WINDOW = 64
A_QKV_WIDTH = len(DILATIONS) * BRANCH_WIDTH
REL_BUCKETS = 32
REL_MAX_DIST = 1024
FNET_GROUP_WIDTH = 128
MLSTM_CONV = 5
MLSTM_GATES = 16
N_MEM = 256
EPS = 1e-6
NEG = -1e30

F32 = jnp.float32
BF16 = jnp.bfloat16
LANES = 128
LSE_LANES_PER_HEAD = LANES // N_HEADS

VMEM_LIMIT_BYTES = 56 * 1024 * 1024
PERM_TILE = 1024
ROW_TILE = 512
ATTN_Q_BLOCK = 128
MLSTM_CHUNK = 256
FFT_ROWS = 8
CONV_HALO = 16


def _params(*sem):
    return pltpu.CompilerParams(dimension_semantics=sem, vmem_limit_bytes=VMEM_LIMIT_BYTES)


def _resident(shape):
    nd = len(shape)
    return pl.BlockSpec(shape, lambda *_: (0,) * nd, pipeline_mode=pl.Buffered(1))


def _sigmoid(x):
    return 1.0 / (1.0 + jnp.exp(-x))


def _log_sigmoid(x):
    return jnp.minimum(x, 0.0) - jnp.log(1.0 + jnp.exp(-jnp.abs(x)))


def _dot(a, b):
    return jnp.dot(a, b, preferred_element_type=F32)


def _dot_nt(a, b):
    return lax.dot_general(a, b, (((1,), (1,)), ((), ())), preferred_element_type=F32)


def _dot_tn(a, b):
    return lax.dot_general(a, b, (((0,), (0,)), ((), ())), preferred_element_type=F32)


def _prenorm_kernel(x_ref, g_ref, h_ref, h4_ref, h16_ref, xs_ref):
    T = PERM_TILE
    x = x_ref[0]
    ms = jnp.mean(x * x, axis=-1, keepdims=True)
    xn = x * lax.rsqrt(ms + EPS) * g_ref[...]
    h_ref[0] = xn.astype(BF16)
    for c in range(D_MODEL // LANES):
        xs_ref[c] = xn[:, c * LANES:(c + 1) * LANES]
    for d, out_ref in ((4, h4_ref), (16, h16_ref)):
        for r in range(d):
            for c in range(D_MODEL // LANES):
                out_ref[0, r, :, c * LANES:(c + 1) * LANES] = xs_ref[c, pl.ds(r, T // d, stride=d), :].astype(BF16)


def _prenorm(x, gain):
    B, L, D = x.shape
    T = PERM_TILE
    return pl.pallas_call(
        _prenorm_kernel,
        name="prenorm",
        grid=(B, L // T),
        in_specs=[pl.BlockSpec((1, T, D), lambda b, i: (b, i, 0)),
                  pl.BlockSpec((1, D), lambda b, i: (0, 0))],
        out_specs=[pl.BlockSpec((1, T, D), lambda b, i: (b, i, 0)),
                   pl.BlockSpec((1, 4, T // 4, D), lambda b, i: (b, 0, i, 0)),
                   pl.BlockSpec((1, 16, T // 16, D), lambda b, i: (b, 0, i, 0))],
        out_shape=[jax.ShapeDtypeStruct((B, L, D), BF16),
                   jax.ShapeDtypeStruct((B, 4, L // 4, D), BF16),
                   jax.ShapeDtypeStruct((B, 16, L // 16, D), BF16)],
        scratch_shapes=[pltpu.VMEM((D // LANES, T, LANES), F32)],
        compiler_params=_params("parallel", "parallel"),
    )(x, gain.reshape(1, D))


def _proj_a_kernel(h_ref, w_ref, q_ref, kv_ref):
    z = _dot(h_ref[...], w_ref[...])
    q_ref[...] = z[:, :BRANCH_WIDTH].astype(BF16)
    kv_ref[...] = z[:, BRANCH_WIDTH:].astype(BF16)


def _proj_a(h_rows, w_qkv):
    R, D = h_rows.shape
    T = ROW_TILE
    W = BRANCH_WIDTH
    return pl.pallas_call(
        _proj_a_kernel,
        name="proj_a",
        grid=(R // T,),
        in_specs=[pl.BlockSpec((T, D), lambda i: (i, 0)), _resident((D, 3 * W))],
        out_specs=[pl.BlockSpec((T, W), lambda i: (i, 0)), pl.BlockSpec((T, 2 * W), lambda i: (i, 0))],
        out_shape=[jax.ShapeDtypeStruct((R, W), BF16), jax.ShapeDtypeStruct((R, 2 * W), BF16)],
        compiler_params=_params("parallel"),
    )(h_rows, w_qkv)


def _attn_kernel(q_ref, kv_ref, kvp_ref, kvn_ref, bias_ref, o_ref, lse_ref, win_ref, *, seq_len, tq):
    i = pl.program_id(1)
    hw, qb, W = HALF_WINDOW, ATTN_Q_BLOCK, BRANCH_WIDTH
    kb = qb + 2 * hw
    win_ref[0:hw] = kvp_ref[0]
    win_ref[hw:hw + tq] = kv_ref[0]
    win_ref[hw + tq:] = kvn_ref[0]
    scale = HEAD_DIM ** -0.5
    lane_head = lax.broadcasted_iota(jnp.int32, (qb, LANES), 1) // LSE_LANES_PER_HEAD
    for j in range(tq // qb):
        a = j * qb
        kidx = i * tq + (a - hw) + lax.broadcasted_iota(jnp.int32, (1, kb), 1)
        valid = (kidx >= 0) & (kidx < seq_len)
        lse_all = None
        for h in range(N_HEADS):
            c0, c1 = h * HEAD_DIM, (h + 1) * HEAD_DIM
            q = q_ref[0, a:a + qb, c0:c1]
            k = win_ref[a:a + kb, c0:c1]
            v = win_ref[a:a + kb, W + c0:W + c1]
            s = _dot_nt(q, k) * scale + bias_ref[h]
            s = jnp.where(valid, s, NEG)
            m = jnp.max(s, axis=-1, keepdims=True)
            p = jnp.exp(s - m)
            l = jnp.sum(p, axis=-1, keepdims=True)
            o = _dot(p.astype(BF16), v) / l
            o_ref[0, a:a + qb, c0:c1] = o.astype(BF16)
            lse = jnp.broadcast_to(m + jnp.log(l), (qb, LANES))
            lse_all = lse if h == 0 else jnp.where(lane_head == h, lse, lse_all)
        lse_ref[0, a:a + qb, :] = lse_all


def _attention(q, kv, bias):
    S, M, W = q.shape
    hw = HALF_WINDOW
    tq = min(M, 512)
    nhb = M // hw
    step = tq // hw
    kern = functools.partial(_attn_kernel, seq_len=M, tq=tq)
    return pl.pallas_call(
        kern,
        name="attn",
        grid=(S, M // tq),
        in_specs=[pl.BlockSpec((1, tq, W), lambda s, i: (s, i, 0)),
                  pl.BlockSpec((1, tq, 2 * W), lambda s, i: (s, i, 0)),
                  pl.BlockSpec((1, hw, 2 * W), lambda s, i: (s, jnp.maximum(i * step - 1, 0), 0)),
                  pl.BlockSpec((1, hw, 2 * W), lambda s, i: (s, jnp.minimum((i + 1) * step, nhb - 1), 0)),
                  _resident(bias.shape)],
        out_specs=[pl.BlockSpec((1, tq, W), lambda s, i: (s, i, 0)),
                   pl.BlockSpec((1, tq, LANES), lambda s, i: (s, i, 0))],
        out_shape=[jax.ShapeDtypeStruct((S, M, W), BF16), jax.ShapeDtypeStruct((S, M, LANES), F32)],
        scratch_shapes=[pltpu.VMEM((tq + 2 * hw, 2 * W), BF16)],
        compiler_params=_params("parallel", "parallel"),
    )(q, kv, kv, kv, bias)


def _combine_kernel(o1_ref, o4_ref, o16_ref, l1_ref, l4_ref, l16_ref, ya_ref, os_ref, ls_ref):
    T, nc = PERM_TILE, BRANCH_WIDTH // LANES
    for g, (d, o_ref, l_ref) in enumerate(((4, o4_ref, l4_ref), (16, o16_ref, l16_ref))):
        for r in range(d):
            ls_ref[g, pl.ds(r, T // d, stride=d), :] = l_ref[0, r]
            for c in range(nc):
                os_ref[g, c, pl.ds(r, T // d, stride=d), :] = o_ref[0, r, :, c * LANES:(c + 1) * LANES].astype(F32)
    l1, l4, l16 = l1_ref[0, 0], ls_ref[0], ls_ref[1]
    m = jnp.maximum(jnp.maximum(l1, l4), l16)
    e1, e4, e16 = jnp.exp(l1 - m), jnp.exp(l4 - m), jnp.exp(l16 - m)
    inv = 1.0 / (e1 + e4 + e16)
    w1, w4, w16 = e1 * inv, e4 * inv, e16 * inv
    for h in range(N_HEADS):
        c0, c1 = h * HEAD_DIM, (h + 1) * HEAD_DIM
        lh = h * LSE_LANES_PER_HEAD
        ya = (w1[:, lh:lh + 1] * o1_ref[0, 0, :, c0:c1].astype(F32)
              + w4[:, lh:lh + 1] * os_ref[0, h] + w16[:, lh:lh + 1] * os_ref[1, h])
        ya_ref[0, :, c0:c1] = ya.astype(BF16)


def _combine(o1, o4, o16, l1, l4, l16):
    B, _, L, W = o1.shape
    T = PERM_TILE

    def spec(d, c):
        return pl.BlockSpec((1, d, T // d, c), lambda b, i: (b, 0, i, 0))

    return pl.pallas_call(
        _combine_kernel,
        name="combine",
        grid=(B, L // T),
        in_specs=[spec(1, W), spec(4, W), spec(16, W), spec(1, LANES), spec(4, LANES), spec(16, LANES)],
        out_specs=pl.BlockSpec((1, T, W), lambda b, i: (b, i, 0)),
        out_shape=jax.ShapeDtypeStruct((B, L, W), BF16),
        scratch_shapes=[pltpu.VMEM((2, W // LANES, T, LANES), F32), pltpu.VMEM((2, T, LANES), F32)],
        compiler_params=_params("parallel", "parallel"),
    )(o1, o4, o16, l1, l4, l16)


def _memkv_kernel(mem_ref, g_ref, w_ref, kv_ref):
    x = mem_ref[0]
    ms = jnp.mean(x * x, axis=-1, keepdims=True)
    xn = (x * lax.rsqrt(ms + EPS) * g_ref[...]).astype(BF16)
    kv_ref[0] = _dot(xn, w_ref[...]).astype(BF16)


def _memkv(mem, gain, w):
    B, N, D = mem.shape
    W2 = w.shape[1]
    return pl.pallas_call(
        _memkv_kernel,
        name="memkv",
        grid=(B,),
        in_specs=[pl.BlockSpec((1, N, D), lambda b: (b, 0, 0)), pl.BlockSpec((1, D), lambda b: (0, 0)),
                  _resident((D, W2))],
        out_specs=pl.BlockSpec((1, N, W2), lambda b: (b, 0, 0)),
        out_shape=jax.ShapeDtypeStruct((B, N, W2), BF16),
        compiler_params=_params("parallel"),
    )(mem, gain.reshape(1, D), w)


def _proj_bd_kernel(h_ref, w_ref, kv_ref, xb_ref, yd_ref):
    W = BRANCH_WIDTH
    z = _dot(h_ref[0], w_ref[...])
    xb_ref[0] = z[:, :W]
    qd = z[:, W:].astype(BF16)
    scale = HEAD_DIM ** -0.5
    for h in range(N_HEADS):
        c0, c1 = h * HEAD_DIM, (h + 1) * HEAD_DIM
        s = _dot_nt(qd[:, c0:c1], kv_ref[0, :, c0:c1]) * scale
        m = jnp.max(s, axis=-1, keepdims=True)
        p = jnp.exp(s - m)
        l = jnp.sum(p, axis=-1, keepdims=True)
        o = _dot(p.astype(BF16), kv_ref[0, :, W + c0:W + c1]) / l
        yd_ref[0, :, c0:c1] = o.astype(BF16)


def _proj_bd(h, w_bd, kv):
    B, L, D = h.shape
    T, W = ROW_TILE, BRANCH_WIDTH
    return pl.pallas_call(
        _proj_bd_kernel,
        name="proj_bd",
        grid=(B, L // T),
        in_specs=[pl.BlockSpec((1, T, D), lambda b, i: (b, i, 0)), _resident((D, 2 * W)),
                  pl.BlockSpec((1, N_MEM, 2 * W), lambda b, i: (b, 0, 0))],
        out_specs=[pl.BlockSpec((1, T, W), lambda b, i: (b, i, 0)), pl.BlockSpec((1, T, W), lambda b, i: (b, i, 0))],
        out_shape=[jax.ShapeDtypeStruct((B, L, W), F32), jax.ShapeDtypeStruct((B, L, W), BF16)],
        compiler_params=_params("parallel", "parallel"),
    )(h, w_bd, kv)


def _fft_factors(L):
    n1 = 1 << ((L.bit_length() - 1 + 1) // 2)
    return n1, L // n1


def _dft_cos_sin(n):
    ang = 2.0 * np.pi * ((np.arange(n)[:, None] * np.arange(n)[None, :]) % n) / n
    return np.cos(ang), np.sin(ang)


def _fft1_kernel(x_ref, f_ref, tc_ref, ts_ref, yr_ref, yi_ref, xs_ref, *, n1):
    W = BRANCH_WIDTH
    for j in range(FFT_ROWS):
        xs_ref[:, j * W:(j + 1) * W] = x_ref[0, :, j, :]
    y = _dot(f_ref[...], xs_ref[...].astype(BF16))
    reps = W // FNET_GROUP_WIDTH
    for j in range(FFT_ROWS):
        yr, yi = y[:n1, j * W:(j + 1) * W], y[n1:, j * W:(j + 1) * W]
        tc = jnp.concatenate([tc_ref[j]] * reps, axis=1)
        ts = jnp.concatenate([ts_ref[j]] * reps, axis=1)
        yr_ref[0, j] = yr * tc + yi * ts
        yi_ref[0, j] = yi * tc - yr * ts


def _fft2_kernel(yr_ref, yi_ref, g1_ref, g2_ref, cs_ref, out_ref, ys_ref, *, n2, norm):
    W, gw = BRANCH_WIDTH, FNET_GROUP_WIDTH
    for j in range(FFT_ROWS):
        ys_ref[0, :, j * W:(j + 1) * W] = yr_ref[0, :, j, :]
        ys_ref[1, :, j * W:(j + 1) * W] = yi_ref[0, :, j, :]
    u = _dot(g1_ref[...], ys_ref[0].astype(BF16)) + _dot(g2_ref[...], ys_ref[1].astype(BF16))
    top, bot = u[:n2].astype(BF16), u[n2:].astype(BF16)
    ng = FFT_ROWS * W // gw
    lhs = jnp.concatenate(
        [jnp.concatenate([top[:, j * gw:(j + 1) * gw], bot[:, j * gw:(j + 1) * gw]], axis=1) for j in range(ng)],
        axis=0)
    res = _dot(lhs, cs_ref[...]) * norm
    for j in range(ng):
        k1, c = divmod(j, W // gw)
        out_ref[0, :, k1, c * gw:(c + 1) * gw] = res[j * n2:(j + 1) * n2]


def _fnet(xb, tables):
    B, L, W = xb.shape
    n1, n2 = _fft_factors(L)
    f1, twc, tws, g1, g2, cs = tables
    R = FFT_ROWS
    yr, yi = pl.pallas_call(
        functools.partial(_fft1_kernel, n1=n1),
        name="fft1",
        grid=(B, n2 // R),
        in_specs=[pl.BlockSpec((1, n1, R, W), lambda b, j: (b, 0, j, 0)), _resident((2 * n1, n1)),
                  pl.BlockSpec((R, n1, FNET_GROUP_WIDTH), lambda b, j: (j, 0, 0)),
                  pl.BlockSpec((R, n1, FNET_GROUP_WIDTH), lambda b, j: (j, 0, 0))],
        out_specs=[pl.BlockSpec((1, R, n1, W), lambda b, j: (b, j, 0, 0))] * 2,
        out_shape=[jax.ShapeDtypeStruct((B, n2, n1, W), F32)] * 2,
        scratch_shapes=[pltpu.VMEM((n1, R * W), F32)],
        compiler_params=_params("parallel", "parallel"),
    )(xb.reshape(B, n1, n2, W), f1, twc, tws)
    out = pl.pallas_call(
        functools.partial(_fft2_kernel, n2=n2, norm=float(1.0 / math.sqrt(L * FNET_GROUP_WIDTH))),
        name="fft2",
        grid=(B, n1 // R),
        in_specs=[pl.BlockSpec((1, n2, R, W), lambda b, j: (b, 0, j, 0))] * 2
        + [_resident((2 * n2, n2)), _resident((2 * n2, n2)), _resident((2 * FNET_GROUP_WIDTH, FNET_GROUP_WIDTH))],
        out_specs=pl.BlockSpec((1, n2, R, W), lambda b, j: (b, 0, j, 0)),
        out_shape=jax.ShapeDtypeStruct((B, n2, n1, W), F32),
        scratch_shapes=[pltpu.VMEM((2, n2, R * W), F32)],
        compiler_params=_params("parallel", "parallel"),
    )(yr, yi, g1, g2, cs)
    return out.reshape(B, L, W)


def _fnet_tables(L):
    n1, n2 = _fft_factors(L)
    c1, s1 = _dft_cos_sin(n1)
    c2, s2 = _dft_cos_sin(n2)
    cc, sc = _dft_cos_sin(FNET_GROUP_WIDTH)
    ang = 2.0 * np.pi * (np.arange(n2)[:, None] * np.arange(n1)[None, :]) / L
    bcast = lambda t: jnp.asarray(np.broadcast_to(t[:, :, None], (n2, n1, FNET_GROUP_WIDTH)), F32)
    return (jnp.asarray(np.concatenate([c1, -s1], axis=0), BF16), bcast(np.cos(ang)), bcast(np.sin(ang)),
            jnp.asarray(np.concatenate([c2, -s2], axis=0), BF16), jnp.asarray(np.concatenate([s2, c2], axis=0), BF16),
            jnp.asarray(np.concatenate([cc, sc], axis=0), BF16))


def _proj_c_kernel(h_ref, hp_ref, hn_ref, wqk_ref, wvo_ref, wg_ref, conv_ref, gb_ref,
                   q_ref, k_ref, v_ref, o_ref, g_ref, ext_ref):
    i = pl.program_id(1)
    T, W, halo = ROW_TILE, BRANCH_WIDTH, CONV_HALO
    h = h_ref[0]
    has_prev = (i > 0).astype(F32)
    has_next = (i < pl.num_programs(1) - 1).astype(F32)
    ext_ref[0:halo] = _dot(hp_ref[0], wqk_ref[...]) * has_prev
    ext_ref[halo:halo + T] = _dot(h, wqk_ref[...])
    ext_ref[halo + T:] = _dot(hn_ref[0], wqk_ref[...]) * has_next
    pad = MLSTM_CONV // 2
    acc = ext_ref[pl.ds(halo - pad, T), :] * conv_ref[0:1, :]
    for j in range(1, MLSTM_CONV):
        acc = acc + ext_ref[pl.ds(halo - pad + j, T), :] * conv_ref[j:j + 1, :]
    y = acc * _sigmoid(acc)
    q_ref[0] = (y[:, :W] * (HEAD_DIM ** -0.5)).astype(BF16)
    k_ref[0] = y[:, W:].astype(BF16)
    vo = _dot(h, wvo_ref[...])
    v_ref[0] = vo[:, :W].astype(BF16)
    o_ref[0] = vo[:, W:].astype(BF16)
    g_ref[0] = _dot(h, wg_ref[...]) + gb_ref[...]


def _proj_c(h, w_qk, w_vo, w_g, conv, gate_bias):
    B, L, D = h.shape
    T, W, halo = ROW_TILE, BRANCH_WIDTH, CONV_HALO
    step = T // halo
    nhb = L // halo
    tile = lambda c: pl.BlockSpec((1, T, c), lambda b, i: (b, i, 0))
    return pl.pallas_call(
        _proj_c_kernel,
        name="proj_c",
        grid=(B, L // T),
        in_specs=[tile(D),
                  pl.BlockSpec((1, halo, D), lambda b, i: (b, jnp.maximum(i * step - 1, 0), 0)),
                  pl.BlockSpec((1, halo, D), lambda b, i: (b, jnp.minimum((i + 1) * step, nhb - 1), 0)),
                  _resident((D, 2 * W)), _resident((D, 2 * W)), _resident((D, MLSTM_GATES)),
                  _resident((MLSTM_CONV, 2 * W)), _resident((1, MLSTM_GATES))],
        out_specs=[tile(W), tile(W), tile(W), tile(W), tile(MLSTM_GATES)],
        out_shape=[jax.ShapeDtypeStruct((B, L, W), BF16)] * 4 + [jax.ShapeDtypeStruct((B, L, MLSTM_GATES), F32)],
        scratch_shapes=[pltpu.VMEM((T + 2 * halo, 2 * W), F32)],
        compiler_params=_params("parallel", "arbitrary"),
    )(h, h, h, w_qk, w_vo, w_g, conv, gate_bias.reshape(1, MLSTM_GATES))


def _mlstm_kernel(*refs, reverse, finalize):
    if finalize:
        (q_ref, k_ref, v_ref, g_ref, gt_ref, hf_ref, oc_ref, gain_ref, out_ref, c_ref, m_ref) = refs
    else:
        (q_ref, k_ref, v_ref, g_ref, gt_ref, out_ref, c_ref, m_ref) = refs
    T, dh, H = MLSTM_CHUNK, HEAD_DIM, N_HEADS

    @pl.when(pl.program_id(1) == 0)
    def _():
        c_ref[...] = jnp.zeros_like(c_ref)
        m_ref[...] = jnp.zeros_like(m_ref)

    gi0 = 8 if reverse else 0
    gcol = g_ref[0]
    grow = gt_ref[0]
    row = lax.broadcasted_iota(jnp.int32, (T, T), 0)
    col = lax.broadcasted_iota(jnp.int32, (T, T), 1)
    feeds = (col >= row) if reverse else (col <= row)
    tri = feeds.astype(F32)
    tri_t = ((row >= col) if reverse else (row <= col)).astype(F32)
    lf_cols = _log_sigmoid(gcol[:, gi0 + H:gi0 + 2 * H])
    lf_rows = _log_sigmoid(grow[gi0 + H:gi0 + 2 * H, :])
    b_cols = jnp.dot(tri, lf_cols, preferred_element_type=F32, precision=lax.Precision.HIGHEST)
    b_rows = jnp.dot(lf_rows, tri_t, preferred_element_type=F32, precision=lax.Precision.HIGHEST)
    lane = lax.broadcasted_iota(jnp.int32, (T, dh), 1)

    for h in range(H):
        c0, c1 = h * dh, (h + 1) * dh
        q, k, v = q_ref[0, :, c0:c1], k_ref[0, :, c0:c1], v_ref[0, :, c0:c1]
        bc, br = b_cols[:, h:h + 1], b_rows[h:h + 1, :]
        ic, ir = gcol[:, gi0 + h:gi0 + h + 1], grow[gi0 + h:gi0 + h + 1, :]
        g_tot = jnp.sum(lf_cols[:, h:h + 1], axis=0, keepdims=True)
        m_prev = m_ref[h][0:1, 0:1]
        state = c_ref[h]

        logd = jnp.where(feeds, bc - br + ir, NEG)
        inter = bc + m_prev
        m_t = jnp.maximum(inter, jnp.max(logd, axis=-1, keepdims=True))
        w_intra = jnp.exp(logd - m_t)
        w_inter = jnp.exp(inter - m_t)
        sqk = w_intra * _dot_nt(q, k)
        qs = _dot(q, state.astype(BF16))
        num = w_inter * qs[:, :dh] + _dot(sqk.astype(BF16), v)
        den = w_inter * qs[:, dh:dh + 1] + jnp.sum(sqk, axis=-1, keepdims=True)
        hout = num / jnp.maximum(jnp.abs(den), jnp.exp(-m_t))

        a = g_tot - bc + ic
        ma = jnp.max(a, axis=0, keepdims=True)
        m_new = jnp.maximum(g_tot + m_prev, ma)
        wa = jnp.exp(a - ma)
        rhs = jnp.concatenate([(v.astype(F32) * wa).astype(BF16),
                               jnp.where(lane == 0, wa, 0.0).astype(BF16)], axis=1)
        c_ref[h] = jnp.exp(g_tot + m_prev - m_new) * state + jnp.exp(ma - m_new) * _dot_tn(k, rhs)
        m_ref[h] = jnp.broadcast_to(m_new, m_ref.shape[1:])

        if finalize:
            hc = _sigmoid(oc_ref[0, :, c0:c1].astype(F32)) * (hout + hf_ref[0, :, c0:c1])
            mu = jnp.mean(hc, axis=-1, keepdims=True)
            cen = hc - mu
            var = jnp.mean(cen * cen, axis=-1, keepdims=True)
            out_ref[0, :, c0:c1] = (cen * lax.rsqrt(var + EPS) * gain_ref[:, c0:c1]).astype(BF16)
        else:
            out_ref[0, :, c0:c1] = hout


def _mlstm_pass(q, k, v, g, gt, extra, *, reverse):
    B, L, W = q.shape
    T = MLSTM_CHUNK
    n = L // T
    pos = (lambda i: n - 1 - i) if reverse else (lambda i: i)
    tile = lambda c: pl.BlockSpec((1, T, c), lambda b, i: (b, pos(i), 0))
    in_specs = [tile(W), tile(W), tile(W), tile(MLSTM_GATES),
                pl.BlockSpec((1, MLSTM_GATES, T), lambda b, i: (b, 0, pos(i)))]
    finalize = extra is not None
    args = [q, k, v, g, gt]
    if finalize:
        in_specs += [tile(W), tile(W), _resident((1, W))]
        args += list(extra)
    return pl.pallas_call(
        functools.partial(_mlstm_kernel, reverse=reverse, finalize=finalize),
        name="mlstm_bwd" if reverse else "mlstm_fwd",
        grid=(B, n),
        in_specs=in_specs,
        out_specs=tile(W),
        out_shape=jax.ShapeDtypeStruct((B, L, W), BF16 if finalize else F32),
        scratch_shapes=[pltpu.VMEM((N_HEADS, HEAD_DIM, 2 * HEAD_DIM), F32), pltpu.VMEM((N_HEADS, 8, 128), F32)],
        compiler_params=_params("parallel", "arbitrary"),
    )(*args)


def _merge_kernel(x_ref, h_ref, ya_ref, yb_ref, yc_ref, yd_ref, wgp_ref, wmg_ref, wbr_ref, wout_ref, g_ref, out_ref):
    W, D = BRANCH_WIDTH, D_MODEL
    h = h_ref[...]
    acc = None
    for g, br_ref in enumerate((ya_ref, yb_ref, yc_ref, yd_ref)):
        gp = _dot(h, wgp_ref[:, g * W:(g + 1) * W])
        gated = (br_ref[...].astype(F32) * (gp * _sigmoid(gp))).astype(BF16)
        proj = _dot(gated, wbr_ref[g])
        term = _sigmoid(_dot(h, wmg_ref[:, g * D:(g + 1) * D])) * proj
        acc = term if acc is None else acc + term
    out = _dot(acc.astype(BF16), wout_ref[...])
    ms = jnp.mean(out * out, axis=-1, keepdims=True)
    out_ref[...] = x_ref[...] + out * lax.rsqrt(ms + EPS) * g_ref[...]


def _merge(x_rows, h_rows, branches, w_gp, w_mg, w_br, w_out, gain):
    R, D = x_rows.shape
    T, W = ROW_TILE, BRANCH_WIDTH
    rows = lambda c: pl.BlockSpec((T, c), lambda i: (i, 0))
    return pl.pallas_call(
        _merge_kernel,
        name="merge",
        grid=(R // T,),
        in_specs=[rows(D), rows(D)] + [rows(W)] * N_BRANCHES
        + [_resident(w_gp.shape), _resident(w_mg.shape), _resident(w_br.shape), _resident(w_out.shape),
           _resident((1, D))],
        out_specs=rows(D),
        out_shape=jax.ShapeDtypeStruct((R, D), F32),
        compiler_params=_params("parallel"),
    )(x_rows, h_rows, *branches, w_gp, w_mg, w_br, w_out, gain.reshape(1, D))


def _t5_bucket(rel):
    nb = REL_BUCKETS // 2
    max_exact = nb // 2
    ret = (rel > 0).astype(np.int32) * nb
    n = np.abs(rel)
    large = max_exact + (np.log(np.maximum(n, 1) / max_exact) / np.log(REL_MAX_DIST / max_exact)
                         * (nb - max_exact)).astype(np.int32)
    large = np.minimum(large, nb - 1)
    return (ret + np.where(n < max_exact, n, large)).astype(np.int32)


def _band_bias(rel_bias, group, dilation):
    qb, hw = ATTN_Q_BLOCK, HALF_WINDOW
    rel = np.arange(qb + 2 * hw)[None, :] - hw - np.arange(qb)[:, None]
    heads = rel_bias[:, group * N_HEADS:(group + 1) * N_HEADS].astype(F32)
    onehot = (jnp.asarray(_t5_bucket(dilation * rel))[None, :, :] == jnp.arange(REL_BUCKETS)[:, None, None])
    bias = jnp.sum(jnp.where(onehot[:, None], heads[:, :, None, None], 0.0), axis=0)
    return jnp.where(jnp.asarray(np.abs(rel) <= hw)[None], bias, NEG)


def _split_weights(w_in, conv_qk, w_mem_kv, w_branch, w_out):
    W, D, A = BRANCH_WIDTH, D_MODEL, A_QKV_WIDTH
    off = np.cumsum([0, A, A, A, W, W, W, W, W, MLSTM_GATES, W, N_BRANCHES * W, N_BRANCHES * D])
    seg = lambda j: w_in[:, off[j]:off[j + 1]]
    qa, ka, va = seg(0), seg(1), seg(2)
    w_a = [jnp.concatenate([t[:, g * W:(g + 1) * W] for t in (qa, ka, va)], axis=1).astype(BF16)
           for g in range(len(DILATIONS))]
    return dict(
        a=w_a,
        bd=jnp.concatenate([seg(3), seg(9)], axis=1).astype(BF16),
        qk=jnp.concatenate([seg(4), seg(5)], axis=1).astype(BF16),
        vo=jnp.concatenate([seg(6), seg(7)], axis=1).astype(BF16),
        g=seg(8).astype(BF16),
        gp=seg(10).astype(BF16),
        mg=seg(11).astype(BF16),
        conv=conv_qk.astype(F32),
        memkv=w_mem_kv.astype(BF16),
        br=w_branch.astype(BF16),
        out=w_out.astype(BF16),
    )


def _layer(x, mem, biases, fft_tables, wts, norm_pre, gate_bias, head_gain, mem_norm, norm_post):
    B, L, D = x.shape
    W = BRANCH_WIDTH
    h, h4, h16 = _prenorm(x, norm_pre)

    outs, lses = [], []
    for g, (d, hp) in enumerate(zip(DILATIONS, (h, h4, h16))):
        q, kv = _proj_a(hp.reshape(B * L, D), wts["a"][g])
        o, lse = _attention(q.reshape(B * d, L // d, W), kv.reshape(B * d, L // d, 2 * W), biases[g])
        outs.append(o.reshape(B, d, L // d, W))
        lses.append(lse.reshape(B, d, L // d, LANES))
    ya = _combine(*outs, *lses)

    kv_mem = _memkv(mem, mem_norm, wts["memkv"])
    xb, yd = _proj_bd(h, wts["bd"], kv_mem)
    yb = _fnet(xb, fft_tables)

    qm, km, vm, oc, gates = _proj_c(h, wts["qk"], wts["vo"], wts["g"], wts["conv"], gate_bias.reshape(-1))
    gates_t = jnp.swapaxes(gates, 1, 2)
    h_fwd = _mlstm_pass(qm, km, vm, gates, gates_t, None, reverse=False)
    yc = _mlstm_pass(qm, km, vm, gates, gates_t, (h_fwd, oc, head_gain.reshape(1, W).astype(F32)), reverse=True)

    rows = lambda t: t.reshape(B * L, t.shape[-1])
    y = _merge(rows(x), rows(h), [rows(ya), rows(yb), rows(yc), rows(yd)],
               wts["gp"], wts["mg"], wts["br"], wts["out"], norm_post)
    return y.reshape(B, L, D)


def kernel(x_prompt, x_sample, mem_prompt, mem_sample, rel_bias, norm_pre, w_in, conv_qk, mlstm_gate_bias,
           mlstm_head_gain, mem_norm, w_mem_kv, w_branch, w_out, norm_post):
    depth = w_in.shape[0]
    biases = [_band_bias(rel_bias, g, d) for g, d in enumerate(DILATIONS)]
    layer_wts = [_split_weights(w_in[l], conv_qk[l], w_mem_kv[l], w_branch[l], w_out[l]) for l in range(depth)]
    outs = []
    for x, mem in ((x_prompt, mem_prompt), (x_sample, mem_sample)):
        tables = _fnet_tables(x.shape[1])
        for l in range(depth):
            x = _layer(x, mem, biases, tables, layer_wts[l], norm_pre[l], mlstm_gate_bias[l],
                       mlstm_head_gain[l], mem_norm[l], norm_post[l])
        outs.append(x)
    return tuple(outs)
```

```python
import functools
import math

import numpy as np
import jax
import jax.numpy as jnp
from jax import lax
from jax.experimental import pallas as pl
from jax.experimental.pallas import tpu as pltpu

D_MODEL = 1024
BRANCH_WIDTH = 512
HEAD_DIM = 128
N_HEADS = 4
N_BRANCHES = 4
DILATIONS = (1, 4, 16)
HALF_WINDOW = 64
A_QKV_WIDTH = len(DILATIONS) * BRANCH_WIDTH
REL_BUCKETS = 32
REL_MAX_DIST = 1024
FNET_GROUP_WIDTH = 128
MLSTM_CONV = 5
MLSTM_GATES = 16
N_MEM = 256
EPS = 1e-6
NEG = -1e30

F32 = jnp.float32
BF16 = jnp.bfloat16
LANES = 128
LSE_LANES_PER_HEAD = LANES // N_HEADS

VMEM_LIMIT_BYTES = 56 * 1024 * 1024
PERM_TILE = 1024
ROW_TILE = 512
ATTN_Q_BLOCK = 128
MLSTM_CHUNK = 256
FFT_ROWS = 8
CONV_HALO = 16


def _params(*sem):
    return pltpu.CompilerParams(dimension_semantics=sem, vmem_limit_bytes=VMEM_LIMIT_BYTES)


def _resident(shape):
    nd = len(shape)
    return pl.BlockSpec(shape, lambda *_: (0,) * nd, pipeline_mode=pl.Buffered(1))


def _sigmoid(x):
    return 1.0 / (1.0 + jnp.exp(-x))


def _log_sigmoid(x):
    return jnp.minimum(x, 0.0) - jnp.log(1.0 + jnp.exp(-jnp.abs(x)))


def _dot(a, b):
    return jnp.dot(a, b, preferred_element_type=F32)


def _dot_nt(a, b):
    return lax.dot_general(a, b, (((1,), (1,)), ((), ())), preferred_element_type=F32)


def _dot_tn(a, b):
    return lax.dot_general(a, b, (((0,), (0,)), ((), ())), preferred_element_type=F32)


def _prenorm_kernel(x_ref, g_ref, h_ref, h4_ref, h16_ref, xs_ref):
    T = PERM_TILE
    x = x_ref[0]
    ms = jnp.mean(x * x, axis=-1, keepdims=True)
    xn = x * lax.rsqrt(ms + EPS) * g_ref[...]
    h_ref[0] = xn.astype(BF16)
    for c in range(D_MODEL // LANES):
        xs_ref[c] = xn[:, c * LANES:(c + 1) * LANES]
    for d, out_ref in ((4, h4_ref), (16, h16_ref)):
        for r in range(d):
            for c in range(D_MODEL // LANES):
                out_ref[0, r, :, c * LANES:(c + 1) * LANES] = xs_ref[c, pl.ds(r, T // d, stride=d), :].astype(BF16)


def _prenorm(x, gain):
    B, L, D = x.shape
    T = PERM_TILE
    return pl.pallas_call(
        _prenorm_kernel,
        name="prenorm",
        grid=(B, L // T),
        in_specs=[pl.BlockSpec((1, T, D), lambda b, i: (b, i, 0)),
                  pl.BlockSpec((1, D), lambda b, i: (0, 0))],
        out_specs=[pl.BlockSpec((1, T, D), lambda b, i: (b, i, 0)),
                   pl.BlockSpec((1, 4, T // 4, D), lambda b, i: (b, 0, i, 0)),
                   pl.BlockSpec((1, 16, T // 16, D), lambda b, i: (b, 0, i, 0))],
        out_shape=[jax.ShapeDtypeStruct((B, L, D), BF16),
                   jax.ShapeDtypeStruct((B, 4, L // 4, D), BF16),
                   jax.ShapeDtypeStruct((B, 16, L // 16, D), BF16)],
        scratch_shapes=[pltpu.VMEM((D // LANES, T, LANES), F32)],
        compiler_params=_params("parallel", "parallel"),
    )(x, gain.reshape(1, D))


def _proj_a_kernel(h_ref, w_ref, q_ref, kv_ref):
    z = _dot(h_ref[...], w_ref[...])
    q_ref[...] = z[:, :BRANCH_WIDTH].astype(BF16)
    kv_ref[...] = z[:, BRANCH_WIDTH:].astype(BF16)


def _proj_a(h_rows, w_qkv):
    R, D = h_rows.shape
    T = ROW_TILE
    W = BRANCH_WIDTH
    return pl.pallas_call(
        _proj_a_kernel,
        name="proj_a",
        grid=(R // T,),
        in_specs=[pl.BlockSpec((T, D), lambda i: (i, 0)), _resident((D, 3 * W))],
        out_specs=[pl.BlockSpec((T, W), lambda i: (i, 0)), pl.BlockSpec((T, 2 * W), lambda i: (i, 0))],
        out_shape=[jax.ShapeDtypeStruct((R, W), BF16), jax.ShapeDtypeStruct((R, 2 * W), BF16)],
        compiler_params=_params("parallel"),
    )(h_rows, w_qkv)


def _attn_kernel(q_ref, kv_ref, kvp_ref, kvn_ref, bias_ref, o_ref, lse_ref, win_ref, *, seq_len, tq):
    i = pl.program_id(1)
    hw, qb, W = HALF_WINDOW, ATTN_Q_BLOCK, BRANCH_WIDTH
    kb = qb + 2 * hw
    win_ref[0:hw] = kvp_ref[0]
    win_ref[hw:hw + tq] = kv_ref[0]
    win_ref[hw + tq:] = kvn_ref[0]
    scale = HEAD_DIM ** -0.5
    lane_head = lax.broadcasted_iota(jnp.int32, (qb, LANES), 1) // LSE_LANES_PER_HEAD
    for j in range(tq // qb):
        a = j * qb
        kidx = i * tq + (a - hw) + lax.broadcasted_iota(jnp.int32, (1, kb), 1)
        valid = (kidx >= 0) & (kidx < seq_len)
        lse_all = None
        for h in range(N_HEADS):
            c0, c1 = h * HEAD_DIM, (h + 1) * HEAD_DIM
            q = q_ref[0, a:a + qb, c0:c1]
            k = win_ref[a:a + kb, c0:c1]
            v = win_ref[a:a + kb, W + c0:W + c1]
            s = _dot_nt(q, k) * scale + bias_ref[h]
            s = jnp.where(valid, s, NEG)
            m = jnp.max(s, axis=-1, keepdims=True)
            p = jnp.exp(s - m)
            l = jnp.sum(p, axis=-1, keepdims=True)
            o = _dot(p.astype(BF16), v) / l
            o_ref[0, a:a + qb, c0:c1] = o.astype(BF16)
            lse = jnp.broadcast_to(m + jnp.log(l), (qb, LANES))
            lse_all = lse if h == 0 else jnp.where(lane_head == h, lse, lse_all)
        lse_ref[0, a:a + qb, :] = lse_all


def _attention(q, kv, bias):
    S, M, W = q.shape
    hw = HALF_WINDOW
    tq = min(M, 512)
    nhb = M // hw
    step = tq // hw
    kern = functools.partial(_attn_kernel, seq_len=M, tq=tq)
    return pl.pallas_call(
        kern,
        name="attn",
        grid=(S, M // tq),
        in_specs=[pl.BlockSpec((1, tq, W), lambda s, i: (s, i, 0)),
                  pl.BlockSpec((1, tq, 2 * W), lambda s, i: (s, i, 0)),
                  pl.BlockSpec((1, hw, 2 * W), lambda s, i: (s, jnp.maximum(i * step - 1, 0), 0)),
                  pl.BlockSpec((1, hw, 2 * W), lambda s, i: (s, jnp.minimum((i + 1) * step, nhb - 1), 0)),
                  _resident(bias.shape)],
        out_specs=[pl.BlockSpec((1, tq, W), lambda s, i: (s, i, 0)),
                   pl.BlockSpec((1, tq, LANES), lambda s, i: (s, i, 0))],
        out_shape=[jax.ShapeDtypeStruct((S, M, W), BF16), jax.ShapeDtypeStruct((S, M, LANES), F32)],
        scratch_shapes=[pltpu.VMEM((tq + 2 * hw, 2 * W), BF16)],
        compiler_params=_params("parallel", "parallel"),
    )(q, kv, kv, kv, bias)


def _combine_kernel(o1_ref, o4_ref, o16_ref, l1_ref, l4_ref, l16_ref, ya_ref, os_ref, ls_ref):
    T, nc = PERM_TILE, BRANCH_WIDTH // LANES
    for g, (d, o_ref, l_ref) in enumerate(((4, o4_ref, l4_ref), (16, o16_ref, l16_ref))):
        for r in range(d):
            ls_ref[g, pl.ds(r, T // d, stride=d), :] = l_ref[0, r]
            for c in range(nc):
                os_ref[g, c, pl.ds(r, T // d, stride=d), :] = o_ref[0, r, :, c * LANES:(c + 1) * LANES].astype(F32)
    l1, l4, l16 = l1_ref[0, 0], ls_ref[0], ls_ref[1]
    m = jnp.maximum(jnp.maximum(l1, l4), l16)
    e1, e4, e16 = jnp.exp(l1 - m), jnp.exp(l4 - m), jnp.exp(l16 - m)
    inv = 1.0 / (e1 + e4 + e16)
    w1, w4, w16 = e1 * inv, e4 * inv, e16 * inv
    for h in range(N_HEADS):
        c0, c1 = h * HEAD_DIM, (h + 1) * HEAD_DIM
        lh = h * LSE_LANES_PER_HEAD
        ya = (w1[:, lh:lh + 1] * o1_ref[0, 0, :, c0:c1].astype(F32)
              + w4[:, lh:lh + 1] * os_ref[0, h] + w16[:, lh:lh + 1] * os_ref[1, h])
        ya_ref[0, :, c0:c1] = ya.astype(BF16)


def _combine(o1, o4, o16, l1, l4, l16):
    B, _, L, W = o1.shape
    T = PERM_TILE

    def spec(d, c):
        return pl.BlockSpec((1, d, T // d, c), lambda b, i: (b, 0, i, 0))

    return pl.pallas_call(
        _combine_kernel,
        name="combine",
        grid=(B, L // T),
        in_specs=[spec(1, W), spec(4, W), spec(16, W), spec(1, LANES), spec(4, LANES), spec(16, LANES)],
        out_specs=pl.BlockSpec((1, T, W), lambda b, i: (b, i, 0)),
        out_shape=jax.ShapeDtypeStruct((B, L, W), BF16),
        scratch_shapes=[pltpu.VMEM((2, W // LANES, T, LANES), F32), pltpu.VMEM((2, T, LANES), F32)],
        compiler_params=_params("parallel", "parallel"),
    )(o1, o4, o16, l1, l4, l16)


def _memkv_kernel(mem_ref, g_ref, w_ref, kv_ref):
    x = mem_ref[0]
    ms = jnp.mean(x * x, axis=-1, keepdims=True)
    xn = (x * lax.rsqrt(ms + EPS) * g_ref[...]).astype(BF16)
    kv_ref[0] = _dot(xn, w_ref[...]).astype(BF16)


def _memkv(mem, gain, w):
    B, N, D = mem.shape
    W2 = w.shape[1]
    return pl.pallas_call(
        _memkv_kernel,
        name="memkv",
        grid=(B,),
        in_specs=[pl.BlockSpec((1, N, D), lambda b: (b, 0, 0)), pl.BlockSpec((1, D), lambda b: (0, 0)),
                  _resident((D, W2))],
        out_specs=pl.BlockSpec((1, N, W2), lambda b: (b, 0, 0)),
        out_shape=jax.ShapeDtypeStruct((B, N, W2), BF16),
        compiler_params=_params("parallel"),
    )(mem, gain.reshape(1, D), w)


def _proj_bd_kernel(h_ref, w_ref, kv_ref, xb_ref, yd_ref):
    W = BRANCH_WIDTH
    z = _dot(h_ref[0], w_ref[...])
    xb_ref[0] = z[:, :W]
    qd = z[:, W:].astype(BF16)
    scale = HEAD_DIM ** -0.5
    for h in range(N_HEADS):
        c0, c1 = h * HEAD_DIM, (h + 1) * HEAD_DIM
        s = _dot_nt(qd[:, c0:c1], kv_ref[0, :, c0:c1]) * scale
        m = jnp.max(s, axis=-1, keepdims=True)
        p = jnp.exp(s - m)
        l = jnp.sum(p, axis=-1, keepdims=True)
        o = _dot(p.astype(BF16), kv_ref[0, :, W + c0:W + c1]) / l
        yd_ref[0, :, c0:c1] = o.astype(BF16)


def _proj_bd(h, w_bd, kv):
    B, L, D = h.shape
    T, W = ROW_TILE, BRANCH_WIDTH
    return pl.pallas_call(
        _proj_bd_kernel,
        name="proj_bd",
        grid=(B, L // T),
        in_specs=[pl.BlockSpec((1, T, D), lambda b, i: (b, i, 0)), _resident((D, 2 * W)),
                  pl.BlockSpec((1, N_MEM, 2 * W), lambda b, i: (b, 0, 0))],
        out_specs=[pl.BlockSpec((1, T, W), lambda b, i: (b, i, 0)), pl.BlockSpec((1, T, W), lambda b, i: (b, i, 0))],
        out_shape=[jax.ShapeDtypeStruct((B, L, W), F32), jax.ShapeDtypeStruct((B, L, W), BF16)],
        compiler_params=_params("parallel", "parallel"),
    )(h, w_bd, kv)


def _fft_factors(L):
    n1 = 1 << ((L.bit_length() - 1 + 1) // 2)
    return n1, L // n1


def _dft_cos_sin(n):
    ang = 2.0 * np.pi * ((np.arange(n)[:, None] * np.arange(n)[None, :]) % n) / n
    return np.cos(ang), np.sin(ang)


def _fft1_kernel(x_ref, f_ref, tc_ref, ts_ref, yr_ref, yi_ref, xs_ref, *, n1):
    W = BRANCH_WIDTH
    for j in range(FFT_ROWS):
        xs_ref[:, j * W:(j + 1) * W] = x_ref[0, :, j, :]
    y = _dot(f_ref[...], xs_ref[...].astype(BF16))
    reps = W // FNET_GROUP_WIDTH
    for j in range(FFT_ROWS):
        yr, yi = y[:n1, j * W:(j + 1) * W], y[n1:, j * W:(j + 1) * W]
        tc = jnp.concatenate([tc_ref[j]] * reps, axis=1)
        ts = jnp.concatenate([ts_ref[j]] * reps, axis=1)
        yr_ref[0, j] = yr * tc + yi * ts
        yi_ref[0, j] = yi * tc - yr * ts


def _fft2_kernel(yr_ref, yi_ref, g1_ref, g2_ref, cs_ref, out_ref, ys_ref, *, n2, norm):
    W, gw = BRANCH_WIDTH, FNET_GROUP_WIDTH
    for j in range(FFT_ROWS):
        ys_ref[0, :, j * W:(j + 1) * W] = yr_ref[0, :, j, :]
        ys_ref[1, :, j * W:(j + 1) * W] = yi_ref[0, :, j, :]
    u = _dot(g1_ref[...], ys_ref[0].astype(BF16)) + _dot(g2_ref[...], ys_ref[1].astype(BF16))
    top, bot = u[:n2].astype(BF16), u[n2:].astype(BF16)
    ng = FFT_ROWS * W // gw
    lhs = jnp.concatenate(
        [jnp.concatenate([top[:, j * gw:(j + 1) * gw], bot[:, j * gw:(j + 1) * gw]], axis=1) for j in range(ng)],
        axis=0)
    res = _dot(lhs, cs_ref[...]) * norm
    for j in range(ng):
        k1, c = divmod(j, W // gw)
        out_ref[0, :, k1, c * gw:(c + 1) * gw] = res[j * n2:(j + 1) * n2]


def _fnet(xb, tables):
    B, L, W = xb.shape
    n1, n2 = _fft_factors(L)
    f1, twc, tws, g1, g2, cs = tables
    R = FFT_ROWS
    yr, yi = pl.pallas_call(
        functools.partial(_fft1_kernel, n1=n1),
        name="fft1",
        grid=(B, n2 // R),
        in_specs=[pl.BlockSpec((1, n1, R, W), lambda b, j: (b, 0, j, 0)), _resident((2 * n1, n1)),
                  pl.BlockSpec((R, n1, FNET_GROUP_WIDTH), lambda b, j: (j, 0, 0)),
                  pl.BlockSpec((R, n1, FNET_GROUP_WIDTH), lambda b, j: (j, 0, 0))],
        out_specs=[pl.BlockSpec((1, R, n1, W), lambda b, j: (b, j, 0, 0))] * 2,
        out_shape=[jax.ShapeDtypeStruct((B, n2, n1, W), F32)] * 2,
        scratch_shapes=[pltpu.VMEM((n1, R * W), F32)],
        compiler_params=_params("parallel", "parallel"),
    )(xb.reshape(B, n1, n2, W), f1, twc, tws)
    out = pl.pallas_call(
        functools.partial(_fft2_kernel, n2=n2, norm=float(1.0 / math.sqrt(L * FNET_GROUP_WIDTH))),
        name="fft2",
        grid=(B, n1 // R),
        in_specs=[pl.BlockSpec((1, n2, R, W), lambda b, j: (b, 0, j, 0))] * 2
        + [_resident((2 * n2, n2)), _resident((2 * n2, n2)), _resident((2 * FNET_GROUP_WIDTH, FNET_GROUP_WIDTH))],
        out_specs=pl.BlockSpec((1, n2, R, W), lambda b, j: (b, 0, j, 0)),
        out_shape=jax.ShapeDtypeStruct((B, n2, n1, W), F32),
        scratch_shapes=[pltpu.VMEM((2, n2, R * W), F32)],
        compiler_params=_params("parallel", "parallel"),
    )(yr, yi, g1, g2, cs)
    return out.reshape(B, L, W)


def _fnet_tables(L):
    n1, n2 = _fft_factors(L)
    c1, s1 = _dft_cos_sin(n1)
    c2, s2 = _dft_cos_sin(n2)
    cc, sc = _dft_cos_sin(FNET_GROUP_WIDTH)
    ang = 2.0 * np.pi * (np.arange(n2)[:, None] * np.arange(n1)[None, :]) / L
    bcast = lambda t: jnp.asarray(np.broadcast_to(t[:, :, None], (n2, n1, FNET_GROUP_WIDTH)), F32)
    return (jnp.asarray(np.concatenate([c1, -s1], axis=0), BF16), bcast(np.cos(ang)), bcast(np.sin(ang)),
            jnp.asarray(np.concatenate([c2, -s2], axis=0), BF16), jnp.asarray(np.concatenate([s2, c2], axis=0), BF16),
            jnp.asarray(np.concatenate([cc, sc], axis=0), BF16))


def _proj_c_kernel(h_ref, hp_ref, hn_ref, wqk_ref, wvo_ref, wg_ref, conv_ref, gb_ref,
                   q_ref, k_ref, v_ref, o_ref, g_ref, ext_ref):
    i = pl.program_id(1)
    T, W, halo = ROW_TILE, BRANCH_WIDTH, CONV_HALO
    h = h_ref[0]
    has_prev = (i > 0).astype(F32)
    has_next = (i < pl.num_programs(1) - 1).astype(F32)
    ext_ref[0:halo] = _dot(hp_ref[0], wqk_ref[...]) * has_prev
    ext_ref[halo:halo + T] = _dot(h, wqk_ref[...])
    ext_ref[halo + T:] = _dot(hn_ref[0], wqk_ref[...]) * has_next
    pad = MLSTM_CONV // 2
    acc = ext_ref[pl.ds(halo - pad, T), :] * conv_ref[0:1, :]
    for j in range(1, MLSTM_CONV):
        acc = acc + ext_ref[pl.ds(halo - pad + j, T), :] * conv_ref[j:j + 1, :]
    y = acc * _sigmoid(acc)
    q = (y[:, :W] * (HEAD_DIM ** -0.5)).astype(BF16)
    eye = (lax.broadcasted_iota(jnp.int32, (W, W), 0) == lax.broadcasted_iota(jnp.int32, (W, W), 1)).astype(BF16)
    q_ref[0] = _dot_nt(eye, q).astype(BF16)
    k_ref[0] = y[:, W:].astype(BF16)
    vo_t = _dot_nt(wvo_ref[...], h)
    v_ref[0] = vo_t[:W].astype(BF16)
    o_ref[0] = vo_t[W:].astype(BF16)
    g_ref[0] = _dot(h, wg_ref[...]) + gb_ref[...]


def _proj_c(h, w_qk, w_vo_t, w_g, conv, gate_bias):
    B, L, D = h.shape
    T, W, halo = ROW_TILE, BRANCH_WIDTH, CONV_HALO
    step = T // halo
    nhb = L // halo
    tile = lambda c: pl.BlockSpec((1, T, c), lambda b, i: (b, i, 0))
    tile_t = pl.BlockSpec((1, W, T), lambda b, i: (b, 0, i))
    rows, cols = jax.ShapeDtypeStruct((B, L, W), BF16), jax.ShapeDtypeStruct((B, W, L), BF16)
    return pl.pallas_call(
        _proj_c_kernel,
        name="proj_c",
        grid=(B, L // T),
        in_specs=[tile(D),
                  pl.BlockSpec((1, halo, D), lambda b, i: (b, jnp.maximum(i * step - 1, 0), 0)),
                  pl.BlockSpec((1, halo, D), lambda b, i: (b, jnp.minimum((i + 1) * step, nhb - 1), 0)),
                  _resident((D, 2 * W)), _resident((2 * W, D)), _resident((D, MLSTM_GATES)),
                  _resident((MLSTM_CONV, 2 * W)), _resident((1, MLSTM_GATES))],
        out_specs=[tile_t, tile(W), tile_t, tile_t, tile(MLSTM_GATES)],
        out_shape=[cols, rows, cols, cols, jax.ShapeDtypeStruct((B, L, MLSTM_GATES), F32)],
        scratch_shapes=[pltpu.VMEM((T + 2 * halo, 2 * W), F32)],
        compiler_params=_params("parallel", "arbitrary"),
    )(h, h, h, w_qk, w_vo_t, w_g, conv, gate_bias.reshape(1, MLSTM_GATES))


GATE_U, GATE_M, GATE_W_INTER, GATE_FLOOR, GATE_WA, GATE_S_OLD = range(6)
N_GATE_TERMS = 6
STATE_ROWS = HEAD_DIM + 16


def _gate_terms_kernel(x_ref, out_ref, g_ref, pe_ref, mp_ref, *, reverse):
    nc, T, H = x_ref.shape[1], MLSTM_CHUNK, N_HEADS
    x = x_ref[0].reshape(nc * 8, T)
    sub = lax.broadcasted_iota(jnp.int32, (nc * 8, T), 0) % 8
    lane = lax.broadcasted_iota(jnp.int32, (nc * 8, T), 1)
    src = lax.broadcasted_iota(jnp.int32, (T, T), 0)
    tgt = lax.broadcasted_iota(jnp.int32, (T, T), 1)
    tri = ((src >= tgt) if reverse else (src <= tgt)).astype(F32)
    cs = jnp.dot(jnp.where(sub >= H, _log_sigmoid(x), x), tri,
                 preferred_element_type=F32, precision=lax.Precision.HIGHEST)
    b = pltpu.roll(cs, nc * 8 - H, 0)
    u = x - b
    p = u
    shift = 1
    while shift < T:
        if reverse:
            p = jnp.maximum(p, jnp.where(lane < T - shift, pltpu.roll(p, T - shift, 1), NEG))
        else:
            p = jnp.maximum(p, jnp.where(lane >= shift, pltpu.roll(p, shift, 1), NEG))
        shift *= 2
    last = 0 if reverse else T - 1
    g = jnp.broadcast_to(b[:, last:last + 1], (nc * 8, T))
    p_end = jnp.broadcast_to(p[:, last:last + 1], (nc * 8, T))
    g_ref[...] = g[:, :LANES].reshape(nc, 8, LANES)
    pe_ref[...] = p_end[:, :LANES].reshape(nc, 8, LANES)

    def step(c, m):
        cc = nc - 1 - c if reverse else c
        mp_ref[cc] = m
        return g_ref[cc] + jnp.maximum(m, pe_ref[cc])

    lax.fori_loop(0, nc, step, jnp.zeros((8, LANES), F32))
    m = jnp.concatenate([mp_ref[...].reshape(nc * 8, LANES)] * (T // LANES), axis=1)
    big_m = jnp.maximum(m, p)
    m_next = g + jnp.maximum(m, p_end)
    terms = {GATE_U: u, GATE_M: big_m, GATE_W_INTER: jnp.exp(m - big_m), GATE_FLOOR: jnp.exp(-(b + big_m)),
             GATE_WA: jnp.exp(g + u - m_next), GATE_S_OLD: jnp.exp(g + m - m_next)}
    for j, val in terms.items():
        out_ref[0, :, j] = jnp.where(sub < H, val, 0.0).reshape(nc, 8, T)


def _gate_terms(gates_dir, *, reverse):
    B, nc, _, T = gates_dir.shape
    return pl.pallas_call(
        functools.partial(_gate_terms_kernel, reverse=reverse),
        name="gate_terms",
        grid=(B,),
        in_specs=[pl.BlockSpec((1, nc, 8, T), lambda b: (b, 0, 0, 0))],
        out_specs=pl.BlockSpec((1, nc, N_GATE_TERMS, 8, T), lambda b: (b, 0, 0, 0, 0)),
        out_shape=jax.ShapeDtypeStruct((B, nc, N_GATE_TERMS, 8, T), F32),
        scratch_shapes=[pltpu.VMEM((nc, 8, LANES), F32)] * 3,
        compiler_params=_params("parallel"),
    )(gates_dir)


def _mlstm_kernel(*refs, reverse, finalize):
    if finalize:
        (q_ref, k_ref, v_ref, r_ref, hf_ref, oc_ref, gain_ref, out_ref, c_ref) = refs
    else:
        (q_ref, k_ref, v_ref, r_ref, out_ref, c_ref) = refs
    T, dh, H = MLSTM_CHUNK, HEAD_DIM, N_HEADS

    @pl.when(pl.program_id(1) == 0)
    def _():
        c_ref[...] = jnp.zeros_like(c_ref)

    src = lax.broadcasted_iota(jnp.int32, (T, T), 0)
    tgt = lax.broadcasted_iota(jnp.int32, (T, T), 1)
    feeds = (src >= tgt) if reverse else (src <= tgt)
    u_cols = jnp.transpose(r_ref[0, 0, GATE_U])
    sub = lax.broadcasted_iota(jnp.int32, (STATE_ROWS - dh, T), 0)
    if finalize:
        eye = (src == tgt).astype(BF16)

    for h in range(H):
        c0, c1 = h * dh, (h + 1) * dh
        q_t, k, v_t = q_ref[0, c0:c1, :], k_ref[0, :, c0:c1], v_ref[0, c0:c1, :]
        row = lambda j: r_ref[0, 0, j, h:h + 1, :]
        w_inter, wa = row(GATE_W_INTER), row(GATE_WA)
        state = c_ref[h]

        logw = jnp.where(feeds, u_cols[:, h:h + 1] - row(GATE_M), NEG)
        sqk_t = jnp.exp(logw) * _dot(k, q_t)
        inter = _dot(state.astype(BF16), q_t)
        num_t = w_inter * inter[:dh] + _dot(v_t, sqk_t.astype(BF16))
        den = w_inter * inter[dh:dh + 1] + jnp.sum(sqk_t, axis=0, keepdims=True)
        hout_t = num_t * (1.0 / jnp.maximum(jnp.abs(den), row(GATE_FLOOR)))

        vw = jnp.concatenate([(v_t.astype(F32) * wa).astype(BF16),
                              jnp.where(sub == 0, wa, 0.0).astype(BF16)], axis=0)
        c_ref[h] = row(GATE_S_OLD)[:, :dh] * state + _dot(vw, k)

        if finalize:
            hc = _sigmoid(oc_ref[0, c0:c1, :].astype(F32)) * (hout_t + hf_ref[0, c0:c1, :])
            mu = jnp.mean(hc, axis=0, keepdims=True)
            cen = hc - mu
            var = jnp.mean(cen * cen, axis=0, keepdims=True)
            gain = jnp.concatenate([gain_ref[c0:c1]] * (T // LANES), axis=1)
            yc_t = (cen * lax.rsqrt(var + EPS) * gain).astype(BF16)
            out_ref[0, :, c0:c1] = _dot_nt(eye, yc_t).astype(BF16)
        else:
            out_ref[0, c0:c1, :] = hout_t


def _mlstm_pass(q_t, k, v_t, terms, extra, *, reverse):
    B, L, W = k.shape
    T = MLSTM_CHUNK
    n = L // T
    pos = (lambda i: n - 1 - i) if reverse else (lambda i: i)
    rows = pl.BlockSpec((1, T, W), lambda b, i: (b, pos(i), 0))
    cols = pl.BlockSpec((1, W, T), lambda b, i: (b, 0, pos(i)))
    in_specs = [cols, rows, cols, pl.BlockSpec((1, 1, N_GATE_TERMS, 8, T), lambda b, i: (b, pos(i), 0, 0, 0))]
    finalize = extra is not None
    args = [q_t, k, v_t, terms]
    if finalize:
        in_specs += [cols, cols, _resident((W, LANES))]
        args += list(extra)
    return pl.pallas_call(
        functools.partial(_mlstm_kernel, reverse=reverse, finalize=finalize),
        name="mlstm_bwd" if reverse else "mlstm_fwd",
        grid=(B, n),
        in_specs=in_specs,
        out_specs=rows if finalize else cols,
        out_shape=jax.ShapeDtypeStruct((B, L, W), BF16) if finalize else jax.ShapeDtypeStruct((B, W, L), F32),
        scratch_shapes=[pltpu.VMEM((N_HEADS, STATE_ROWS, HEAD_DIM), F32)],
        compiler_params=_params("parallel", "arbitrary"),
    )(*args)


def _merge_kernel(x_ref, h_ref, ya_ref, yb_ref, yc_ref, yd_ref, wgp_ref, wmg_ref, wbr_ref, wout_ref, g_ref, out_ref):
    W, D = BRANCH_WIDTH, D_MODEL
    h = h_ref[...]
    acc = None
    for g, br_ref in enumerate((ya_ref, yb_ref, yc_ref, yd_ref)):
        gp = _dot(h, wgp_ref[:, g * W:(g + 1) * W])
        gated = (br_ref[...].astype(F32) * (gp * _sigmoid(gp))).astype(BF16)
        proj = _dot(gated, wbr_ref[g])
        term = _sigmoid(_dot(h, wmg_ref[:, g * D:(g + 1) * D])) * proj
        acc = term if acc is None else acc + term
    out = _dot(acc.astype(BF16), wout_ref[...])
    ms = jnp.mean(out * out, axis=-1, keepdims=True)
    out_ref[...] = x_ref[...] + out * lax.rsqrt(ms + EPS) * g_ref[...]


def _merge(x_rows, h_rows, branches, w_gp, w_mg, w_br, w_out, gain):
    R, D = x_rows.shape
    T, W = ROW_TILE, BRANCH_WIDTH
    rows = lambda c: pl.BlockSpec((T, c), lambda i: (i, 0))
    return pl.pallas_call(
        _merge_kernel,
        name="merge",
        grid=(R // T,),
        in_specs=[rows(D), rows(D)] + [rows(W)] * N_BRANCHES
        + [_resident(w_gp.shape), _resident(w_mg.shape), _resident(w_br.shape), _resident(w_out.shape),
           _resident((1, D))],
        out_specs=rows(D),
        out_shape=jax.ShapeDtypeStruct((R, D), F32),
        compiler_params=_params("parallel"),
    )(x_rows, h_rows, *branches, w_gp, w_mg, w_br, w_out, gain.reshape(1, D))


def _t5_bucket(rel):
    nb = REL_BUCKETS // 2
    max_exact = nb // 2
    ret = (rel > 0).astype(np.int32) * nb
    n = np.abs(rel)
    large = max_exact + (np.log(np.maximum(n, 1) / max_exact) / np.log(REL_MAX_DIST / max_exact)
                         * (nb - max_exact)).astype(np.int32)
    large = np.minimum(large, nb - 1)
    return (ret + np.where(n < max_exact, n, large)).astype(np.int32)


def _band_bias(rel_bias, group, dilation):
    qb, hw = ATTN_Q_BLOCK, HALF_WINDOW
    rel = np.arange(qb + 2 * hw)[None, :] - hw - np.arange(qb)[:, None]
    heads = rel_bias[:, group * N_HEADS:(group + 1) * N_HEADS].astype(F32)
    onehot = (jnp.asarray(_t5_bucket(dilation * rel))[None, :, :] == jnp.arange(REL_BUCKETS)[:, None, None])
    bias = jnp.sum(jnp.where(onehot[:, None], heads[:, :, None, None], 0.0), axis=0)
    return jnp.where(jnp.asarray(np.abs(rel) <= hw)[None], bias, NEG)


def _split_weights(w_in, conv_qk, w_mem_kv, w_branch, w_out):
    W, D, A = BRANCH_WIDTH, D_MODEL, A_QKV_WIDTH
    off = np.cumsum([0, A, A, A, W, W, W, W, W, MLSTM_GATES, W, N_BRANCHES * W, N_BRANCHES * D])
    seg = lambda j: w_in[:, off[j]:off[j + 1]]
    qa, ka, va = seg(0), seg(1), seg(2)
    w_a = [jnp.concatenate([t[:, g * W:(g + 1) * W] for t in (qa, ka, va)], axis=1).astype(BF16)
           for g in range(len(DILATIONS))]
    return dict(
        a=w_a,
        bd=jnp.concatenate([seg(3), seg(9)], axis=1).astype(BF16),
        qk=jnp.concatenate([seg(4), seg(5)], axis=1).astype(BF16),
        vo_t=jnp.concatenate([seg(6), seg(7)], axis=1).T.astype(BF16),
        g=seg(8).astype(BF16),
        gp=seg(10).astype(BF16),
        mg=seg(11).astype(BF16),
        conv=conv_qk.astype(F32),
        memkv=w_mem_kv.astype(BF16),
        br=w_branch.astype(BF16),
        out=w_out.astype(BF16),
    )


def _layer(x, mem, biases, fft_tables, wts, norm_pre, gate_bias, head_gain, mem_norm, norm_post):
    B, L, D = x.shape
    W = BRANCH_WIDTH
    h, h4, h16 = _prenorm(x, norm_pre)

    outs, lses = [], []
    for g, (d, hp) in enumerate(zip(DILATIONS, (h, h4, h16))):
        q, kv = _proj_a(hp.reshape(B * L, D), wts["a"][g])
        o, lse = _attention(q.reshape(B * d, L // d, W), kv.reshape(B * d, L // d, 2 * W), biases[g])
        outs.append(o.reshape(B, d, L // d, W))
        lses.append(lse.reshape(B, d, L // d, LANES))
    ya = _combine(*outs, *lses)

    kv_mem = _memkv(mem, mem_norm, wts["memkv"])
    xb, yd = _proj_bd(h, wts["bd"], kv_mem)
    yb = _fnet(xb, fft_tables)

    q_t, km, v_t, oc_t, gates = _proj_c(h, wts["qk"], wts["vo_t"], wts["g"], wts["conv"], gate_bias.reshape(-1))
    T = MLSTM_CHUNK
    gates_t = jnp.swapaxes(gates.reshape(B, L // T, T, MLSTM_GATES), 2, 3)
    terms_fwd = _gate_terms(gates_t[:, :, :8], reverse=False)
    terms_bwd = _gate_terms(gates_t[:, :, 8:], reverse=True)
    gain = jnp.broadcast_to(head_gain.astype(F32)[:, None], (W, LANES))
    h_fwd_t = _mlstm_pass(q_t, km, v_t, terms_fwd, None, reverse=False)
    yc = _mlstm_pass(q_t, km, v_t, terms_bwd, (h_fwd_t, oc_t, gain), reverse=True)

    rows = lambda t: t.reshape(B * L, t.shape[-1])
    y = _merge(rows(x), rows(h), [rows(ya), rows(yb), rows(yc), rows(yd)],
               wts["gp"], wts["mg"], wts["br"], wts["out"], norm_post)
    return y.reshape(B, L, D)


def kernel(x_prompt, x_sample, mem_prompt, mem_sample, rel_bias, norm_pre, w_in, conv_qk, mlstm_gate_bias,
           mlstm_head_gain, mem_norm, w_mem_kv, w_branch, w_out, norm_post):
    depth = w_in.shape[0]
    biases = [_band_bias(rel_bias, g, d) for g, d in enumerate(DILATIONS)]
    layer_wts = [_split_weights(w_in[l], conv_qk[l], w_mem_kv[l], w_branch[l], w_out[l]) for l in range(depth)]
    outs = []
    for x, mem in ((x_prompt, mem_prompt), (x_sample, mem_sample)):
        tables = _fnet_tables(x.shape[1])
        for l in range(depth):
            x = _layer(x, mem, biases, tables, layer_wts[l], norm_pre[l], mlstm_gate_bias[l],
                       mlstm_head_gain[l], mem_norm[l], norm_post[l])
        outs.append(x)
    return tuple(outs)
```

```python
import functools
import math

import numpy as np
import jax
import jax.numpy as jnp
from jax import lax
from jax.experimental import pallas as pl
from jax.experimental.pallas import tpu as pltpu

D_MODEL = 1024
BRANCH_WIDTH = 512
HEAD_DIM = 128
N_HEADS = 4
N_BRANCHES = 4
DILATIONS = (1, 4, 16)
HALF_WINDOW = 64
A_QKV_WIDTH = len(DILATIONS) * BRANCH_WIDTH
REL_BUCKETS = 32
REL_MAX_DIST = 1024
FNET_GROUP_WIDTH = 128
MLSTM_CONV = 5
MLSTM_GATES = 16
N_MEM = 256
EPS = 1e-6
NEG = -1e30
LOG2_E = math.log2(math.e)

F32 = jnp.float32
BF16 = jnp.bfloat16
LANES = 128
LSE_LANES_PER_HEAD = LANES // N_HEADS

VMEM_LIMIT_BYTES = 56 * 1024 * 1024
PERM_TILE = 1024
ROW_TILE = 512
ATTN_Q_BLOCK = 128
MLSTM_CHUNK = 256
MLSTM_CHUNKS_PER_STEP = 2
FFT_ROWS = 8
CONV_HALO = 16
CONV_COLS = 256


def _params(*sem):
    return pltpu.CompilerParams(dimension_semantics=sem, vmem_limit_bytes=VMEM_LIMIT_BYTES)


def _resident(shape):
    nd = len(shape)
    return pl.BlockSpec(shape, lambda *_: (0,) * nd, pipeline_mode=pl.Buffered(1))


def _sigmoid(x):
    return 1.0 / (1.0 + jnp.exp(-x))


def _log_sigmoid(x):
    return jnp.minimum(x, 0.0) - jnp.log(1.0 + jnp.exp(-jnp.abs(x)))


def _dot(a, b):
    return jnp.dot(a, b, preferred_element_type=F32)


def _dot_nt(a, b):
    return lax.dot_general(a, b, (((1,), (1,)), ((), ())), preferred_element_type=F32)


def _dot_tn(a, b):
    return lax.dot_general(a, b, (((0,), (0,)), ((), ())), preferred_element_type=F32)


def _prenorm_kernel(x_ref, g_ref, h_ref, h4_ref, h16_ref, xs_ref):
    T = PERM_TILE
    x = x_ref[0]
    ms = jnp.mean(x * x, axis=-1, keepdims=True)
    xn = x * lax.rsqrt(ms + EPS) * g_ref[...]
    h_ref[0] = xn.astype(BF16)
    for c in range(D_MODEL // LANES):
        xs_ref[c] = xn[:, c * LANES:(c + 1) * LANES]
    for d, out_ref in ((4, h4_ref), (16, h16_ref)):
        for r in range(d):
            for c in range(D_MODEL // LANES):
                out_ref[0, r, :, c * LANES:(c + 1) * LANES] = xs_ref[c, pl.ds(r, T // d, stride=d), :].astype(BF16)


def _prenorm(x, gain):
    B, L, D = x.shape
    T = PERM_TILE
    return pl.pallas_call(
        _prenorm_kernel,
        name="prenorm",
        grid=(B, L // T),
        in_specs=[pl.BlockSpec((1, T, D), lambda b, i: (b, i, 0)),
                  pl.BlockSpec((1, D), lambda b, i: (0, 0))],
        out_specs=[pl.BlockSpec((1, T, D), lambda b, i: (b, i, 0)),
                   pl.BlockSpec((1, 4, T // 4, D), lambda b, i: (b, 0, i, 0)),
                   pl.BlockSpec((1, 16, T // 16, D), lambda b, i: (b, 0, i, 0))],
        out_shape=[jax.ShapeDtypeStruct((B, L, D), BF16),
                   jax.ShapeDtypeStruct((B, 4, L // 4, D), BF16),
                   jax.ShapeDtypeStruct((B, 16, L // 16, D), BF16)],
        scratch_shapes=[pltpu.VMEM((D // LANES, T, LANES), F32)],
        compiler_params=_params("parallel", "parallel"),
    )(x, gain.reshape(1, D))


def _proj_a_kernel(h_ref, w_ref, q_ref, kv_ref):
    z = _dot(h_ref[...], w_ref[...])
    q_ref[...] = z[:, :BRANCH_WIDTH].astype(BF16)
    kv_ref[...] = z[:, BRANCH_WIDTH:].astype(BF16)


def _proj_a(h_rows, w_qkv):
    R, D = h_rows.shape
    T = ROW_TILE
    W = BRANCH_WIDTH
    return pl.pallas_call(
        _proj_a_kernel,
        name="proj_a",
        grid=(R // T,),
        in_specs=[pl.BlockSpec((T, D), lambda i: (i, 0)), _resident((D, 3 * W))],
        out_specs=[pl.BlockSpec((T, W), lambda i: (i, 0)), pl.BlockSpec((T, 2 * W), lambda i: (i, 0))],
        out_shape=[jax.ShapeDtypeStruct((R, W), BF16), jax.ShapeDtypeStruct((R, 2 * W), BF16)],
        compiler_params=_params("parallel"),
    )(h_rows, w_qkv)


def _attn_kernel(q_ref, kv_ref, kvp_ref, kvn_ref, bias_ref, o_ref, lse_ref, win_ref, *, seq_len, tq):
    i = pl.program_id(1)
    hw, qb, W = HALF_WINDOW, ATTN_Q_BLOCK, BRANCH_WIDTH
    kb = qb + 2 * hw
    win_ref[0:hw] = kvp_ref[0]
    win_ref[hw:hw + tq] = kv_ref[0]
    win_ref[hw + tq:] = kvn_ref[0]
    scale = HEAD_DIM ** -0.5
    lane_head = lax.broadcasted_iota(jnp.int32, (qb, LANES), 1) // LSE_LANES_PER_HEAD
    n_blocks = tq // qb
    for j in range(n_blocks):
        a = j * qb
        kidx = i * tq + (a - hw) + lax.broadcasted_iota(jnp.int32, (1, kb), 1)
        valid = (kidx >= 0) & (kidx < seq_len) if j in (0, n_blocks - 1) else None
        lse_all = None
        for h in range(N_HEADS):
            c0, c1 = h * HEAD_DIM, (h + 1) * HEAD_DIM
            q = q_ref[0, a:a + qb, c0:c1]
            k = win_ref[a:a + kb, c0:c1]
            v = win_ref[a:a + kb, W + c0:W + c1]
            s = _dot_nt(q, k) + bias_ref[h]
            if valid is not None:
                s = jnp.where(valid, s, NEG)
            m = jnp.max(s, axis=-1, keepdims=True)
            p = jnp.exp2((s - m) * (scale * LOG2_E))
            l = jnp.sum(p, axis=-1, keepdims=True)
            o = _dot(p.astype(BF16), v) * (1.0 / l)
            o_ref[0, a:a + qb, c0:c1] = o.astype(BF16)
            lse = jnp.broadcast_to(m * scale + jnp.log(l), (qb, LANES))
            lse_all = lse if h == 0 else jnp.where(lane_head == h, lse, lse_all)
        lse_ref[0, a:a + qb, :] = lse_all


def _attention(q, kv, bias):
    S, M, W = q.shape
    hw = HALF_WINDOW
    tq = min(M, 512)
    nhb = M // hw
    step = tq // hw
    kern = functools.partial(_attn_kernel, seq_len=M, tq=tq)
    return pl.pallas_call(
        kern,
        name="attn",
        grid=(S, M // tq),
        in_specs=[pl.BlockSpec((1, tq, W), lambda s, i: (s, i, 0)),
                  pl.BlockSpec((1, tq, 2 * W), lambda s, i: (s, i, 0)),
                  pl.BlockSpec((1, hw, 2 * W), lambda s, i: (s, jnp.maximum(i * step - 1, 0), 0)),
                  pl.BlockSpec((1, hw, 2 * W), lambda s, i: (s, jnp.minimum((i + 1) * step, nhb - 1), 0)),
                  _resident(bias.shape)],
        out_specs=[pl.BlockSpec((1, tq, W), lambda s, i: (s, i, 0)),
                   pl.BlockSpec((1, tq, LANES), lambda s, i: (s, i, 0))],
        out_shape=[jax.ShapeDtypeStruct((S, M, W), BF16), jax.ShapeDtypeStruct((S, M, LANES), F32)],
        scratch_shapes=[pltpu.VMEM((tq + 2 * hw, 2 * W), BF16)],
        compiler_params=_params("parallel", "parallel"),
    )(q, kv, kv, kv, bias)


def _combine_kernel(o1_ref, o4_ref, o16_ref, l1_ref, l4_ref, l16_ref, ya_ref, os_ref, ls_ref):
    T, nc = PERM_TILE, BRANCH_WIDTH // LANES
    for g, (d, o_ref, l_ref) in enumerate(((4, o4_ref, l4_ref), (16, o16_ref, l16_ref))):
        for r in range(d):
            ls_ref[g, pl.ds(r, T // d, stride=d), :] = l_ref[0, r]
            for c in range(nc):
                os_ref[g, c, pl.ds(r, T // d, stride=d), :] = o_ref[0, r, :, c * LANES:(c + 1) * LANES].astype(F32)
    l1, l4, l16 = l1_ref[0, 0], ls_ref[0], ls_ref[1]
    m = jnp.maximum(jnp.maximum(l1, l4), l16)
    e1, e4, e16 = jnp.exp(l1 - m), jnp.exp(l4 - m), jnp.exp(l16 - m)
    inv = 1.0 / (e1 + e4 + e16)
    w1, w4, w16 = e1 * inv, e4 * inv, e16 * inv
    for h in range(N_HEADS):
        c0, c1 = h * HEAD_DIM, (h + 1) * HEAD_DIM
        lh = h * LSE_LANES_PER_HEAD
        ya = (w1[:, lh:lh + 1] * o1_ref[0, 0, :, c0:c1].astype(F32)
              + w4[:, lh:lh + 1] * os_ref[0, h] + w16[:, lh:lh + 1] * os_ref[1, h])
        ya_ref[0, :, c0:c1] = ya.astype(BF16)


def _combine(o1, o4, o16, l1, l4, l16):
    B, _, L, W = o1.shape
    T = PERM_TILE

    def spec(d, c):
        return pl.BlockSpec((1, d, T // d, c), lambda b, i: (b, 0, i, 0))

    return pl.pallas_call(
        _combine_kernel,
        name="combine",
        grid=(B, L // T),
        in_specs=[spec(1, W), spec(4, W), spec(16, W), spec(1, LANES), spec(4, LANES), spec(16, LANES)],
        out_specs=pl.BlockSpec((1, T, W), lambda b, i: (b, i, 0)),
        out_shape=jax.ShapeDtypeStruct((B, L, W), BF16),
        scratch_shapes=[pltpu.VMEM((2, W // LANES, T, LANES), F32), pltpu.VMEM((2, T, LANES), F32)],
        compiler_params=_params("parallel", "parallel"),
    )(o1, o4, o16, l1, l4, l16)


def _memkv_kernel(mem_ref, g_ref, w_ref, kv_ref):
    x = mem_ref[0]
    ms = jnp.mean(x * x, axis=-1, keepdims=True)
    xn = (x * lax.rsqrt(ms + EPS) * g_ref[...]).astype(BF16)
    kv_ref[0] = _dot(xn, w_ref[...]).astype(BF16)


def _memkv(mem, gain, w):
    B, N, D = mem.shape
    W2 = w.shape[1]
    return pl.pallas_call(
        _memkv_kernel,
        name="memkv",
        grid=(B,),
        in_specs=[pl.BlockSpec((1, N, D), lambda b: (b, 0, 0)), pl.BlockSpec((1, D), lambda b: (0, 0)),
                  _resident((D, W2))],
        out_specs=pl.BlockSpec((1, N, W2), lambda b: (b, 0, 0)),
        out_shape=jax.ShapeDtypeStruct((B, N, W2), BF16),
        compiler_params=_params("parallel"),
    )(mem, gain.reshape(1, D), w)


def _proj_bd_kernel(h_ref, w_ref, kv_ref, xb_ref, yd_ref):
    W = BRANCH_WIDTH
    z = _dot(h_ref[0], w_ref[...])
    xb_ref[0] = z[:, :W]
    qd = z[:, W:].astype(BF16)
    scale = HEAD_DIM ** -0.5
    for h in range(N_HEADS):
        c0, c1 = h * HEAD_DIM, (h + 1) * HEAD_DIM
        s = _dot_nt(qd[:, c0:c1], kv_ref[0, :, c0:c1]) * scale
        m = jnp.max(s, axis=-1, keepdims=True)
        p = jnp.exp(s - m)
        l = jnp.sum(p, axis=-1, keepdims=True)
        o = _dot(p.astype(BF16), kv_ref[0, :, W + c0:W + c1]) / l
        yd_ref[0, :, c0:c1] = o.astype(BF16)


def _proj_bd(h, w_bd, kv):
    B, L, D = h.shape
    T, W = ROW_TILE, BRANCH_WIDTH
    return pl.pallas_call(
        _proj_bd_kernel,
        name="proj_bd",
        grid=(B, L // T),
        in_specs=[pl.BlockSpec((1, T, D), lambda b, i: (b, i, 0)), _resident((D, 2 * W)),
                  pl.BlockSpec((1, N_MEM, 2 * W), lambda b, i: (b, 0, 0))],
        out_specs=[pl.BlockSpec((1, T, W), lambda b, i: (b, i, 0)), pl.BlockSpec((1, T, W), lambda b, i: (b, i, 0))],
        out_shape=[jax.ShapeDtypeStruct((B, L, W), F32), jax.ShapeDtypeStruct((B, L, W), BF16)],
        compiler_params=_params("parallel", "parallel"),
    )(h, w_bd, kv)


def _fft_factors(L):
    n1 = 1 << ((L.bit_length() - 1 + 1) // 2)
    return n1, L // n1


def _dft_cos_sin(n):
    ang = 2.0 * np.pi * ((np.arange(n)[:, None] * np.arange(n)[None, :]) % n) / n
    return np.cos(ang), np.sin(ang)


def _fft1_kernel(x_ref, f_ref, tc_ref, ts_ref, yr_ref, yi_ref, xs_ref, *, n1):
    W = BRANCH_WIDTH
    for j in range(FFT_ROWS):
        xs_ref[:, j * W:(j + 1) * W] = x_ref[0, :, j, :]
    y = _dot(f_ref[...], xs_ref[...].astype(BF16))
    reps = W // FNET_GROUP_WIDTH
    for j in range(FFT_ROWS):
        yr, yi = y[:n1, j * W:(j + 1) * W], y[n1:, j * W:(j + 1) * W]
        tc = jnp.concatenate([tc_ref[j]] * reps, axis=1)
        ts = jnp.concatenate([ts_ref[j]] * reps, axis=1)
        yr_ref[0, j] = yr * tc + yi * ts
        yi_ref[0, j] = yi * tc - yr * ts


def _fft2_kernel(yr_ref, yi_ref, g1_ref, g2_ref, cs_ref, out_ref, ys_ref, *, n2, norm):
    W, gw = BRANCH_WIDTH, FNET_GROUP_WIDTH
    for j in range(FFT_ROWS):
        ys_ref[0, :, j * W:(j + 1) * W] = yr_ref[0, :, j, :]
        ys_ref[1, :, j * W:(j + 1) * W] = yi_ref[0, :, j, :]
    u = _dot(g1_ref[...], ys_ref[0].astype(BF16)) + _dot(g2_ref[...], ys_ref[1].astype(BF16))
    top, bot = u[:n2].astype(BF16), u[n2:].astype(BF16)
    ng = FFT_ROWS * W // gw
    lhs = jnp.concatenate(
        [jnp.concatenate([top[:, j * gw:(j + 1) * gw], bot[:, j * gw:(j + 1) * gw]], axis=1) for j in range(ng)],
        axis=0)
    res = _dot(lhs, cs_ref[...]) * norm
    for j in range(ng):
        k1, c = divmod(j, W // gw)
        out_ref[0, :, k1, c * gw:(c + 1) * gw] = res[j * n2:(j + 1) * n2]


def _fnet(xb, tables):
    B, L, W = xb.shape
    n1, n2 = _fft_factors(L)
    f1, twc, tws, g1, g2, cs = tables
    R = FFT_ROWS
    yr, yi = pl.pallas_call(
        functools.partial(_fft1_kernel, n1=n1),
        name="fft1",
        grid=(B, n2 // R),
        in_specs=[pl.BlockSpec((1, n1, R, W), lambda b, j: (b, 0, j, 0)), _resident((2 * n1, n1)),
                  pl.BlockSpec((R, n1, FNET_GROUP_WIDTH), lambda b, j: (j, 0, 0)),
                  pl.BlockSpec((R, n1, FNET_GROUP_WIDTH), lambda b, j: (j, 0, 0))],
        out_specs=[pl.BlockSpec((1, R, n1, W), lambda b, j: (b, j, 0, 0))] * 2,
        out_shape=[jax.ShapeDtypeStruct((B, n2, n1, W), F32)] * 2,
        scratch_shapes=[pltpu.VMEM((n1, R * W), F32)],
        compiler_params=_params("parallel", "parallel"),
    )(xb.reshape(B, n1, n2, W), f1, twc, tws)
    out = pl.pallas_call(
        functools.partial(_fft2_kernel, n2=n2, norm=float(1.0 / math.sqrt(L * FNET_GROUP_WIDTH))),
        name="fft2",
        grid=(B, n1 // R),
        in_specs=[pl.BlockSpec((1, n2, R, W), lambda b, j: (b, 0, j, 0))] * 2
        + [_resident((2 * n2, n2)), _resident((2 * n2, n2)), _resident((2 * FNET_GROUP_WIDTH, FNET_GROUP_WIDTH))],
        out_specs=pl.BlockSpec((1, n2, R, W), lambda b, j: (b, 0, j, 0)),
        out_shape=jax.ShapeDtypeStruct((B, n2, n1, W), F32),
        scratch_shapes=[pltpu.VMEM((2, n2, R * W), F32)],
        compiler_params=_params("parallel", "parallel"),
    )(yr, yi, g1, g2, cs)
    return out.reshape(B, L, W)


def _fnet_tables(L):
    n1, n2 = _fft_factors(L)
    c1, s1 = _dft_cos_sin(n1)
    c2, s2 = _dft_cos_sin(n2)
    cc, sc = _dft_cos_sin(FNET_GROUP_WIDTH)
    ang = 2.0 * np.pi * (np.arange(n2)[:, None] * np.arange(n1)[None, :]) / L
    bcast = lambda t: jnp.asarray(np.broadcast_to(t[:, :, None], (n2, n1, FNET_GROUP_WIDTH)), F32)
    return (jnp.asarray(np.concatenate([c1, -s1], axis=0), BF16), bcast(np.cos(ang)), bcast(np.sin(ang)),
            jnp.asarray(np.concatenate([c2, -s2], axis=0), BF16), jnp.asarray(np.concatenate([s2, c2], axis=0), BF16),
            jnp.asarray(np.concatenate([cc, sc], axis=0), BF16))


def _proj_c_kernel(h_ref, hp_ref, hn_ref, wqk_ref, wvo_ref, wg_ref, conv_ref, gb_ref,
                   q_ref, k_ref, v_ref, o_ref, g_ref):
    i = pl.program_id(1)
    T, W, halo = ROW_TILE, BRANCH_WIDTH, CONV_HALO
    h = h_ref[0]
    has_prev = (i > 0).astype(F32)
    has_next = (i < pl.num_programs(1) - 1).astype(F32)
    pad = MLSTM_CONV // 2
    n_ext = T + 2 * halo
    for c in range(2 * W // CONV_COLS):
        cs = slice(c * CONV_COLS, (c + 1) * CONV_COLS)
        w = wqk_ref[:, cs]
        ext = jnp.concatenate([_dot(hp_ref[0], w) * has_prev, _dot(h, w), _dot(hn_ref[0], w) * has_next], axis=0)
        acc = ext[halo:halo + T] * conv_ref[pad:pad + 1, cs]
        for j in range(MLSTM_CONV):
            if j != pad:
                acc = acc + pltpu.roll(ext, (pad - j) % n_ext, 0)[halo:halo + T] * conv_ref[j:j + 1, cs]
        y = acc * _sigmoid(acc)
        if c < W // CONV_COLS:
            q_ref[0, :, cs] = (y * (HEAD_DIM ** -0.5)).astype(BF16)
        else:
            k_ref[0, :, c * CONV_COLS - W:(c + 1) * CONV_COLS - W] = y.astype(BF16)
    vo_t = _dot_nt(wvo_ref[...], h)
    v_ref[0] = vo_t[:W].astype(BF16)
    o_ref[0] = vo_t[W:].astype(BF16)
    g_ref[0] = _dot(h, wg_ref[...]) + gb_ref[...]


def _proj_c(h, w_qk, w_vo_t, w_g, conv, gate_bias):
    B, L, D = h.shape
    T, W, halo = ROW_TILE, BRANCH_WIDTH, CONV_HALO
    step = T // halo
    nhb = L // halo
    tile = lambda c: pl.BlockSpec((1, T, c), lambda b, i: (b, i, 0))
    tile_t = pl.BlockSpec((1, W, T), lambda b, i: (b, 0, i))
    rows, cols = jax.ShapeDtypeStruct((B, L, W), BF16), jax.ShapeDtypeStruct((B, W, L), BF16)
    return pl.pallas_call(
        _proj_c_kernel,
        name="proj_c",
        grid=(B, L // T),
        in_specs=[tile(D),
                  pl.BlockSpec((1, halo, D), lambda b, i: (b, jnp.maximum(i * step - 1, 0), 0)),
                  pl.BlockSpec((1, halo, D), lambda b, i: (b, jnp.minimum((i + 1) * step, nhb - 1), 0)),
                  _resident((D, 2 * W)), _resident((2 * W, D)), _resident((D, MLSTM_GATES)),
                  _resident((MLSTM_CONV, 2 * W)), _resident((1, MLSTM_GATES))],
        out_specs=[tile(W), tile(W), tile_t, tile_t, tile(MLSTM_GATES)],
        out_shape=[rows, rows, cols, cols, jax.ShapeDtypeStruct((B, L, MLSTM_GATES), F32)],
        compiler_params=_params("parallel", "arbitrary"),
    )(h, h, h, w_qk, w_vo_t, w_g, conv, gate_bias.reshape(1, MLSTM_GATES))


GATE_U, GATE_M, GATE_W_INTER, GATE_FLOOR, GATE_WA, GATE_S_OLD = range(6)
N_GATE_TERMS = 6
STATE_ROWS = HEAD_DIM + 16


def _gate_terms_kernel(x_ref, out_ref, g_ref, pe_ref, mp_ref, *, reverse):
    nc, T, H = x_ref.shape[1], MLSTM_CHUNK, N_HEADS
    x = x_ref[0].reshape(nc * 8, T)
    sub = lax.broadcasted_iota(jnp.int32, (nc * 8, T), 0) % 8
    lane = lax.broadcasted_iota(jnp.int32, (nc * 8, T), 1)
    src = lax.broadcasted_iota(jnp.int32, (T, T), 0)
    tgt = lax.broadcasted_iota(jnp.int32, (T, T), 1)
    tri = ((src >= tgt) if reverse else (src <= tgt)).astype(F32)
    cs = jnp.dot(jnp.where(sub >= H, _log_sigmoid(x), x), tri,
                 preferred_element_type=F32, precision=lax.Precision.HIGHEST)
    b = pltpu.roll(cs, nc * 8 - H, 0)
    u = x - b
    p = u
    shift = 1
    while shift < T:
        if reverse:
            p = jnp.maximum(p, jnp.where(lane < T - shift, pltpu.roll(p, T - shift, 1), NEG))
        else:
            p = jnp.maximum(p, jnp.where(lane >= shift, pltpu.roll(p, shift, 1), NEG))
        shift *= 2
    last = 0 if reverse else T - 1
    g = jnp.broadcast_to(b[:, last:last + 1], (nc * 8, T))
    p_end = jnp.broadcast_to(p[:, last:last + 1], (nc * 8, T))
    g_ref[...] = g[:, :LANES].reshape(nc, 8, LANES)
    pe_ref[...] = p_end[:, :LANES].reshape(nc, 8, LANES)

    def step(c, m):
        cc = nc - 1 - c if reverse else c
        mp_ref[cc] = m
        return g_ref[cc] + jnp.maximum(m, pe_ref[cc])

    lax.fori_loop(0, nc, step, jnp.zeros((8, LANES), F32))
    m = jnp.concatenate([mp_ref[...].reshape(nc * 8, LANES)] * (T // LANES), axis=1)
    big_m = jnp.maximum(m, p)
    m_next = g + jnp.maximum(m, p_end)
    terms = {GATE_U: u, GATE_M: big_m, GATE_W_INTER: jnp.exp(m - big_m), GATE_FLOOR: jnp.exp(-(b + big_m)),
             GATE_WA: jnp.exp(g + u - m_next), GATE_S_OLD: jnp.exp(g + m - m_next)}
    for j, val in terms.items():
        out_ref[0, :, j] = jnp.where(sub < H, val, 0.0).reshape(nc, 8, T)


def _gate_terms(gates_dir, *, reverse):
    B, nc, _, T = gates_dir.shape
    return pl.pallas_call(
        functools.partial(_gate_terms_kernel, reverse=reverse),
        name="gate_terms",
        grid=(B,),
        in_specs=[pl.BlockSpec((1, nc, 8, T), lambda b: (b, 0, 0, 0))],
        out_specs=pl.BlockSpec((1, nc, N_GATE_TERMS, 8, T), lambda b: (b, 0, 0, 0, 0)),
        out_shape=jax.ShapeDtypeStruct((B, nc, N_GATE_TERMS, 8, T), F32),
        scratch_shapes=[pltpu.VMEM((nc, 8, LANES), F32)] * 3,
        compiler_params=_params("parallel"),
    )(gates_dir)


def _mlstm_kernel(*refs, reverse, finalize):
    if finalize:
        (q_ref, k_ref, v_ref, r_ref, hf_ref, oc_ref, gain_ref, out_ref, c_ref) = refs
    else:
        (q_ref, k_ref, v_ref, r_ref, out_ref, c_ref) = refs
    T, dh, H = MLSTM_CHUNK, HEAD_DIM, N_HEADS

    @pl.when(pl.program_id(1) == 0)
    def _():
        c_ref[...] = jnp.zeros_like(c_ref)

    src = lax.broadcasted_iota(jnp.int32, (T, T), 0)
    tgt = lax.broadcasted_iota(jnp.int32, (T, T), 1)
    feeds = (src >= tgt) if reverse else (src <= tgt)
    sub = lax.broadcasted_iota(jnp.int32, (STATE_ROWS - dh, T), 0)
    if finalize:
        eye = (src == tgt).astype(BF16)

    chunks = range(MLSTM_CHUNKS_PER_STEP)
    for ci in (reversed(chunks) if reverse else chunks):
        t0, t1 = ci * T, (ci + 1) * T
        u_cols = jnp.transpose(r_ref[0, ci, GATE_U])
        for h in range(H):
            c0, c1 = h * dh, (h + 1) * dh
            q, k, v_t = q_ref[0, t0:t1, c0:c1], k_ref[0, t0:t1, c0:c1], v_ref[0, c0:c1, t0:t1]
            row = lambda j: r_ref[0, ci, j, h:h + 1, :]
            w_inter, wa = row(GATE_W_INTER), row(GATE_WA)
            state = c_ref[h]

            logw = jnp.where(feeds, u_cols[:, h:h + 1] - row(GATE_M), NEG)
            sqk_t = jnp.exp(logw) * _dot_nt(k, q)
            inter = _dot_nt(state.astype(BF16), q)
            num_t = w_inter * inter[:dh] + _dot(v_t, sqk_t.astype(BF16))
            den = w_inter * inter[dh:dh + 1] + jnp.sum(sqk_t, axis=0, keepdims=True)
            hout_t = num_t * (1.0 / jnp.maximum(jnp.abs(den), row(GATE_FLOOR)))

            vw = jnp.concatenate([(v_t.astype(F32) * wa).astype(BF16),
                                  jnp.where(sub == 0, wa, 0.0).astype(BF16)], axis=0)
            c_ref[h] = row(GATE_S_OLD)[:, :dh] * state + _dot(vw, k)

            if finalize:
                hc = _sigmoid(oc_ref[0, c0:c1, t0:t1].astype(F32)) * (hout_t + hf_ref[0, c0:c1, t0:t1])
                mu = jnp.mean(hc, axis=0, keepdims=True)
                cen = hc - mu
                var = jnp.mean(cen * cen, axis=0, keepdims=True)
                gain = jnp.concatenate([gain_ref[c0:c1]] * (T // LANES), axis=1)
                yc_t = (cen * lax.rsqrt(var + EPS) * gain).astype(BF16)
                out_ref[0, t0:t1, c0:c1] = _dot_nt(eye, yc_t).astype(BF16)
            else:
                out_ref[0, c0:c1, t0:t1] = hout_t


def _mlstm_pass(q, k, v_t, terms, extra, *, reverse):
    B, L, W = k.shape
    cps = MLSTM_CHUNKS_PER_STEP
    T = MLSTM_CHUNK * cps
    n = L // T
    pos = (lambda i: n - 1 - i) if reverse else (lambda i: i)
    rows = pl.BlockSpec((1, T, W), lambda b, i: (b, pos(i), 0))
    cols = pl.BlockSpec((1, W, T), lambda b, i: (b, 0, pos(i)))
    in_specs = [rows, rows, cols,
                pl.BlockSpec((1, cps, N_GATE_TERMS, 8, MLSTM_CHUNK), lambda b, i: (b, pos(i), 0, 0, 0))]
    finalize = extra is not None
    args = [q, k, v_t, terms]
    if finalize:
        in_specs += [cols, cols, _resident((W, LANES))]
        args += list(extra)
    return pl.pallas_call(
        functools.partial(_mlstm_kernel, reverse=reverse, finalize=finalize),
        name="mlstm_bwd" if reverse else "mlstm_fwd",
        grid=(B, n),
        in_specs=in_specs,
        out_specs=rows if finalize else cols,
        out_shape=jax.ShapeDtypeStruct((B, L, W), BF16) if finalize else jax.ShapeDtypeStruct((B, W, L), F32),
        scratch_shapes=[pltpu.VMEM((N_HEADS, STATE_ROWS, HEAD_DIM), F32)],
        compiler_params=_params("parallel", "arbitrary"),
    )(*args)


def _merge_kernel(x_ref, h_ref, ya_ref, yb_ref, yc_ref, yd_ref, wgp_ref, wmg_ref, wbr_ref, wout_ref, g_ref, out_ref):
    W, D = BRANCH_WIDTH, D_MODEL
    h = h_ref[...]
    acc = None
    for g, br_ref in enumerate((ya_ref, yb_ref, yc_ref, yd_ref)):
        gp = _dot(h, wgp_ref[:, g * W:(g + 1) * W])
        gated = (br_ref[...].astype(F32) * (gp * _sigmoid(gp))).astype(BF16)
        proj = _dot(gated, wbr_ref[g])
        term = _sigmoid(_dot(h, wmg_ref[:, g * D:(g + 1) * D])) * proj
        acc = term if acc is None else acc + term
    out = _dot(acc.astype(BF16), wout_ref[...])
    ms = jnp.mean(out * out, axis=-1, keepdims=True)
    out_ref[...] = x_ref[...] + out * lax.rsqrt(ms + EPS) * g_ref[...]


def _merge(x_rows, h_rows, branches, w_gp, w_mg, w_br, w_out, gain):
    R, D = x_rows.shape
    T, W = ROW_TILE, BRANCH_WIDTH
    rows = lambda c: pl.BlockSpec((T, c), lambda i: (i, 0))
    return pl.pallas_call(
        _merge_kernel,
        name="merge",
        grid=(R // T,),
        in_specs=[rows(D), rows(D)] + [rows(W)] * N_BRANCHES
        + [_resident(w_gp.shape), _resident(w_mg.shape), _resident(w_br.shape), _resident(w_out.shape),
           _resident((1, D))],
        out_specs=rows(D),
        out_shape=jax.ShapeDtypeStruct((R, D), F32),
        compiler_params=_params("parallel"),
    )(x_rows, h_rows, *branches, w_gp, w_mg, w_br, w_out, gain.reshape(1, D))


def _t5_bucket(rel):
    nb = REL_BUCKETS // 2
    max_exact = nb // 2
    ret = (rel > 0).astype(np.int32) * nb
    n = np.abs(rel)
    large = max_exact + (np.log(np.maximum(n, 1) / max_exact) / np.log(REL_MAX_DIST / max_exact)
                         * (nb - max_exact)).astype(np.int32)
    large = np.minimum(large, nb - 1)
    return (ret + np.where(n < max_exact, n, large)).astype(np.int32)


def _band_bias(rel_bias, group, dilation):
    qb, hw = ATTN_Q_BLOCK, HALF_WINDOW
    rel = np.arange(qb + 2 * hw)[None, :] - hw - np.arange(qb)[:, None]
    heads = rel_bias[:, group * N_HEADS:(group + 1) * N_HEADS].astype(F32)
    onehot = (jnp.asarray(_t5_bucket(dilation * rel))[None, :, :] == jnp.arange(REL_BUCKETS)[:, None, None])
    bias = jnp.sum(jnp.where(onehot[:, None], heads[:, :, None, None], 0.0), axis=0)
    return jnp.where(jnp.asarray(np.abs(rel) <= hw)[None], bias * (HEAD_DIM ** 0.5), NEG)


def _split_weights(w_in, conv_qk, w_mem_kv, w_branch, w_out):
    W, D, A = BRANCH_WIDTH, D_MODEL, A_QKV_WIDTH
    off = np.cumsum([0, A, A, A, W, W, W, W, W, MLSTM_GATES, W, N_BRANCHES * W, N_BRANCHES * D])
    seg = lambda j: w_in[:, off[j]:off[j + 1]]
    qa, ka, va = seg(0), seg(1), seg(2)
    w_a = [jnp.concatenate([t[:, g * W:(g + 1) * W] for t in (qa, ka, va)], axis=1).astype(BF16)
           for g in range(len(DILATIONS))]
    return dict(
        a=w_a,
        bd=jnp.concatenate([seg(3), seg(9)], axis=1).astype(BF16),
        qk=jnp.concatenate([seg(4), seg(5)], axis=1).astype(BF16),
        vo_t=jnp.concatenate([seg(6), seg(7)], axis=1).T.astype(BF16),
        g=seg(8).astype(BF16),
        gp=seg(10).astype(BF16),
        mg=seg(11).astype(BF16),
        conv=conv_qk.astype(F32),
        memkv=w_mem_kv.astype(BF16),
        br=w_branch.astype(BF16),
        out=w_out.astype(BF16),
    )


def _layer(x, mem, biases, fft_tables, wts, norm_pre, gate_bias, head_gain, mem_norm, norm_post):
    B, L, D = x.shape
    W = BRANCH_WIDTH
    h, h4, h16 = _prenorm(x, norm_pre)

    outs, lses = [], []
    for g, (d, hp) in enumerate(zip(DILATIONS, (h, h4, h16))):
        q, kv = _proj_a(hp.reshape(B * L, D), wts["a"][g])
        o, lse = _attention(q.reshape(B * d, L // d, W), kv.reshape(B * d, L // d, 2 * W), biases[g])
        outs.append(o.reshape(B, d, L // d, W))
        lses.append(lse.reshape(B, d, L // d, LANES))
    ya = _combine(*outs, *lses)

    kv_mem = _memkv(mem, mem_norm, wts["memkv"])
    xb, yd = _proj_bd(h, wts["bd"], kv_mem)
    yb = _fnet(xb, fft_tables)

    qm, km, v_t, oc_t, gates = _proj_c(h, wts["qk"], wts["vo_t"], wts["g"], wts["conv"], gate_bias.reshape(-1))
    T = MLSTM_CHUNK
    gates_t = jnp.swapaxes(gates.reshape(B, L // T, T, MLSTM_GATES), 2, 3)
    terms_fwd = _gate_terms(gates_t[:, :, :8], reverse=False)
    terms_bwd = _gate_terms(gates_t[:, :, 8:], reverse=True)
    gain = jnp.broadcast_to(head_gain.astype(F32)[:, None], (W, LANES))
    h_fwd_t = _mlstm_pass(qm, km, v_t, terms_fwd, None, reverse=False)
    yc = _mlstm_pass(qm, km, v_t, terms_bwd, (h_fwd_t, oc_t, gain), reverse=True)

    rows = lambda t: t.reshape(B * L, t.shape[-1])
    y = _merge(rows(x), rows(h), [rows(ya), rows(yb), rows(yc), rows(yd)],
               wts["gp"], wts["mg"], wts["br"], wts["out"], norm_post)
    return y.reshape(B, L, D)


def kernel(x_prompt, x_sample, mem_prompt, mem_sample, rel_bias, norm_pre, w_in, conv_qk, mlstm_gate_bias,
           mlstm_head_gain, mem_norm, w_mem_kv, w_branch, w_out, norm_post):
    depth = w_in.shape[0]
    biases = [_band_bias(rel_bias, g, d) for g, d in enumerate(DILATIONS)]
    layer_wts = [_split_weights(w_in[l], conv_qk[l], w_mem_kv[l], w_branch[l], w_out[l]) for l in range(depth)]
    outs = []
    for x, mem in ((x_prompt, mem_prompt), (x_sample, mem_sample)):
        tables = _fnet_tables(x.shape[1])
        for l in range(depth):
            x = _layer(x, mem, biases, tables, layer_wts[l], norm_pre[l], mlstm_gate_bias[l],
                       mlstm_head_gain[l], mem_norm[l], norm_post[l])
        outs.append(x)
    return tuple(outs)
```

```python
import functools
import math

import numpy as np
import jax
import jax.numpy as jnp
from jax import lax
from jax.experimental import pallas as pl
from jax.experimental.pallas import tpu as pltpu

D_MODEL = 1024
BRANCH_WIDTH = 512
HEAD_DIM = 128
N_HEADS = 4
N_BRANCHES = 4
DILATIONS = (1, 4, 16)
HALF_WINDOW = 64
A_QKV_WIDTH = len(DILATIONS) * BRANCH_WIDTH
REL_BUCKETS = 32
REL_MAX_DIST = 1024
FNET_GROUP_WIDTH = 128
MLSTM_CONV = 5
MLSTM_GATES = 16
N_MEM = 256
EPS = 1e-6
NEG = -1e30
LOG2_E = math.log2(math.e)

F32 = jnp.float32
BF16 = jnp.bfloat16
LANES = 128
LSE_LANES_PER_HEAD = LANES // N_HEADS

VMEM_LIMIT_BYTES = 56 * 1024 * 1024
PERM_TILE = 1024
ROW_TILE = 512
ATTN_Q_BLOCK = 128
MLSTM_CHUNK = 256
MLSTM_CHUNKS_PER_STEP = 2
FFT_ROWS = 8
CONV_HALO = 16
CONV_COLS = 256


def _params(*sem):
    return pltpu.CompilerParams(dimension_semantics=sem, vmem_limit_bytes=VMEM_LIMIT_BYTES)


def _resident(shape):
    nd = len(shape)
    return pl.BlockSpec(shape, lambda *_: (0,) * nd, pipeline_mode=pl.Buffered(1))


def _sigmoid(x):
    return 1.0 / (1.0 + jnp.exp(-x))


def _log_sigmoid(x):
    return jnp.minimum(x, 0.0) - jnp.log(1.0 + jnp.exp(-jnp.abs(x)))


def _dot(a, b):
    return jnp.dot(a, b, preferred_element_type=F32)


def _dot_nt(a, b):
    return lax.dot_general(a, b, (((1,), (1,)), ((), ())), preferred_element_type=F32)


def _dot_tn(a, b):
    return lax.dot_general(a, b, (((0,), (0,)), ((), ())), preferred_element_type=F32)


def _prenorm_kernel(x_ref, g_ref, h_ref, h4_ref, h16_ref, xs_ref):
    T = PERM_TILE
    x = x_ref[0]
    ms = jnp.mean(x * x, axis=-1, keepdims=True)
    xn = x * lax.rsqrt(ms + EPS) * g_ref[...]
    h_ref[0] = xn.astype(BF16)
    for c in range(D_MODEL // LANES):
        xs_ref[c] = xn[:, c * LANES:(c + 1) * LANES]
    for d, out_ref in ((4, h4_ref), (16, h16_ref)):
        for r in range(d):
            for c in range(D_MODEL // LANES):
                out_ref[0, r, :, c * LANES:(c + 1) * LANES] = xs_ref[c, pl.ds(r, T // d, stride=d), :].astype(BF16)


def _prenorm(x, gain):
    B, L, D = x.shape
    T = PERM_TILE
    return pl.pallas_call(
        _prenorm_kernel,
        name="prenorm",
        grid=(B, L // T),
        in_specs=[pl.BlockSpec((1, T, D), lambda b, i: (b, i, 0)),
                  pl.BlockSpec((1, D), lambda b, i: (0, 0))],
        out_specs=[pl.BlockSpec((1, T, D), lambda b, i: (b, i, 0)),
                   pl.BlockSpec((1, 4, T // 4, D), lambda b, i: (b, 0, i, 0)),
                   pl.BlockSpec((1, 16, T // 16, D), lambda b, i: (b, 0, i, 0))],
        out_shape=[jax.ShapeDtypeStruct((B, L, D), BF16),
                   jax.ShapeDtypeStruct((B, 4, L // 4, D), BF16),
                   jax.ShapeDtypeStruct((B, 16, L // 16, D), BF16)],
        scratch_shapes=[pltpu.VMEM((D // LANES, T, LANES), F32)],
        compiler_params=_params("parallel", "parallel"),
    )(x, gain.reshape(1, D))


def _proj_a_kernel(h_ref, w_ref, q_ref, kv_ref):
    z = _dot(h_ref[...], w_ref[...])
    q_ref[...] = z[:, :BRANCH_WIDTH].astype(BF16)
    kv_ref[...] = z[:, BRANCH_WIDTH:].astype(BF16)


def _proj_a(h_rows, w_qkv):
    R, D = h_rows.shape
    T = ROW_TILE
    W = BRANCH_WIDTH
    return pl.pallas_call(
        _proj_a_kernel,
        name="proj_a",
        grid=(R // T,),
        in_specs=[pl.BlockSpec((T, D), lambda i: (i, 0)), _resident((D, 3 * W))],
        out_specs=[pl.BlockSpec((T, W), lambda i: (i, 0)), pl.BlockSpec((T, 2 * W), lambda i: (i, 0))],
        out_shape=[jax.ShapeDtypeStruct((R, W), BF16), jax.ShapeDtypeStruct((R, 2 * W), BF16)],
        compiler_params=_params("parallel"),
    )(h_rows, w_qkv)


def _attn_kernel(q_ref, kv_ref, kvp_ref, kvn_ref, bias_ref, o_ref, lse_ref, win_ref, *, seq_len, tq):
    i = pl.program_id(1)
    hw, qb, W = HALF_WINDOW, ATTN_Q_BLOCK, BRANCH_WIDTH
    kb = qb + 2 * hw
    win_ref[0:hw] = kvp_ref[0]
    win_ref[hw:hw + tq] = kv_ref[0]
    win_ref[hw + tq:] = kvn_ref[0]
    scale = HEAD_DIM ** -0.5
    lane_head = lax.broadcasted_iota(jnp.int32, (qb, LANES), 1) // LSE_LANES_PER_HEAD
    n_blocks = tq // qb
    for j in range(n_blocks):
        a = j * qb
        kidx = i * tq + (a - hw) + lax.broadcasted_iota(jnp.int32, (1, kb), 1)
        valid = (kidx >= 0) & (kidx < seq_len) if j in (0, n_blocks - 1) else None
        heads = [(h * HEAD_DIM, (h + 1) * HEAD_DIM) for h in range(N_HEADS)]
        scores = [_dot_nt(q_ref[0, a:a + qb, c0:c1], win_ref[a:a + kb, c0:c1]) for c0, c1 in heads]
        probs, stats = [], []
        for h in range(N_HEADS):
            s = scores[h] + bias_ref[h]
            if valid is not None:
                s = jnp.where(valid, s, NEG)
            m = jnp.max(s, axis=-1, keepdims=True)
            p = jnp.exp2((s - m) * (scale * LOG2_E))
            probs.append(p.astype(BF16))
            stats.append((m, jnp.sum(p, axis=-1, keepdims=True)))
        m_all, l_all = stats[0]
        for h, (c0, c1) in enumerate(heads):
            m, l = stats[h]
            o = _dot(probs[h], win_ref[a:a + kb, W + c0:W + c1]) * (1.0 / l)
            o_ref[0, a:a + qb, c0:c1] = o.astype(BF16)
            if h:
                m_all, l_all = jnp.where(lane_head == h, m, m_all), jnp.where(lane_head == h, l, l_all)
        lse_ref[0, a:a + qb, :] = m_all * scale + jnp.log(l_all)


def _attention(q, kv, bias):
    S, M, W = q.shape
    hw = HALF_WINDOW
    tq = min(M, 512)
    nhb = M // hw
    step = tq // hw
    kern = functools.partial(_attn_kernel, seq_len=M, tq=tq)
    return pl.pallas_call(
        kern,
        name="attn",
        grid=(S, M // tq),
        in_specs=[pl.BlockSpec((1, tq, W), lambda s, i: (s, i, 0)),
                  pl.BlockSpec((1, tq, 2 * W), lambda s, i: (s, i, 0)),
                  pl.BlockSpec((1, hw, 2 * W), lambda s, i: (s, jnp.maximum(i * step - 1, 0), 0)),
                  pl.BlockSpec((1, hw, 2 * W), lambda s, i: (s, jnp.minimum((i + 1) * step, nhb - 1), 0)),
                  _resident(bias.shape)],
        out_specs=[pl.BlockSpec((1, tq, W), lambda s, i: (s, i, 0)),
                   pl.BlockSpec((1, tq, LANES), lambda s, i: (s, i, 0))],
        out_shape=[jax.ShapeDtypeStruct((S, M, W), BF16), jax.ShapeDtypeStruct((S, M, LANES), F32)],
        scratch_shapes=[pltpu.VMEM((tq + 2 * hw, 2 * W), BF16)],
        compiler_params=_params("parallel", "parallel"),
    )(q, kv, kv, kv, bias)


def _combine_kernel(o1_ref, o4_ref, o16_ref, l1_ref, l4_ref, l16_ref, ya_ref, os_ref, ls_ref):
    T, nc = PERM_TILE, BRANCH_WIDTH // LANES
    for g, (d, o_ref, l_ref) in enumerate(((4, o4_ref, l4_ref), (16, o16_ref, l16_ref))):
        for r in range(d):
            ls_ref[g, pl.ds(r, T // d, stride=d), :] = l_ref[0, r]
            for c in range(nc):
                os_ref[g, c, pl.ds(r, T // d, stride=d), :] = o_ref[0, r, :, c * LANES:(c + 1) * LANES].astype(F32)
    l1, l4, l16 = l1_ref[0, 0], ls_ref[0], ls_ref[1]
    m = jnp.maximum(jnp.maximum(l1, l4), l16)
    e1, e4, e16 = jnp.exp(l1 - m), jnp.exp(l4 - m), jnp.exp(l16 - m)
    inv = 1.0 / (e1 + e4 + e16)
    w1, w4, w16 = e1 * inv, e4 * inv, e16 * inv
    for h in range(N_HEADS):
        c0, c1 = h * HEAD_DIM, (h + 1) * HEAD_DIM
        lh = h * LSE_LANES_PER_HEAD
        ya = (w1[:, lh:lh + 1] * o1_ref[0, 0, :, c0:c1].astype(F32)
              + w4[:, lh:lh + 1] * os_ref[0, h] + w16[:, lh:lh + 1] * os_ref[1, h])
        ya_ref[0, :, c0:c1] = ya.astype(BF16)


def _combine(o1, o4, o16, l1, l4, l16):
    B, _, L, W = o1.shape
    T = PERM_TILE

    def spec(d, c):
        return pl.BlockSpec((1, d, T // d, c), lambda b, i: (b, 0, i, 0))

    return pl.pallas_call(
        _combine_kernel,
        name="combine",
        grid=(B, L // T),
        in_specs=[spec(1, W), spec(4, W), spec(16, W), spec(1, LANES), spec(4, LANES), spec(16, LANES)],
        out_specs=pl.BlockSpec((1, T, W), lambda b, i: (b, i, 0)),
        out_shape=jax.ShapeDtypeStruct((B, L, W), BF16),
        scratch_shapes=[pltpu.VMEM((2, W // LANES, T, LANES), F32), pltpu.VMEM((2, T, LANES), F32)],
        compiler_params=_params("parallel", "parallel"),
    )(o1, o4, o16, l1, l4, l16)


def _memkv_kernel(mem_ref, g_ref, w_ref, kv_ref):
    x = mem_ref[0]
    ms = jnp.mean(x * x, axis=-1, keepdims=True)
    xn = (x * lax.rsqrt(ms + EPS) * g_ref[...]).astype(BF16)
    kv_ref[0] = _dot(xn, w_ref[...]).astype(BF16)


def _memkv(mem, gain, w):
    B, N, D = mem.shape
    W2 = w.shape[1]
    return pl.pallas_call(
        _memkv_kernel,
        name="memkv",
        grid=(B,),
        in_specs=[pl.BlockSpec((1, N, D), lambda b: (b, 0, 0)), pl.BlockSpec((1, D), lambda b: (0, 0)),
                  _resident((D, W2))],
        out_specs=pl.BlockSpec((1, N, W2), lambda b: (b, 0, 0)),
        out_shape=jax.ShapeDtypeStruct((B, N, W2), BF16),
        compiler_params=_params("parallel"),
    )(mem, gain.reshape(1, D), w)


def _proj_bd_kernel(h_ref, w_ref, kv_ref, xb_ref, yd_ref):
    W = BRANCH_WIDTH
    z = _dot(h_ref[0], w_ref[...])
    xb_ref[0] = z[:, :W]
    qd = z[:, W:].astype(BF16)
    scale = HEAD_DIM ** -0.5
    for h in range(N_HEADS):
        c0, c1 = h * HEAD_DIM, (h + 1) * HEAD_DIM
        s = _dot_nt(qd[:, c0:c1], kv_ref[0, :, c0:c1])
        m = jnp.max(s, axis=-1, keepdims=True)
        p = jnp.exp2((s - m) * (scale * LOG2_E))
        l = jnp.sum(p, axis=-1, keepdims=True)
        o = _dot(p.astype(BF16), kv_ref[0, :, W + c0:W + c1]) * (1.0 / l)
        yd_ref[0, :, c0:c1] = o.astype(BF16)


def _proj_bd(h, w_bd, kv):
    B, L, D = h.shape
    T, W = ROW_TILE, BRANCH_WIDTH
    return pl.pallas_call(
        _proj_bd_kernel,
        name="proj_bd",
        grid=(B, L // T),
        in_specs=[pl.BlockSpec((1, T, D), lambda b, i: (b, i, 0)), _resident((D, 2 * W)),
                  pl.BlockSpec((1, N_MEM, 2 * W), lambda b, i: (b, 0, 0))],
        out_specs=[pl.BlockSpec((1, T, W), lambda b, i: (b, i, 0)), pl.BlockSpec((1, T, W), lambda b, i: (b, i, 0))],
        out_shape=[jax.ShapeDtypeStruct((B, L, W), F32), jax.ShapeDtypeStruct((B, L, W), BF16)],
        compiler_params=_params("parallel", "parallel"),
    )(h, w_bd, kv)


def _fft_factors(L):
    n1 = 1 << ((L.bit_length() - 1 + 1) // 2)
    return n1, L // n1


def _dft_cos_sin(n):
    ang = 2.0 * np.pi * ((np.arange(n)[:, None] * np.arange(n)[None, :]) % n) / n
    return np.cos(ang), np.sin(ang)


def _fft1_kernel(x_ref, f_ref, tc_ref, ts_ref, yr_ref, yi_ref, xs_ref, *, n1):
    W = BRANCH_WIDTH
    for j in range(FFT_ROWS):
        xs_ref[:, j * W:(j + 1) * W] = x_ref[0, :, j, :]
    y = _dot(f_ref[...], xs_ref[...].astype(BF16))
    reps = W // FNET_GROUP_WIDTH
    for j in range(FFT_ROWS):
        yr, yi = y[:n1, j * W:(j + 1) * W], y[n1:, j * W:(j + 1) * W]
        tc = jnp.concatenate([tc_ref[j]] * reps, axis=1)
        ts = jnp.concatenate([ts_ref[j]] * reps, axis=1)
        yr_ref[0, j] = yr * tc + yi * ts
        yi_ref[0, j] = yi * tc - yr * ts


def _fft2_kernel(yr_ref, yi_ref, g1_ref, g2_ref, cs_ref, out_ref, ys_ref, *, n2, norm):
    W, gw = BRANCH_WIDTH, FNET_GROUP_WIDTH
    for j in range(FFT_ROWS):
        ys_ref[0, :, j * W:(j + 1) * W] = yr_ref[0, :, j, :]
        ys_ref[1, :, j * W:(j + 1) * W] = yi_ref[0, :, j, :]
    u = _dot(g1_ref[...], ys_ref[0].astype(BF16)) + _dot(g2_ref[...], ys_ref[1].astype(BF16))
    top, bot = u[:n2].astype(BF16), u[n2:].astype(BF16)
    ng = FFT_ROWS * W // gw
    lhs = jnp.concatenate(
        [jnp.concatenate([top[:, j * gw:(j + 1) * gw], bot[:, j * gw:(j + 1) * gw]], axis=1) for j in range(ng)],
        axis=0)
    res = _dot(lhs, cs_ref[...]) * norm
    for j in range(ng):
        k1, c = divmod(j, W // gw)
        out_ref[0, :, k1, c * gw:(c + 1) * gw] = res[j * n2:(j + 1) * n2]


def _fnet(xb, tables):
    B, L, W = xb.shape
    n1, n2 = _fft_factors(L)
    f1, twc, tws, g1, g2, cs = tables
    R = FFT_ROWS
    yr, yi = pl.pallas_call(
        functools.partial(_fft1_kernel, n1=n1),
        name="fft1",
        grid=(B, n2 // R),
        in_specs=[pl.BlockSpec((1, n1, R, W), lambda b, j: (b, 0, j, 0)), _resident((2 * n1, n1)),
                  pl.BlockSpec((R, n1, FNET_GROUP_WIDTH), lambda b, j: (j, 0, 0)),
                  pl.BlockSpec((R, n1, FNET_GROUP_WIDTH), lambda b, j: (j, 0, 0))],
        out_specs=[pl.BlockSpec((1, R, n1, W), lambda b, j: (b, j, 0, 0))] * 2,
        out_shape=[jax.ShapeDtypeStruct((B, n2, n1, W), F32)] * 2,
        scratch_shapes=[pltpu.VMEM((n1, R * W), F32)],
        compiler_params=_params("parallel", "parallel"),
    )(xb.reshape(B, n1, n2, W), f1, twc, tws)
    out = pl.pallas_call(
        functools.partial(_fft2_kernel, n2=n2, norm=float(1.0 / math.sqrt(L * FNET_GROUP_WIDTH))),
        name="fft2",
        grid=(B, n1 // R),
        in_specs=[pl.BlockSpec((1, n2, R, W), lambda b, j: (b, 0, j, 0))] * 2
        + [_resident((2 * n2, n2)), _resident((2 * n2, n2)), _resident((2 * FNET_GROUP_WIDTH, FNET_GROUP_WIDTH))],
        out_specs=pl.BlockSpec((1, n2, R, W), lambda b, j: (b, 0, j, 0)),
        out_shape=jax.ShapeDtypeStruct((B, n2, n1, W), F32),
        scratch_shapes=[pltpu.VMEM((2, n2, R * W), F32)],
        compiler_params=_params("parallel", "parallel"),
    )(yr, yi, g1, g2, cs)
    return out.reshape(B, L, W)


def _fnet_tables(L):
    n1, n2 = _fft_factors(L)
    c1, s1 = _dft_cos_sin(n1)
    c2, s2 = _dft_cos_sin(n2)
    cc, sc = _dft_cos_sin(FNET_GROUP_WIDTH)
    ang = 2.0 * np.pi * (np.arange(n2)[:, None] * np.arange(n1)[None, :]) / L
    bcast = lambda t: jnp.asarray(np.broadcast_to(t[:, :, None], (n2, n1, FNET_GROUP_WIDTH)), F32)
    return (jnp.asarray(np.concatenate([c1, -s1], axis=0), BF16), bcast(np.cos(ang)), bcast(np.sin(ang)),
            jnp.asarray(np.concatenate([c2, -s2], axis=0), BF16), jnp.asarray(np.concatenate([s2, c2], axis=0), BF16),
            jnp.asarray(np.concatenate([cc, sc], axis=0), BF16))


def _proj_c_kernel(h_ref, hp_ref, hn_ref, wqk_ref, wvo_ref, wg_ref, conv_ref, gb_ref,
                   q_ref, k_ref, v_ref, o_ref, g_ref):
    i = pl.program_id(1)
    T, W, halo = ROW_TILE, BRANCH_WIDTH, CONV_HALO
    h = h_ref[0]
    has_prev = (i > 0).astype(F32)
    has_next = (i < pl.num_programs(1) - 1).astype(F32)
    pad = MLSTM_CONV // 2
    n_ext = T + 2 * halo
    for c in range(2 * W // CONV_COLS):
        cs = slice(c * CONV_COLS, (c + 1) * CONV_COLS)
        w = wqk_ref[:, cs]
        ext = jnp.concatenate([_dot(hp_ref[0], w) * has_prev, _dot(h, w), _dot(hn_ref[0], w) * has_next], axis=0)
        acc = ext[halo:halo + T] * conv_ref[pad:pad + 1, cs]
        for j in range(MLSTM_CONV):
            if j != pad:
                acc = acc + pltpu.roll(ext, (pad - j) % n_ext, 0)[halo:halo + T] * conv_ref[j:j + 1, cs]
        y = acc * _sigmoid(acc)
        if c < W // CONV_COLS:
            q_ref[0, :, cs] = (y * (HEAD_DIM ** -0.5)).astype(BF16)
        else:
            k_ref[0, :, c * CONV_COLS - W:(c + 1) * CONV_COLS - W] = y.astype(BF16)
    vo_t = _dot_nt(wvo_ref[...], h)
    v_ref[0] = vo_t[:W].astype(BF16)
    o_ref[0] = vo_t[W:].astype(BF16)
    g_ref[0] = _dot(h, wg_ref[...]) + gb_ref[...]


def _proj_c(h, w_qk, w_vo_t, w_g, conv, gate_bias):
    B, L, D = h.shape
    T, W, halo = ROW_TILE, BRANCH_WIDTH, CONV_HALO
    step = T // halo
    nhb = L // halo
    tile = lambda c: pl.BlockSpec((1, T, c), lambda b, i: (b, i, 0))
    tile_t = pl.BlockSpec((1, W, T), lambda b, i: (b, 0, i))
    rows, cols = jax.ShapeDtypeStruct((B, L, W), BF16), jax.ShapeDtypeStruct((B, W, L), BF16)
    return pl.pallas_call(
        _proj_c_kernel,
        name="proj_c",
        grid=(B, L // T),
        in_specs=[tile(D),
                  pl.BlockSpec((1, halo, D), lambda b, i: (b, jnp.maximum(i * step - 1, 0), 0)),
                  pl.BlockSpec((1, halo, D), lambda b, i: (b, jnp.minimum((i + 1) * step, nhb - 1), 0)),
                  _resident((D, 2 * W)), _resident((2 * W, D)), _resident((D, MLSTM_GATES)),
                  _resident((MLSTM_CONV, 2 * W)), _resident((1, MLSTM_GATES))],
        out_specs=[tile(W), tile(W), tile_t, tile_t, tile(MLSTM_GATES)],
        out_shape=[rows, rows, cols, cols, jax.ShapeDtypeStruct((B, L, MLSTM_GATES), F32)],
        compiler_params=_params("parallel", "arbitrary"),
    )(h, h, h, w_qk, w_vo_t, w_g, conv, gate_bias.reshape(1, MLSTM_GATES))


GATE_U, GATE_M, GATE_W_INTER, GATE_FLOOR, GATE_WA, GATE_S_OLD = range(6)
N_GATE_TERMS = 6
STATE_ROWS = HEAD_DIM + 16


def _gate_terms_kernel(x_ref, out_ref, g_ref, pe_ref, mp_ref, *, reverse):
    nc, T, H = x_ref.shape[1], MLSTM_CHUNK, N_HEADS
    x = x_ref[0].reshape(nc * 8, T)
    sub = lax.broadcasted_iota(jnp.int32, (nc * 8, T), 0) % 8
    lane = lax.broadcasted_iota(jnp.int32, (nc * 8, T), 1)
    src = lax.broadcasted_iota(jnp.int32, (T, T), 0)
    tgt = lax.broadcasted_iota(jnp.int32, (T, T), 1)
    tri = ((src >= tgt) if reverse else (src <= tgt)).astype(F32)
    cs = jnp.dot(jnp.where(sub >= H, _log_sigmoid(x), x), tri,
                 preferred_element_type=F32, precision=lax.Precision.HIGHEST)
    b = pltpu.roll(cs, nc * 8 - H, 0)
    u = x - b
    p = u
    shift = 1
    while shift < T:
        if reverse:
            p = jnp.maximum(p, jnp.where(lane < T - shift, pltpu.roll(p, T - shift, 1), NEG))
        else:
            p = jnp.maximum(p, jnp.where(lane >= shift, pltpu.roll(p, shift, 1), NEG))
        shift *= 2
    last = 0 if reverse else T - 1
    g = jnp.broadcast_to(b[:, last:last + 1], (nc * 8, T))
    p_end = jnp.broadcast_to(p[:, last:last + 1], (nc * 8, T))
    g_ref[...] = g[:, :LANES].reshape(nc, 8, LANES)
    pe_ref[...] = p_end[:, :LANES].reshape(nc, 8, LANES)

    def step(c, m):
        cc = nc - 1 - c if reverse else c
        mp_ref[cc] = m
        return g_ref[cc] + jnp.maximum(m, pe_ref[cc])

    lax.fori_loop(0, nc, step, jnp.zeros((8, LANES), F32))
    m = jnp.concatenate([mp_ref[...].reshape(nc * 8, LANES)] * (T // LANES), axis=1)
    big_m = jnp.maximum(m, p)
    m_next = g + jnp.maximum(m, p_end)
    terms = {GATE_U: u, GATE_M: big_m, GATE_W_INTER: jnp.exp(m - big_m), GATE_FLOOR: jnp.exp(-(b + big_m)),
             GATE_WA: jnp.exp(g + u - m_next), GATE_S_OLD: jnp.exp(g + m - m_next)}
    for j, val in terms.items():
        out_ref[0, :, j] = jnp.where(sub < H, val, 0.0).reshape(nc, 8, T)


def _gate_terms(gates_dir, *, reverse):
    B, nc, _, T = gates_dir.shape
    return pl.pallas_call(
        functools.partial(_gate_terms_kernel, reverse=reverse),
        name="gate_terms",
        grid=(B,),
        in_specs=[pl.BlockSpec((1, nc, 8, T), lambda b: (b, 0, 0, 0))],
        out_specs=pl.BlockSpec((1, nc, N_GATE_TERMS, 8, T), lambda b: (b, 0, 0, 0, 0)),
        out_shape=jax.ShapeDtypeStruct((B, nc, N_GATE_TERMS, 8, T), F32),
        scratch_shapes=[pltpu.VMEM((nc, 8, LANES), F32)] * 3,
        compiler_params=_params("parallel"),
    )(gates_dir)


def _mlstm_kernel(*refs, reverse, finalize):
    if finalize:
        (q_ref, k_ref, v_ref, r_ref, hf_ref, oc_ref, gain_ref, out_ref, c_ref) = refs
    else:
        (q_ref, k_ref, v_ref, r_ref, out_ref, c_ref) = refs
    T, dh, H = MLSTM_CHUNK, HEAD_DIM, N_HEADS

    @pl.when(pl.program_id(1) == 0)
    def _():
        c_ref[...] = jnp.zeros_like(c_ref)

    src = lax.broadcasted_iota(jnp.int32, (T, T), 0)
    tgt = lax.broadcasted_iota(jnp.int32, (T, T), 1)
    feeds = (src >= tgt) if reverse else (src <= tgt)
    sub = lax.broadcasted_iota(jnp.int32, (STATE_ROWS - dh, T), 0)
    if finalize:
        eye = (src == tgt).astype(BF16)

    chunks = range(MLSTM_CHUNKS_PER_STEP)
    for ci in (reversed(chunks) if reverse else chunks):
        t0, t1 = ci * T, (ci + 1) * T
        u_cols = jnp.transpose(r_ref[0, ci, GATE_U])
        for h in range(H):
            c0, c1 = h * dh, (h + 1) * dh
            q, k, v_t = q_ref[0, t0:t1, c0:c1], k_ref[0, t0:t1, c0:c1], v_ref[0, c0:c1, t0:t1]
            row = lambda j: r_ref[0, ci, j, h:h + 1, :]
            w_inter, wa = row(GATE_W_INTER), row(GATE_WA)
            state = c_ref[h]

            logw = jnp.where(feeds, u_cols[:, h:h + 1] - row(GATE_M), NEG)
            sqk_t = jnp.exp(logw) * _dot_nt(k, q)
            inter = _dot_nt(state.astype(BF16), q)
            num_t = w_inter * inter[:dh] + _dot(v_t, sqk_t.astype(BF16))
            den = w_inter * inter[dh:dh + 1] + jnp.sum(sqk_t, axis=0, keepdims=True)
            hout_t = num_t * (1.0 / jnp.maximum(jnp.abs(den), row(GATE_FLOOR)))

            vw = jnp.concatenate([(v_t.astype(F32) * wa).astype(BF16),
                                  jnp.where(sub == 0, wa, 0.0).astype(BF16)], axis=0)
            c_ref[h] = row(GATE_S_OLD)[:, :dh] * state + _dot(vw, k)

            if finalize:
                hc = _sigmoid(oc_ref[0, c0:c1, t0:t1].astype(F32)) * (hout_t + hf_ref[0, c0:c1, t0:t1])
                mu = jnp.mean(hc, axis=0, keepdims=True)
                cen = hc - mu
                var = jnp.mean(cen * cen, axis=0, keepdims=True)
                gain = jnp.concatenate([gain_ref[c0:c1]] * (T // LANES), axis=1)
                yc_t = (cen * lax.rsqrt(var + EPS) * gain).astype(BF16)
                out_ref[0, t0:t1, c0:c1] = _dot_nt(eye, yc_t).astype(BF16)
            else:
                out_ref[0, c0:c1, t0:t1] = hout_t


def _mlstm_pass(q, k, v_t, terms, extra, *, reverse):
    B, L, W = k.shape
    cps = MLSTM_CHUNKS_PER_STEP
    T = MLSTM_CHUNK * cps
    n = L // T
    pos = (lambda i: n - 1 - i) if reverse else (lambda i: i)
    rows = pl.BlockSpec((1, T, W), lambda b, i: (b, pos(i), 0))
    cols = pl.BlockSpec((1, W, T), lambda b, i: (b, 0, pos(i)))
    in_specs = [rows, rows, cols,
                pl.BlockSpec((1, cps, N_GATE_TERMS, 8, MLSTM_CHUNK), lambda b, i: (b, pos(i), 0, 0, 0))]
    finalize = extra is not None
    args = [q, k, v_t, terms]
    if finalize:
        in_specs += [cols, cols, _resident((W, LANES))]
        args += list(extra)
    return pl.pallas_call(
        functools.partial(_mlstm_kernel, reverse=reverse, finalize=finalize),
        name="mlstm_bwd" if reverse else "mlstm_fwd",
        grid=(B, n),
        in_specs=in_specs,
        out_specs=rows if finalize else cols,
        out_shape=jax.ShapeDtypeStruct((B, L, W), BF16) if finalize else jax.ShapeDtypeStruct((B, W, L), F32),
        scratch_shapes=[pltpu.VMEM((N_HEADS, STATE_ROWS, HEAD_DIM), F32)],
        compiler_params=_params("parallel", "arbitrary"),
    )(*args)


def _merge_kernel(x_ref, h_ref, ya_ref, yb_ref, yc_ref, yd_ref, wgp_ref, wmg_ref, wbr_ref, wout_ref, g_ref, out_ref):
    W, D = BRANCH_WIDTH, D_MODEL
    h = h_ref[...]
    acc = None
    for g, br_ref in enumerate((ya_ref, yb_ref, yc_ref, yd_ref)):
        gp = _dot(h, wgp_ref[:, g * W:(g + 1) * W])
        gated = (br_ref[...].astype(F32) * (gp * _sigmoid(gp))).astype(BF16)
        proj = _dot(gated, wbr_ref[g])
        term = _sigmoid(_dot(h, wmg_ref[:, g * D:(g + 1) * D])) * proj
        acc = term if acc is None else acc + term
    out = _dot(acc.astype(BF16), wout_ref[...])
    ms = jnp.mean(out * out, axis=-1, keepdims=True)
    out_ref[...] = x_ref[...] + out * lax.rsqrt(ms + EPS) * g_ref[...]


def _merge(x_rows, h_rows, branches, w_gp, w_mg, w_br, w_out, gain):
    R, D = x_rows.shape
    T, W = ROW_TILE, BRANCH_WIDTH
    rows = lambda c: pl.BlockSpec((T, c), lambda i: (i, 0))
    return pl.pallas_call(
        _merge_kernel,
        name="merge",
        grid=(R // T,),
        in_specs=[rows(D), rows(D)] + [rows(W)] * N_BRANCHES
        + [_resident(w_gp.shape), _resident(w_mg.shape), _resident(w_br.shape), _resident(w_out.shape),
           _resident((1, D))],
        out_specs=rows(D),
        out_shape=jax.ShapeDtypeStruct((R, D), F32),
        compiler_params=_params("parallel"),
    )(x_rows, h_rows, *branches, w_gp, w_mg, w_br, w_out, gain.reshape(1, D))


def _t5_bucket(rel):
    nb = REL_BUCKETS // 2
    max_exact = nb // 2
    ret = (rel > 0).astype(np.int32) * nb
    n = np.abs(rel)
    large = max_exact + (np.log(np.maximum(n, 1) / max_exact) / np.log(REL_MAX_DIST / max_exact)
                         * (nb - max_exact)).astype(np.int32)
    large = np.minimum(large, nb - 1)
    return (ret + np.where(n < max_exact, n, large)).astype(np.int32)


def _band_bias(rel_bias, group, dilation):
    qb, hw = ATTN_Q_BLOCK, HALF_WINDOW
    rel = np.arange(qb + 2 * hw)[None, :] - hw - np.arange(qb)[:, None]
    heads = rel_bias[:, group * N_HEADS:(group + 1) * N_HEADS].astype(F32)
    onehot = (jnp.asarray(_t5_bucket(dilation * rel))[None, :, :] == jnp.arange(REL_BUCKETS)[:, None, None])
    bias = jnp.sum(jnp.where(onehot[:, None], heads[:, :, None, None], 0.0), axis=0)
    return jnp.where(jnp.asarray(np.abs(rel) <= hw)[None], bias * (HEAD_DIM ** 0.5), NEG)


def _split_weights(w_in, conv_qk, w_mem_kv, w_branch, w_out):
    W, D, A = BRANCH_WIDTH, D_MODEL, A_QKV_WIDTH
    off = np.cumsum([0, A, A, A, W, W, W, W, W, MLSTM_GATES, W, N_BRANCHES * W, N_BRANCHES * D])
    seg = lambda j: w_in[:, off[j]:off[j + 1]]
    qa, ka, va = seg(0), seg(1), seg(2)
    w_a = [jnp.concatenate([t[:, g * W:(g + 1) * W] for t in (qa, ka, va)], axis=1).astype(BF16)
           for g in range(len(DILATIONS))]
    return dict(
        a=w_a,
        bd=jnp.concatenate([seg(3), seg(9)], axis=1).astype(BF16),
        qk=jnp.concatenate([seg(4), seg(5)], axis=1).astype(BF16),
        vo_t=jnp.concatenate([seg(6), seg(7)], axis=1).T.astype(BF16),
        g=seg(8).astype(BF16),
        gp=seg(10).astype(BF16),
        mg=seg(11).astype(BF16),
        conv=conv_qk.astype(F32),
        memkv=w_mem_kv.astype(BF16),
        br=w_branch.astype(BF16),
        out=w_out.astype(BF16),
    )


def _layer(x, mem, biases, fft_tables, wts, norm_pre, gate_bias, head_gain, mem_norm, norm_post):
    B, L, D = x.shape
    W = BRANCH_WIDTH
    h, h4, h16 = _prenorm(x, norm_pre)

    outs, lses = [], []
    for g, (d, hp) in enumerate(zip(DILATIONS, (h, h4, h16))):
        q, kv = _proj_a(hp.reshape(B * L, D), wts["a"][g])
        o, lse = _attention(q.reshape(B * d, L // d, W), kv.reshape(B * d, L // d, 2 * W), biases[g])
        outs.append(o.reshape(B, d, L // d, W))
        lses.append(lse.reshape(B, d, L // d, LANES))
    ya = _combine(*outs, *lses)

    kv_mem = _memkv(mem, mem_norm, wts["memkv"])
    xb, yd = _proj_bd(h, wts["bd"], kv_mem)
    yb = _fnet(xb, fft_tables)

    qm, km, v_t, oc_t, gates = _proj_c(h, wts["qk"], wts["vo_t"], wts["g"], wts["conv"], gate_bias.reshape(-1))
    T = MLSTM_CHUNK
    gates_t = jnp.swapaxes(gates.reshape(B, L // T, T, MLSTM_GATES), 2, 3)
    terms_fwd = _gate_terms(gates_t[:, :, :8], reverse=False)
    terms_bwd = _gate_terms(gates_t[:, :, 8:], reverse=True)
    gain = jnp.broadcast_to(head_gain.astype(F32)[:, None], (W, LANES))
    h_fwd_t = _mlstm_pass(qm, km, v_t, terms_fwd, None, reverse=False)
    yc = _mlstm_pass(qm, km, v_t, terms_bwd, (h_fwd_t, oc_t, gain), reverse=True)

    rows = lambda t: t.reshape(B * L, t.shape[-1])
    y = _merge(rows(x), rows(h), [rows(ya), rows(yb), rows(yc), rows(yd)],
               wts["gp"], wts["mg"], wts["br"], wts["out"], norm_post)
    return y.reshape(B, L, D)


def kernel(x_prompt, x_sample, mem_prompt, mem_sample, rel_bias, norm_pre, w_in, conv_qk, mlstm_gate_bias,
           mlstm_head_gain, mem_norm, w_mem_kv, w_branch, w_out, norm_post):
    depth = w_in.shape[0]
    biases = [_band_bias(rel_bias, g, d) for g, d in enumerate(DILATIONS)]
    layer_wts = [_split_weights(w_in[l], conv_qk[l], w_mem_kv[l], w_branch[l], w_out[l]) for l in range(depth)]
    outs = []
    for x, mem in ((x_prompt, mem_prompt), (x_sample, mem_sample)):
        tables = _fnet_tables(x.shape[1])
        for l in range(depth):
            x = _layer(x, mem, biases, tables, layer_wts[l], norm_pre[l], mlstm_gate_bias[l],
                       mlstm_head_gain[l], mem_norm[l], norm_post[l])
        outs.append(x)
    return tuple(outs)
```

```python
import functools
import math

import numpy as np
import jax
import jax.numpy as jnp
from jax import lax
from jax.experimental import pallas as pl
from jax.experimental.pallas import tpu as pltpu

D_MODEL = 1024
BRANCH_WIDTH = 512
HEAD_DIM = 128
N_HEADS = 4
N_BRANCHES = 4
DILATIONS = (1, 4, 16)
HALF_WINDOW = 64
A_QKV_WIDTH = len(DILATIONS) * BRANCH_WIDTH
REL_BUCKETS = 32
REL_MAX_DIST = 1024
FNET_GROUP_WIDTH = 128
MLSTM_CONV = 5
MLSTM_GATES = 16
N_MEM = 256
EPS = 1e-6
NEG = -1e30
LOG2_E = math.log2(math.e)

F32 = jnp.float32
BF16 = jnp.bfloat16
LANES = 128
LSE_LANES_PER_HEAD = LANES // N_HEADS

VMEM_LIMIT_BYTES = 56 * 1024 * 1024
PERM_TILE = 1024
ROW_TILE = 512
PROJ_TILE = 1024
ATTN_TILE = 1024
ATTN_Q_BLOCK = 128
MLSTM_CHUNK = 256
MLSTM_CHUNKS_PER_STEP = 2
FFT_ROWS = 8
CONV_HALO = 16
CONV_COLS = 256


def _params(*sem):
    return pltpu.CompilerParams(dimension_semantics=sem, vmem_limit_bytes=VMEM_LIMIT_BYTES)


def _resident(shape):
    nd = len(shape)
    return pl.BlockSpec(shape, lambda *_: (0,) * nd, pipeline_mode=pl.Buffered(1))


def _sigmoid(x):
    return 1.0 / (1.0 + jnp.exp(-x))


def _log_sigmoid(x):
    return jnp.minimum(x, 0.0) - jnp.log(1.0 + jnp.exp(-jnp.abs(x)))


def _dot(a, b):
    return jnp.dot(a, b, preferred_element_type=F32)


def _dot_nt(a, b):
    return lax.dot_general(a, b, (((1,), (1,)), ((), ())), preferred_element_type=F32)


def _dot_tn(a, b):
    return lax.dot_general(a, b, (((0,), (0,)), ((), ())), preferred_element_type=F32)


def _prenorm_kernel(x_ref, g_ref, h_ref, h4_ref, h16_ref, xs_ref):
    T = PERM_TILE
    x = x_ref[0]
    ms = jnp.mean(x * x, axis=-1, keepdims=True)
    xn = x * lax.rsqrt(ms + EPS) * g_ref[...]
    h_ref[0] = xn.astype(BF16)
    for c in range(D_MODEL // LANES):
        xs_ref[c] = xn[:, c * LANES:(c + 1) * LANES]
    for d, out_ref in ((4, h4_ref), (16, h16_ref)):
        for r in range(d):
            for c in range(D_MODEL // LANES):
                out_ref[0, r, :, c * LANES:(c + 1) * LANES] = xs_ref[c, pl.ds(r, T // d, stride=d), :].astype(BF16)


def _prenorm(x, gain):
    B, L, D = x.shape
    T = PERM_TILE
    return pl.pallas_call(
        _prenorm_kernel,
        name="prenorm",
        grid=(B, L // T),
        in_specs=[pl.BlockSpec((1, T, D), lambda b, i: (b, i, 0)),
                  pl.BlockSpec((1, D), lambda b, i: (0, 0))],
        out_specs=[pl.BlockSpec((1, T, D), lambda b, i: (b, i, 0)),
                   pl.BlockSpec((1, 4, T // 4, D), lambda b, i: (b, 0, i, 0)),
                   pl.BlockSpec((1, 16, T // 16, D), lambda b, i: (b, 0, i, 0))],
        out_shape=[jax.ShapeDtypeStruct((B, L, D), BF16),
                   jax.ShapeDtypeStruct((B, 4, L // 4, D), BF16),
                   jax.ShapeDtypeStruct((B, 16, L // 16, D), BF16)],
        scratch_shapes=[pltpu.VMEM((D // LANES, T, LANES), F32)],
        compiler_params=_params("parallel", "parallel"),
    )(x, gain.reshape(1, D))


def _proj_a_kernel(h_ref, w_ref, q_ref, kv_ref):
    z = _dot(h_ref[...], w_ref[...])
    q_ref[...] = z[:, :BRANCH_WIDTH].astype(BF16)
    kv_ref[...] = z[:, BRANCH_WIDTH:].astype(BF16)


def _proj_a(h_rows, w_qkv):
    R, D = h_rows.shape
    T = PROJ_TILE
    W = BRANCH_WIDTH
    return pl.pallas_call(
        _proj_a_kernel,
        name="proj_a",
        grid=(R // T,),
        in_specs=[pl.BlockSpec((T, D), lambda i: (i, 0)), _resident((D, 3 * W))],
        out_specs=[pl.BlockSpec((T, W), lambda i: (i, 0)), pl.BlockSpec((T, 2 * W), lambda i: (i, 0))],
        out_shape=[jax.ShapeDtypeStruct((R, W), BF16), jax.ShapeDtypeStruct((R, 2 * W), BF16)],
        compiler_params=_params("parallel"),
    )(h_rows, w_qkv)


def _attn_kernel(q_ref, kv_ref, kvp_ref, kvn_ref, bias_ref, o_ref, lse_ref, win_ref, *, seq_len, tq):
    i = pl.program_id(1)
    hw, qb, W = HALF_WINDOW, ATTN_Q_BLOCK, BRANCH_WIDTH
    kb = qb + 2 * hw
    win_ref[0:hw] = kvp_ref[0]
    win_ref[hw:hw + tq] = kv_ref[0]
    win_ref[hw + tq:] = kvn_ref[0]
    scale = HEAD_DIM ** -0.5
    lane_head = lax.broadcasted_iota(jnp.int32, (qb, LANES), 1) // LSE_LANES_PER_HEAD
    n_blocks = tq // qb
    for j in range(n_blocks):
        a = j * qb
        kidx = i * tq + (a - hw) + lax.broadcasted_iota(jnp.int32, (1, kb), 1)
        valid = (kidx >= 0) & (kidx < seq_len) if j in (0, n_blocks - 1) else None
        lse_all = None
        for h in range(N_HEADS):
            c0, c1 = h * HEAD_DIM, (h + 1) * HEAD_DIM
            q = q_ref[0, a:a + qb, c0:c1]
            k = win_ref[a:a + kb, c0:c1]
            v = win_ref[a:a + kb, W + c0:W + c1]
            s = _dot_nt(q, k) + bias_ref[h]
            if valid is not None:
                s = jnp.where(valid, s, NEG)
            m = jnp.max(s, axis=-1, keepdims=True)
            p = jnp.exp2((s - m) * (scale * LOG2_E))
            l = jnp.sum(p, axis=-1, keepdims=True)
            o = _dot(p.astype(BF16), v) * (1.0 / l)
            o_ref[0, a:a + qb, c0:c1] = o.astype(BF16)
            lse = jnp.broadcast_to(m * scale + jnp.log(l), (qb, LANES))
            lse_all = lse if h == 0 else jnp.where(lane_head == h, lse, lse_all)
        lse_ref[0, a:a + qb, :] = lse_all


def _attention(q, kv, bias):
    S, M, W = q.shape
    hw = HALF_WINDOW
    tq = min(M, ATTN_TILE)
    nhb = M // hw
    step = tq // hw
    kern = functools.partial(_attn_kernel, seq_len=M, tq=tq)
    return pl.pallas_call(
        kern,
        name="attn",
        grid=(S, M // tq),
        in_specs=[pl.BlockSpec((1, tq, W), lambda s, i: (s, i, 0)),
                  pl.BlockSpec((1, tq, 2 * W), lambda s, i: (s, i, 0)),
                  pl.BlockSpec((1, hw, 2 * W), lambda s, i: (s, jnp.maximum(i * step - 1, 0), 0)),
                  pl.BlockSpec((1, hw, 2 * W), lambda s, i: (s, jnp.minimum((i + 1) * step, nhb - 1), 0)),
                  _resident(bias.shape)],
        out_specs=[pl.BlockSpec((1, tq, W), lambda s, i: (s, i, 0)),
                   pl.BlockSpec((1, tq, LANES), lambda s, i: (s, i, 0))],
        out_shape=[jax.ShapeDtypeStruct((S, M, W), BF16), jax.ShapeDtypeStruct((S, M, LANES), F32)],
        scratch_shapes=[pltpu.VMEM((tq + 2 * hw, 2 * W), BF16)],
        compiler_params=_params("parallel", "parallel"),
    )(q, kv, kv, kv, bias)


def _combine_kernel(o1_ref, o4_ref, o16_ref, l1_ref, l4_ref, l16_ref, ya_ref, os_ref, ls_ref):
    T, nc = PERM_TILE, BRANCH_WIDTH // LANES
    for g, (d, o_ref, l_ref) in enumerate(((4, o4_ref, l4_ref), (16, o16_ref, l16_ref))):
        for r in range(d):
            ls_ref[g, pl.ds(r, T // d, stride=d), :] = l_ref[0, r]
            for c in range(nc):
                os_ref[g, c, pl.ds(r, T // d, stride=d), :] = o_ref[0, r, :, c * LANES:(c + 1) * LANES].astype(F32)
    l1, l4, l16 = l1_ref[0, 0], ls_ref[0], ls_ref[1]
    m = jnp.maximum(jnp.maximum(l1, l4), l16)
    e1, e4, e16 = jnp.exp(l1 - m), jnp.exp(l4 - m), jnp.exp(l16 - m)
    inv = 1.0 / (e1 + e4 + e16)
    w1, w4, w16 = e1 * inv, e4 * inv, e16 * inv
    for h in range(N_HEADS):
        c0, c1 = h * HEAD_DIM, (h + 1) * HEAD_DIM
        lh = h * LSE_LANES_PER_HEAD
        ya = (w1[:, lh:lh + 1] * o1_ref[0, 0, :, c0:c1].astype(F32)
              + w4[:, lh:lh + 1] * os_ref[0, h] + w16[:, lh:lh + 1] * os_ref[1, h])
        ya_ref[0, :, c0:c1] = ya.astype(BF16)


def _combine(o1, o4, o16, l1, l4, l16):
    B, _, L, W = o1.shape
    T = PERM_TILE

    def spec(d, c):
        return pl.BlockSpec((1, d, T // d, c), lambda b, i: (b, 0, i, 0))

    return pl.pallas_call(
        _combine_kernel,
        name="combine",
        grid=(B, L // T),
        in_specs=[spec(1, W), spec(4, W), spec(16, W), spec(1, LANES), spec(4, LANES), spec(16, LANES)],
        out_specs=pl.BlockSpec((1, T, W), lambda b, i: (b, i, 0)),
        out_shape=jax.ShapeDtypeStruct((B, L, W), BF16),
        scratch_shapes=[pltpu.VMEM((2, W // LANES, T, LANES), F32), pltpu.VMEM((2, T, LANES), F32)],
        compiler_params=_params("parallel", "parallel"),
    )(o1, o4, o16, l1, l4, l16)


def _memkv_kernel(mem_ref, g_ref, w_ref, kv_ref):
    x = mem_ref[0]
    ms = jnp.mean(x * x, axis=-1, keepdims=True)
    xn = (x * lax.rsqrt(ms + EPS) * g_ref[...]).astype(BF16)
    kv_ref[0] = _dot(xn, w_ref[...]).astype(BF16)


def _memkv(mem, gain, w):
    B, N, D = mem.shape
    W2 = w.shape[1]
    return pl.pallas_call(
        _memkv_kernel,
        name="memkv",
        grid=(B,),
        in_specs=[pl.BlockSpec((1, N, D), lambda b: (b, 0, 0)), pl.BlockSpec((1, D), lambda b: (0, 0)),
                  _resident((D, W2))],
        out_specs=pl.BlockSpec((1, N, W2), lambda b: (b, 0, 0)),
        out_shape=jax.ShapeDtypeStruct((B, N, W2), BF16),
        compiler_params=_params("parallel"),
    )(mem, gain.reshape(1, D), w)


def _proj_bd_kernel(h_ref, w_ref, kv_ref, xb_ref, yd_ref):
    W = BRANCH_WIDTH
    z = _dot(h_ref[0], w_ref[...])
    xb_ref[0] = z[:, :W]
    qd = z[:, W:].astype(BF16)
    scale = HEAD_DIM ** -0.5
    for h in range(N_HEADS):
        c0, c1 = h * HEAD_DIM, (h + 1) * HEAD_DIM
        s = _dot_nt(qd[:, c0:c1], kv_ref[0, :, c0:c1])
        m = jnp.max(s, axis=-1, keepdims=True)
        p = jnp.exp2((s - m) * (scale * LOG2_E))
        l = jnp.sum(p, axis=-1, keepdims=True)
        o = _dot(p.astype(BF16), kv_ref[0, :, W + c0:W + c1]) * (1.0 / l)
        yd_ref[0, :, c0:c1] = o.astype(BF16)


def _proj_bd(h, w_bd, kv):
    B, L, D = h.shape
    T, W = PROJ_TILE, BRANCH_WIDTH
    return pl.pallas_call(
        _proj_bd_kernel,
        name="proj_bd",
        grid=(B, L // T),
        in_specs=[pl.BlockSpec((1, T, D), lambda b, i: (b, i, 0)), _resident((D, 2 * W)),
                  pl.BlockSpec((1, N_MEM, 2 * W), lambda b, i: (b, 0, 0))],
        out_specs=[pl.BlockSpec((1, T, W), lambda b, i: (b, i, 0)), pl.BlockSpec((1, T, W), lambda b, i: (b, i, 0))],
        out_shape=[jax.ShapeDtypeStruct((B, L, W), F32), jax.ShapeDtypeStruct((B, L, W), BF16)],
        compiler_params=_params("parallel", "parallel"),
    )(h, w_bd, kv)


def _fft_factors(L):
    n1 = 1 << ((L.bit_length() - 1 + 1) // 2)
    return n1, L // n1


def _dft_cos_sin(n):
    ang = 2.0 * np.pi * ((np.arange(n)[:, None] * np.arange(n)[None, :]) % n) / n
    return np.cos(ang), np.sin(ang)


def _fft1_kernel(x_ref, f_ref, tc_ref, ts_ref, yr_ref, yi_ref, xs_ref, *, n1):
    W = BRANCH_WIDTH
    for j in range(FFT_ROWS):
        xs_ref[:, j * W:(j + 1) * W] = x_ref[0, :, j, :]
    y = _dot(f_ref[...], xs_ref[...].astype(BF16))
    reps = W // FNET_GROUP_WIDTH
    for j in range(FFT_ROWS):
        yr, yi = y[:n1, j * W:(j + 1) * W], y[n1:, j * W:(j + 1) * W]
        tc = jnp.concatenate([tc_ref[j]] * reps, axis=1)
        ts = jnp.concatenate([ts_ref[j]] * reps, axis=1)
        yr_ref[0, j] = yr * tc + yi * ts
        yi_ref[0, j] = yi * tc - yr * ts


def _fft2_kernel(yr_ref, yi_ref, g1_ref, g2_ref, cs_ref, out_ref, ys_ref, *, n2, norm):
    W, gw = BRANCH_WIDTH, FNET_GROUP_WIDTH
    for j in range(FFT_ROWS):
        ys_ref[0, :, j * W:(j + 1) * W] = yr_ref[0, :, j, :]
        ys_ref[1, :, j * W:(j + 1) * W] = yi_ref[0, :, j, :]
    u = _dot(g1_ref[...], ys_ref[0].astype(BF16)) + _dot(g2_ref[...], ys_ref[1].astype(BF16))
    top, bot = u[:n2].astype(BF16), u[n2:].astype(BF16)
    ng = FFT_ROWS * W // gw
    lhs = jnp.concatenate(
        [jnp.concatenate([top[:, j * gw:(j + 1) * gw], bot[:, j * gw:(j + 1) * gw]], axis=1) for j in range(ng)],
        axis=0)
    res = _dot(lhs, cs_ref[...]) * norm
    for j in range(ng):
        k1, c = divmod(j, W // gw)
        out_ref[0, :, k1, c * gw:(c + 1) * gw] = res[j * n2:(j + 1) * n2]


def _fnet(xb, tables):
    B, L, W = xb.shape
    n1, n2 = _fft_factors(L)
    f1, twc, tws, g1, g2, cs = tables
    R = FFT_ROWS
    yr, yi = pl.pallas_call(
        functools.partial(_fft1_kernel, n1=n1),
        name="fft1",
        grid=(B, n2 // R),
        in_specs=[pl.BlockSpec((1, n1, R, W), lambda b, j: (b, 0, j, 0)), _resident((2 * n1, n1)),
                  pl.BlockSpec((R, n1, FNET_GROUP_WIDTH), lambda b, j: (j, 0, 0)),
                  pl.BlockSpec((R, n1, FNET_GROUP_WIDTH), lambda b, j: (j, 0, 0))],
        out_specs=[pl.BlockSpec((1, R, n1, W), lambda b, j: (b, j, 0, 0))] * 2,
        out_shape=[jax.ShapeDtypeStruct((B, n2, n1, W), F32)] * 2,
        scratch_shapes=[pltpu.VMEM((n1, R * W), F32)],
        compiler_params=_params("parallel", "parallel"),
    )(xb.reshape(B, n1, n2, W), f1, twc, tws)
    out = pl.pallas_call(
        functools.partial(_fft2_kernel, n2=n2, norm=float(1.0 / math.sqrt(L * FNET_GROUP_WIDTH))),
        name="fft2",
        grid=(B, n1 // R),
        in_specs=[pl.BlockSpec((1, n2, R, W), lambda b, j: (b, 0, j, 0))] * 2
        + [_resident((2 * n2, n2)), _resident((2 * n2, n2)), _resident((2 * FNET_GROUP_WIDTH, FNET_GROUP_WIDTH))],
        out_specs=pl.BlockSpec((1, n2, R, W), lambda b, j: (b, 0, j, 0)),
        out_shape=jax.ShapeDtypeStruct((B, n2, n1, W), F32),
        scratch_shapes=[pltpu.VMEM((2, n2, R * W), F32)],
        compiler_params=_params("parallel", "parallel"),
    )(yr, yi, g1, g2, cs)
    return out.reshape(B, L, W)


def _fnet_tables(L):
    n1, n2 = _fft_factors(L)
    c1, s1 = _dft_cos_sin(n1)
    c2, s2 = _dft_cos_sin(n2)
    cc, sc = _dft_cos_sin(FNET_GROUP_WIDTH)
    ang = 2.0 * np.pi * (np.arange(n2)[:, None] * np.arange(n1)[None, :]) / L
    bcast = lambda t: jnp.asarray(np.broadcast_to(t[:, :, None], (n2, n1, FNET_GROUP_WIDTH)), F32)
    return (jnp.asarray(np.concatenate([c1, -s1], axis=0), BF16), bcast(np.cos(ang)), bcast(np.sin(ang)),
            jnp.asarray(np.concatenate([c2, -s2], axis=0), BF16), jnp.asarray(np.concatenate([s2, c2], axis=0), BF16),
            jnp.asarray(np.concatenate([cc, sc], axis=0), BF16))


def _proj_c_kernel(h_ref, hp_ref, hn_ref, wqk_ref, wvo_ref, wg_ref, conv_ref, gb_ref,
                   q_ref, k_ref, v_ref, o_ref, g_ref):
    i = pl.program_id(1)
    T, W, halo = PROJ_TILE, BRANCH_WIDTH, CONV_HALO
    h = h_ref[0]
    has_prev = (i > 0).astype(F32)
    has_next = (i < pl.num_programs(1) - 1).astype(F32)
    pad = MLSTM_CONV // 2
    n_ext = T + 2 * halo
    for c in range(2 * W // CONV_COLS):
        cs = slice(c * CONV_COLS, (c + 1) * CONV_COLS)
        w = wqk_ref[:, cs]
        ext = jnp.concatenate([_dot(hp_ref[0], w) * has_prev, _dot(h, w), _dot(hn_ref[0], w) * has_next], axis=0)
        acc = ext[halo:halo + T] * conv_ref[pad:pad + 1, cs]
        for j in range(MLSTM_CONV):
            if j != pad:
                acc = acc + pltpu.roll(ext, (pad - j) % n_ext, 0)[halo:halo + T] * conv_ref[j:j + 1, cs]
        y = acc * _sigmoid(acc)
        if c < W // CONV_COLS:
            q_ref[0, :, cs] = (y * (HEAD_DIM ** -0.5)).astype(BF16)
        else:
            k_ref[0, :, c * CONV_COLS - W:(c + 1) * CONV_COLS - W] = y.astype(BF16)
    vo_t = _dot_nt(wvo_ref[...], h)
    v_ref[0] = vo_t[:W].astype(BF16)
    o_ref[0] = vo_t[W:].astype(BF16)
    g_ref[0] = _dot(h, wg_ref[...]) + gb_ref[...]


def _proj_c(h, w_qk, w_vo_t, w_g, conv, gate_bias):
    B, L, D = h.shape
    T, W, halo = PROJ_TILE, BRANCH_WIDTH, CONV_HALO
    step = T // halo
    nhb = L // halo
    tile = lambda c: pl.BlockSpec((1, T, c), lambda b, i: (b, i, 0))
    tile_t = pl.BlockSpec((1, W, T), lambda b, i: (b, 0, i))
    rows, cols = jax.ShapeDtypeStruct((B, L, W), BF16), jax.ShapeDtypeStruct((B, W, L), BF16)
    return pl.pallas_call(
        _proj_c_kernel,
        name="proj_c",
        grid=(B, L // T),
        in_specs=[tile(D),
                  pl.BlockSpec((1, halo, D), lambda b, i: (b, jnp.maximum(i * step - 1, 0), 0)),
                  pl.BlockSpec((1, halo, D), lambda b, i: (b, jnp.minimum((i + 1) * step, nhb - 1), 0)),
                  _resident((D, 2 * W)), _resident((2 * W, D)), _resident((D, MLSTM_GATES)),
                  _resident((MLSTM_CONV, 2 * W)), _resident((1, MLSTM_GATES))],
        out_specs=[tile(W), tile(W), tile_t, tile_t, tile(MLSTM_GATES)],
        out_shape=[rows, rows, cols, cols, jax.ShapeDtypeStruct((B, L, MLSTM_GATES), F32)],
        compiler_params=_params("parallel", "arbitrary"),
    )(h, h, h, w_qk, w_vo_t, w_g, conv, gate_bias.reshape(1, MLSTM_GATES))


GATE_U, GATE_M, GATE_W_INTER, GATE_FLOOR, GATE_WA, GATE_S_OLD = range(6)
N_GATE_TERMS = 6
STATE_ROWS = HEAD_DIM + 16


def _gate_terms_kernel(x_ref, out_ref, g_ref, pe_ref, mp_ref, *, reverse):
    nc, T, H = x_ref.shape[1], MLSTM_CHUNK, N_HEADS
    x = x_ref[0].reshape(nc * 8, T)
    sub = lax.broadcasted_iota(jnp.int32, (nc * 8, T), 0) % 8
    lane = lax.broadcasted_iota(jnp.int32, (nc * 8, T), 1)
    src = lax.broadcasted_iota(jnp.int32, (T, T), 0)
    tgt = lax.broadcasted_iota(jnp.int32, (T, T), 1)
    tri = ((src >= tgt) if reverse else (src <= tgt)).astype(F32)
    cs = jnp.dot(jnp.where(sub >= H, _log_sigmoid(x), x), tri,
                 preferred_element_type=F32, precision=lax.Precision.HIGHEST)
    b = pltpu.roll(cs, nc * 8 - H, 0)
    u = x - b
    p = u
    shift = 1
    while shift < T:
        if reverse:
            p = jnp.maximum(p, jnp.where(lane < T - shift, pltpu.roll(p, T - shift, 1), NEG))
        else:
            p = jnp.maximum(p, jnp.where(lane >= shift, pltpu.roll(p, shift, 1), NEG))
        shift *= 2
    last = 0 if reverse else T - 1
    g = jnp.broadcast_to(b[:, last:last + 1], (nc * 8, T))
    p_end = jnp.broadcast_to(p[:, last:last + 1], (nc * 8, T))
    g_ref[...] = g[:, :LANES].reshape(nc, 8, LANES)
    pe_ref[...] = p_end[:, :LANES].reshape(nc, 8, LANES)

    def step(c, m):
        cc = nc - 1 - c if reverse else c
        mp_ref[cc] = m
        return g_ref[cc] + jnp.maximum(m, pe_ref[cc])

    lax.fori_loop(0, nc, step, jnp.zeros((8, LANES), F32))
    m = jnp.concatenate([mp_ref[...].reshape(nc * 8, LANES)] * (T // LANES), axis=1)
    big_m = jnp.maximum(m, p)
    m_next = g + jnp.maximum(m, p_end)
    terms = {GATE_U: u, GATE_M: big_m, GATE_W_INTER: jnp.exp(m - big_m), GATE_FLOOR: jnp.exp(-(b + big_m)),
             GATE_WA: jnp.exp(g + u - m_next), GATE_S_OLD: jnp.exp(g + m - m_next)}
    for j, val in terms.items():
        out_ref[0, :, j] = jnp.where(sub < H, val, 0.0).reshape(nc, 8, T)


def _gate_terms(gates_dir, *, reverse):
    B, nc, _, T = gates_dir.shape
    return pl.pallas_call(
        functools.partial(_gate_terms_kernel, reverse=reverse),
        name="gate_terms",
        grid=(B,),
        in_specs=[pl.BlockSpec((1, nc, 8, T), lambda b: (b, 0, 0, 0))],
        out_specs=pl.BlockSpec((1, nc, N_GATE_TERMS, 8, T), lambda b: (b, 0, 0, 0, 0)),
        out_shape=jax.ShapeDtypeStruct((B, nc, N_GATE_TERMS, 8, T), F32),
        scratch_shapes=[pltpu.VMEM((nc, 8, LANES), F32)] * 3,
        compiler_params=_params("parallel"),
    )(gates_dir)


def _mlstm_kernel(*refs, reverse, finalize):
    if finalize:
        (q_ref, k_ref, v_ref, r_ref, hf_ref, oc_ref, gain_ref, out_ref, c_ref) = refs
    else:
        (q_ref, k_ref, v_ref, r_ref, out_ref, c_ref) = refs
    T, dh, H = MLSTM_CHUNK, HEAD_DIM, N_HEADS

    @pl.when(pl.program_id(1) == 0)
    def _():
        c_ref[...] = jnp.zeros_like(c_ref)

    src = lax.broadcasted_iota(jnp.int32, (T, T), 0)
    tgt = lax.broadcasted_iota(jnp.int32, (T, T), 1)
    feeds = (src >= tgt) if reverse else (src <= tgt)
    sub = lax.broadcasted_iota(jnp.int32, (STATE_ROWS - dh, T), 0)
    if finalize:
        eye = (src == tgt).astype(BF16)

    chunks = range(MLSTM_CHUNKS_PER_STEP)
    for ci in (reversed(chunks) if reverse else chunks):
        t0, t1 = ci * T, (ci + 1) * T
        u_cols = jnp.transpose(r_ref[0, ci, GATE_U])
        for h in range(H):
            c0, c1 = h * dh, (h + 1) * dh
            q, k, v_t = q_ref[0, t0:t1, c0:c1], k_ref[0, t0:t1, c0:c1], v_ref[0, c0:c1, t0:t1]
            row = lambda j: r_ref[0, ci, j, h:h + 1, :]
            w_inter, wa = row(GATE_W_INTER), row(GATE_WA)
            state = c_ref[h]

            logw = jnp.where(feeds, u_cols[:, h:h + 1] - row(GATE_M), NEG)
            sqk_t = jnp.exp(logw) * _dot_nt(k, q)
            inter = _dot_nt(state.astype(BF16), q)
            num_t = w_inter * inter[:dh] + _dot(v_t, sqk_t.astype(BF16))
            den = w_inter * inter[dh:dh + 1] + jnp.sum(sqk_t, axis=0, keepdims=True)
            hout_t = num_t * (1.0 / jnp.maximum(jnp.abs(den), row(GATE_FLOOR)))

            vw = jnp.concatenate([(v_t.astype(F32) * wa).astype(BF16),
                                  jnp.where(sub == 0, wa, 0.0).astype(BF16)], axis=0)
            c_ref[h] = row(GATE_S_OLD)[:, :dh] * state + _dot(vw, k)

            if finalize:
                hc = _sigmoid(oc_ref[0, c0:c1, t0:t1].astype(F32)) * (hout_t + hf_ref[0, c0:c1, t0:t1])
                mu = jnp.mean(hc, axis=0, keepdims=True)
                cen = hc - mu
                var = jnp.mean(cen * cen, axis=0, keepdims=True)
                gain = jnp.concatenate([gain_ref[c0:c1]] * (T // LANES), axis=1)
                yc_t = (cen * lax.rsqrt(var + EPS) * gain).astype(BF16)
                out_ref[0, t0:t1, c0:c1] = _dot_nt(eye, yc_t).astype(BF16)
            else:
                out_ref[0, c0:c1, t0:t1] = hout_t


def _mlstm_pass(q, k, v_t, terms, extra, *, reverse):
    B, L, W = k.shape
    cps = MLSTM_CHUNKS_PER_STEP
    T = MLSTM_CHUNK * cps
    n = L // T
    pos = (lambda i: n - 1 - i) if reverse else (lambda i: i)
    rows = pl.BlockSpec((1, T, W), lambda b, i: (b, pos(i), 0))
    cols = pl.BlockSpec((1, W, T), lambda b, i: (b, 0, pos(i)))
    in_specs = [rows, rows, cols,
                pl.BlockSpec((1, cps, N_GATE_TERMS, 8, MLSTM_CHUNK), lambda b, i: (b, pos(i), 0, 0, 0))]
    finalize = extra is not None
    args = [q, k, v_t, terms]
    if finalize:
        in_specs += [cols, cols, _resident((W, LANES))]
        args += list(extra)
    return pl.pallas_call(
        functools.partial(_mlstm_kernel, reverse=reverse, finalize=finalize),
        name="mlstm_bwd" if reverse else "mlstm_fwd",
        grid=(B, n),
        in_specs=in_specs,
        out_specs=rows if finalize else cols,
        out_shape=jax.ShapeDtypeStruct((B, L, W), BF16) if finalize else jax.ShapeDtypeStruct((B, W, L), F32),
        scratch_shapes=[pltpu.VMEM((N_HEADS, STATE_ROWS, HEAD_DIM), F32)],
        compiler_params=_params("parallel", "arbitrary"),
    )(*args)


def _merge_kernel(x_ref, h_ref, ya_ref, yb_ref, yc_ref, yd_ref, wgp_ref, wmg_ref, wbr_ref, wout_ref, g_ref, out_ref):
    W, D = BRANCH_WIDTH, D_MODEL
    h = h_ref[...]
    acc = None
    for g, br_ref in enumerate((ya_ref, yb_ref, yc_ref, yd_ref)):
        gp = _dot(h, wgp_ref[:, g * W:(g + 1) * W])
        gated = (br_ref[...].astype(F32) * (gp * _sigmoid(gp))).astype(BF16)
        proj = _dot(gated, wbr_ref[g])
        term = _sigmoid(_dot(h, wmg_ref[:, g * D:(g + 1) * D])) * proj
        acc = term if acc is None else acc + term
    out = _dot(acc.astype(BF16), wout_ref[...])
    ms = jnp.mean(out * out, axis=-1, keepdims=True)
    out_ref[...] = x_ref[...] + out * lax.rsqrt(ms + EPS) * g_ref[...]


def _merge(x_rows, h_rows, branches, w_gp, w_mg, w_br, w_out, gain):
    R, D = x_rows.shape
    T, W = ROW_TILE, BRANCH_WIDTH
    rows = lambda c: pl.BlockSpec((T, c), lambda i: (i, 0))
    return pl.pallas_call(
        _merge_kernel,
        name="merge",
        grid=(R // T,),
        in_specs=[rows(D), rows(D)] + [rows(W)] * N_BRANCHES
        + [_resident(w_gp.shape), _resident(w_mg.shape), _resident(w_br.shape), _resident(w_out.shape),
           _resident((1, D))],
        out_specs=rows(D),
        out_shape=jax.ShapeDtypeStruct((R, D), F32),
        compiler_params=_params("parallel"),
    )(x_rows, h_rows, *branches, w_gp, w_mg, w_br, w_out, gain.reshape(1, D))


def _t5_bucket(rel):
    nb = REL_BUCKETS // 2
    max_exact = nb // 2
    ret = (rel > 0).astype(np.int32) * nb
    n = np.abs(rel)
    large = max_exact + (np.log(np.maximum(n, 1) / max_exact) / np.log(REL_MAX_DIST / max_exact)
                         * (nb - max_exact)).astype(np.int32)
    large = np.minimum(large, nb - 1)
    return (ret + np.where(n < max_exact, n, large)).astype(np.int32)


def _band_bias(rel_bias, group, dilation):
    qb, hw = ATTN_Q_BLOCK, HALF_WINDOW
    rel = np.arange(qb + 2 * hw)[None, :] - hw - np.arange(qb)[:, None]
    heads = rel_bias[:, group * N_HEADS:(group + 1) * N_HEADS].astype(F32)
    onehot = (jnp.asarray(_t5_bucket(dilation * rel))[None, :, :] == jnp.arange(REL_BUCKETS)[:, None, None])
    bias = jnp.sum(jnp.where(onehot[:, None], heads[:, :, None, None], 0.0), axis=0)
    return jnp.where(jnp.asarray(np.abs(rel) <= hw)[None], bias * (HEAD_DIM ** 0.5), NEG)


def _split_weights(w_in, conv_qk, w_mem_kv, w_branch, w_out):
    W, D, A = BRANCH_WIDTH, D_MODEL, A_QKV_WIDTH
    off = np.cumsum([0, A, A, A, W, W, W, W, W, MLSTM_GATES, W, N_BRANCHES * W, N_BRANCHES * D])
    seg = lambda j: w_in[:, off[j]:off[j + 1]]
    qa, ka, va = seg(0), seg(1), seg(2)
    w_a = [jnp.concatenate([t[:, g * W:(g + 1) * W] for t in (qa, ka, va)], axis=1).astype(BF16)
           for g in range(len(DILATIONS))]
    return dict(
        a=w_a,
        bd=jnp.concatenate([seg(3), seg(9)], axis=1).astype(BF16),
        qk=jnp.concatenate([seg(4), seg(5)], axis=1).astype(BF16),
        vo_t=jnp.concatenate([seg(6), seg(7)], axis=1).T.astype(BF16),
        g=seg(8).astype(BF16),
        gp=seg(10).astype(BF16),
        mg=seg(11).astype(BF16),
        conv=conv_qk.astype(F32),
        memkv=w_mem_kv.astype(BF16),
        br=w_branch.astype(BF16),
        out=w_out.astype(BF16),
    )


def _layer(x, mem, biases, fft_tables, wts, norm_pre, gate_bias, head_gain, mem_norm, norm_post):
    B, L, D = x.shape
    W = BRANCH_WIDTH
    h, h4, h16 = _prenorm(x, norm_pre)

    outs, lses = [], []
    for g, (d, hp) in enumerate(zip(DILATIONS, (h, h4, h16))):
        q, kv = _proj_a(hp.reshape(B * L, D), wts["a"][g])
        o, lse = _attention(q.reshape(B * d, L // d, W), kv.reshape(B * d, L // d, 2 * W), biases[g])
        outs.append(o.reshape(B, d, L // d, W))
        lses.append(lse.reshape(B, d, L // d, LANES))
    ya = _combine(*outs, *lses)

    kv_mem = _memkv(mem, mem_norm, wts["memkv"])
    xb, yd = _proj_bd(h, wts["bd"], kv_mem)
    yb = _fnet(xb, fft_tables)

    qm, km, v_t, oc_t, gates = _proj_c(h, wts["qk"], wts["vo_t"], wts["g"], wts["conv"], gate_bias.reshape(-1))
    T = MLSTM_CHUNK
    gates_t = jnp.swapaxes(gates.reshape(B, L // T, T, MLSTM_GATES), 2, 3)
    terms_fwd = _gate_terms(gates_t[:, :, :8], reverse=False)
    terms_bwd = _gate_terms(gates_t[:, :, 8:], reverse=True)
    gain = jnp.broadcast_to(head_gain.astype(F32)[:, None], (W, LANES))
    h_fwd_t = _mlstm_pass(qm, km, v_t, terms_fwd, None, reverse=False)
    yc = _mlstm_pass(qm, km, v_t, terms_bwd, (h_fwd_t, oc_t, gain), reverse=True)

    rows = lambda t: t.reshape(B * L, t.shape[-1])
    y = _merge(rows(x), rows(h), [rows(ya), rows(yb), rows(yc), rows(yd)],
               wts["gp"], wts["mg"], wts["br"], wts["out"], norm_post)
    return y.reshape(B, L, D)


def kernel(x_prompt, x_sample, mem_prompt, mem_sample, rel_bias, norm_pre, w_in, conv_qk, mlstm_gate_bias,
           mlstm_head_gain, mem_norm, w_mem_kv, w_branch, w_out, norm_post):
    depth = w_in.shape[0]
    biases = [_band_bias(rel_bias, g, d) for g, d in enumerate(DILATIONS)]
    layer_wts = [_split_weights(w_in[l], conv_qk[l], w_mem_kv[l], w_branch[l], w_out[l]) for l in range(depth)]
    outs = []
    for x, mem in ((x_prompt, mem_prompt), (x_sample, mem_sample)):
        tables = _fnet_tables(x.shape[1])
        for l in range(depth):
            x = _layer(x, mem, biases, tables, layer_wts[l], norm_pre[l], mlstm_gate_bias[l],
                       mlstm_head_gain[l], mem_norm[l], norm_post[l])
        outs.append(x)
    return tuple(outs)
```

```python
import functools
import math

import numpy as np
import jax
import jax.numpy as jnp
from jax import lax
from jax.experimental import pallas as pl
from jax.experimental.pallas import tpu as pltpu

D_MODEL = 1024
BRANCH_WIDTH = 512
HEAD_DIM = 128
N_HEADS = 4
N_BRANCHES = 4
DILATIONS = (1, 4, 16)
HALF_WINDOW = 64
A_QKV_WIDTH = len(DILATIONS) * BRANCH_WIDTH
REL_BUCKETS = 32
REL_MAX_DIST = 1024
FNET_GROUP_WIDTH = 128
MLSTM_CONV = 5
MLSTM_GATES = 16
N_MEM = 256
EPS = 1e-6
NEG = -1e30
LOG2_E = math.log2(math.e)

F32 = jnp.float32
BF16 = jnp.bfloat16
LANES = 128
LSE_LANES_PER_HEAD = LANES // N_HEADS

VMEM_LIMIT_BYTES = 56 * 1024 * 1024
PERM_TILE = 1024
ROW_TILE = 512
PROJ_TILE = 2048
ATTN_TILE = 2048
ATTN_Q_BLOCK = 128
MLSTM_CHUNK = 256
MLSTM_CHUNKS_PER_STEP = 4
FFT_ROWS = 8
CONV_HALO = 16
CONV_COLS = 256


def _params(*sem):
    return pltpu.CompilerParams(dimension_semantics=sem, vmem_limit_bytes=VMEM_LIMIT_BYTES)


def _resident(shape):
    nd = len(shape)
    return pl.BlockSpec(shape, lambda *_: (0,) * nd, pipeline_mode=pl.Buffered(1))


def _sigmoid(x):
    return 1.0 / (1.0 + jnp.exp(-x))


def _log_sigmoid(x):
    return jnp.minimum(x, 0.0) - jnp.log(1.0 + jnp.exp(-jnp.abs(x)))


def _dot(a, b):
    return jnp.dot(a, b, preferred_element_type=F32)


def _dot_nt(a, b):
    return lax.dot_general(a, b, (((1,), (1,)), ((), ())), preferred_element_type=F32)


def _dot_tn(a, b):
    return lax.dot_general(a, b, (((0,), (0,)), ((), ())), preferred_element_type=F32)


def _prenorm_kernel(x_ref, g_ref, h_ref, h4_ref, h16_ref, xs_ref):
    T = PERM_TILE
    x = x_ref[0]
    ms = jnp.mean(x * x, axis=-1, keepdims=True)
    xn = x * lax.rsqrt(ms + EPS) * g_ref[...]
    h_ref[0] = xn.astype(BF16)
    for c in range(D_MODEL // LANES):
        xs_ref[c] = xn[:, c * LANES:(c + 1) * LANES]
    for d, out_ref in ((4, h4_ref), (16, h16_ref)):
        for r in range(d):
            for c in range(D_MODEL // LANES):
                out_ref[0, r, :, c * LANES:(c + 1) * LANES] = xs_ref[c, pl.ds(r, T // d, stride=d), :].astype(BF16)


def _prenorm(x, gain):
    B, L, D = x.shape
    T = PERM_TILE
    return pl.pallas_call(
        _prenorm_kernel,
        name="prenorm",
        grid=(B, L // T),
        in_specs=[pl.BlockSpec((1, T, D), lambda b, i: (b, i, 0)),
                  pl.BlockSpec((1, D), lambda b, i: (0, 0))],
        out_specs=[pl.BlockSpec((1, T, D), lambda b, i: (b, i, 0)),
                   pl.BlockSpec((1, 4, T // 4, D), lambda b, i: (b, 0, i, 0)),
                   pl.BlockSpec((1, 16, T // 16, D), lambda b, i: (b, 0, i, 0))],
        out_shape=[jax.ShapeDtypeStruct((B, L, D), BF16),
                   jax.ShapeDtypeStruct((B, 4, L // 4, D), BF16),
                   jax.ShapeDtypeStruct((B, 16, L // 16, D), BF16)],
        scratch_shapes=[pltpu.VMEM((D // LANES, T, LANES), F32)],
        compiler_params=_params("parallel", "parallel"),
    )(x, gain.reshape(1, D))


def _proj_a_kernel(h_ref, w_ref, q_ref, kv_ref):
    z = _dot(h_ref[...], w_ref[...])
    q_ref[...] = z[:, :BRANCH_WIDTH].astype(BF16)
    kv_ref[...] = z[:, BRANCH_WIDTH:].astype(BF16)


def _proj_a(h_rows, w_qkv):
    R, D = h_rows.shape
    T = PROJ_TILE
    W = BRANCH_WIDTH
    return pl.pallas_call(
        _proj_a_kernel,
        name="proj_a",
        grid=(R // T,),
        in_specs=[pl.BlockSpec((T, D), lambda i: (i, 0)), _resident((D, 3 * W))],
        out_specs=[pl.BlockSpec((T, W), lambda i: (i, 0)), pl.BlockSpec((T, 2 * W), lambda i: (i, 0))],
        out_shape=[jax.ShapeDtypeStruct((R, W), BF16), jax.ShapeDtypeStruct((R, 2 * W), BF16)],
        compiler_params=_params("parallel"),
    )(h_rows, w_qkv)


def _attn_kernel(q_ref, kv_ref, kvp_ref, kvn_ref, bias_ref, o_ref, lse_ref, win_ref, *, seq_len, tq):
    i = pl.program_id(1)
    hw, qb, W = HALF_WINDOW, ATTN_Q_BLOCK, BRANCH_WIDTH
    kb = qb + 2 * hw
    win_ref[0:hw] = kvp_ref[0]
    win_ref[hw:hw + tq] = kv_ref[0]
    win_ref[hw + tq:] = kvn_ref[0]
    scale = HEAD_DIM ** -0.5
    lane_head = lax.broadcasted_iota(jnp.int32, (qb, LANES), 1) // LSE_LANES_PER_HEAD
    n_blocks = tq // qb
    for j in range(n_blocks):
        a = j * qb
        kidx = i * tq + (a - hw) + lax.broadcasted_iota(jnp.int32, (1, kb), 1)
        valid = (kidx >= 0) & (kidx < seq_len) if j in (0, n_blocks - 1) else None
        lse_all = None
        for h in range(N_HEADS):
            c0, c1 = h * HEAD_DIM, (h + 1) * HEAD_DIM
            q = q_ref[0, a:a + qb, c0:c1]
            k = win_ref[a:a + kb, c0:c1]
            v = win_ref[a:a + kb, W + c0:W + c1]
            s = _dot_nt(q, k) + bias_ref[h]
            if valid is not None:
                s = jnp.where(valid, s, NEG)
            m = jnp.max(s, axis=-1, keepdims=True)
            p = jnp.exp2((s - m) * (scale * LOG2_E))
            l = jnp.sum(p, axis=-1, keepdims=True)
            o = _dot(p.astype(BF16), v) * (1.0 / l)
            o_ref[0, a:a + qb, c0:c1] = o.astype(BF16)
            lse = jnp.broadcast_to(m * scale + jnp.log(l), (qb, LANES))
            lse_all = lse if h == 0 else jnp.where(lane_head == h, lse, lse_all)
        lse_ref[0, a:a + qb, :] = lse_all


def _attention(q, kv, bias):
    S, M, W = q.shape
    hw = HALF_WINDOW
    tq = min(M, ATTN_TILE)
    nhb = M // hw
    step = tq // hw
    kern = functools.partial(_attn_kernel, seq_len=M, tq=tq)
    return pl.pallas_call(
        kern,
        name="attn",
        grid=(S, M // tq),
        in_specs=[pl.BlockSpec((1, tq, W), lambda s, i: (s, i, 0)),
                  pl.BlockSpec((1, tq, 2 * W), lambda s, i: (s, i, 0)),
                  pl.BlockSpec((1, hw, 2 * W), lambda s, i: (s, jnp.maximum(i * step - 1, 0), 0)),
                  pl.BlockSpec((1, hw, 2 * W), lambda s, i: (s, jnp.minimum((i + 1) * step, nhb - 1), 0)),
                  _resident(bias.shape)],
        out_specs=[pl.BlockSpec((1, tq, W), lambda s, i: (s, i, 0)),
                   pl.BlockSpec((1, tq, LANES), lambda s, i: (s, i, 0))],
        out_shape=[jax.ShapeDtypeStruct((S, M, W), BF16), jax.ShapeDtypeStruct((S, M, LANES), F32)],
        scratch_shapes=[pltpu.VMEM((tq + 2 * hw, 2 * W), BF16)],
        compiler_params=_params("parallel", "parallel"),
    )(q, kv, kv, kv, bias)


def _combine_kernel(o1_ref, o4_ref, o16_ref, l1_ref, l4_ref, l16_ref, ya_ref, os_ref, ls_ref):
    T, nc = PERM_TILE, BRANCH_WIDTH // LANES
    for g, (d, o_ref, l_ref) in enumerate(((4, o4_ref, l4_ref), (16, o16_ref, l16_ref))):
        for r in range(d):
            ls_ref[g, pl.ds(r, T // d, stride=d), :] = l_ref[0, r]
            for c in range(nc):
                os_ref[g, c, pl.ds(r, T // d, stride=d), :] = o_ref[0, r, :, c * LANES:(c + 1) * LANES].astype(F32)
    l1, l4, l16 = l1_ref[0, 0], ls_ref[0], ls_ref[1]
    m = jnp.maximum(jnp.maximum(l1, l4), l16)
    e1, e4, e16 = jnp.exp(l1 - m), jnp.exp(l4 - m), jnp.exp(l16 - m)
    inv = 1.0 / (e1 + e4 + e16)
    w1, w4, w16 = e1 * inv, e4 * inv, e16 * inv
    for h in range(N_HEADS):
        c0, c1 = h * HEAD_DIM, (h + 1) * HEAD_DIM
        lh = h * LSE_LANES_PER_HEAD
        ya = (w1[:, lh:lh + 1] * o1_ref[0, 0, :, c0:c1].astype(F32)
              + w4[:, lh:lh + 1] * os_ref[0, h] + w16[:, lh:lh + 1] * os_ref[1, h])
        ya_ref[0, :, c0:c1] = ya.astype(BF16)


def _combine(o1, o4, o16, l1, l4, l16):
    B, _, L, W = o1.shape
    T = PERM_TILE

    def spec(d, c):
        return pl.BlockSpec((1, d, T // d, c), lambda b, i: (b, 0, i, 0))

    return pl.pallas_call(
        _combine_kernel,
        name="combine",
        grid=(B, L // T),
        in_specs=[spec(1, W), spec(4, W), spec(16, W), spec(1, LANES), spec(4, LANES), spec(16, LANES)],
        out_specs=pl.BlockSpec((1, T, W), lambda b, i: (b, i, 0)),
        out_shape=jax.ShapeDtypeStruct((B, L, W), BF16),
        scratch_shapes=[pltpu.VMEM((2, W // LANES, T, LANES), F32), pltpu.VMEM((2, T, LANES), F32)],
        compiler_params=_params("parallel", "parallel"),
    )(o1, o4, o16, l1, l4, l16)


def _memkv_kernel(mem_ref, g_ref, w_ref, kv_ref):
    x = mem_ref[0]
    ms = jnp.mean(x * x, axis=-1, keepdims=True)
    xn = (x * lax.rsqrt(ms + EPS) * g_ref[...]).astype(BF16)
    kv_ref[0] = _dot(xn, w_ref[...]).astype(BF16)


def _memkv(mem, gain, w):
    B, N, D = mem.shape
    W2 = w.shape[1]
    return pl.pallas_call(
        _memkv_kernel,
        name="memkv",
        grid=(B,),
        in_specs=[pl.BlockSpec((1, N, D), lambda b: (b, 0, 0)), pl.BlockSpec((1, D), lambda b: (0, 0)),
                  _resident((D, W2))],
        out_specs=pl.BlockSpec((1, N, W2), lambda b: (b, 0, 0)),
        out_shape=jax.ShapeDtypeStruct((B, N, W2), BF16),
        compiler_params=_params("parallel"),
    )(mem, gain.reshape(1, D), w)


def _proj_bd_kernel(h_ref, w_ref, kv_ref, xb_ref, yd_ref):
    W = BRANCH_WIDTH
    z = _dot(h_ref[0], w_ref[...])
    xb_ref[0] = z[:, :W]
    qd = z[:, W:].astype(BF16)
    scale = HEAD_DIM ** -0.5
    for h in range(N_HEADS):
        c0, c1 = h * HEAD_DIM, (h + 1) * HEAD_DIM
        s = _dot_nt(qd[:, c0:c1], kv_ref[0, :, c0:c1])
        m = jnp.max(s, axis=-1, keepdims=True)
        p = jnp.exp2((s - m) * (scale * LOG2_E))
        l = jnp.sum(p, axis=-1, keepdims=True)
        o = _dot(p.astype(BF16), kv_ref[0, :, W + c0:W + c1]) * (1.0 / l)
        yd_ref[0, :, c0:c1] = o.astype(BF16)


def _proj_bd(h, w_bd, kv):
    B, L, D = h.shape
    T, W = PROJ_TILE, BRANCH_WIDTH
    return pl.pallas_call(
        _proj_bd_kernel,
        name="proj_bd",
        grid=(B, L // T),
        in_specs=[pl.BlockSpec((1, T, D), lambda b, i: (b, i, 0)), _resident((D, 2 * W)),
                  pl.BlockSpec((1, N_MEM, 2 * W), lambda b, i: (b, 0, 0))],
        out_specs=[pl.BlockSpec((1, T, W), lambda b, i: (b, i, 0)), pl.BlockSpec((1, T, W), lambda b, i: (b, i, 0))],
        out_shape=[jax.ShapeDtypeStruct((B, L, W), F32), jax.ShapeDtypeStruct((B, L, W), BF16)],
        compiler_params=_params("parallel", "parallel"),
    )(h, w_bd, kv)


def _fft_factors(L):
    n1 = 1 << ((L.bit_length() - 1 + 1) // 2)
    return n1, L // n1


def _dft_cos_sin(n):
    ang = 2.0 * np.pi * ((np.arange(n)[:, None] * np.arange(n)[None, :]) % n) / n
    return np.cos(ang), np.sin(ang)


def _fft1_kernel(x_ref, f_ref, tc_ref, ts_ref, yr_ref, yi_ref, xs_ref, *, n1):
    W = BRANCH_WIDTH
    for j in range(FFT_ROWS):
        xs_ref[:, j * W:(j + 1) * W] = x_ref[0, :, j, :]
    y = _dot(f_ref[...], xs_ref[...].astype(BF16))
    reps = W // FNET_GROUP_WIDTH
    for j in range(FFT_ROWS):
        yr, yi = y[:n1, j * W:(j + 1) * W], y[n1:, j * W:(j + 1) * W]
        tc = jnp.concatenate([tc_ref[j]] * reps, axis=1)
        ts = jnp.concatenate([ts_ref[j]] * reps, axis=1)
        yr_ref[0, j] = yr * tc + yi * ts
        yi_ref[0, j] = yi * tc - yr * ts


def _fft2_kernel(yr_ref, yi_ref, g1_ref, g2_ref, cs_ref, out_ref, ys_ref, *, n2, norm):
    W, gw = BRANCH_WIDTH, FNET_GROUP_WIDTH
    for j in range(FFT_ROWS):
        ys_ref[0, :, j * W:(j + 1) * W] = yr_ref[0, :, j, :]
        ys_ref[1, :, j * W:(j + 1) * W] = yi_ref[0, :, j, :]
    u = _dot(g1_ref[...], ys_ref[0].astype(BF16)) + _dot(g2_ref[...], ys_ref[1].astype(BF16))
    top, bot = u[:n2].astype(BF16), u[n2:].astype(BF16)
    ng = FFT_ROWS * W // gw
    lhs = jnp.concatenate(
        [jnp.concatenate([top[:, j * gw:(j + 1) * gw], bot[:, j * gw:(j + 1) * gw]], axis=1) for j in range(ng)],
        axis=0)
    res = _dot(lhs, cs_ref[...]) * norm
    for j in range(ng):
        k1, c = divmod(j, W // gw)
        out_ref[0, :, k1, c * gw:(c + 1) * gw] = res[j * n2:(j + 1) * n2]


def _fnet(xb, tables):
    B, L, W = xb.shape
    n1, n2 = _fft_factors(L)
    f1, twc, tws, g1, g2, cs = tables
    R = FFT_ROWS
    yr, yi = pl.pallas_call(
        functools.partial(_fft1_kernel, n1=n1),
        name="fft1",
        grid=(B, n2 // R),
        in_specs=[pl.BlockSpec((1, n1, R, W), lambda b, j: (b, 0, j, 0)), _resident((2 * n1, n1)),
                  pl.BlockSpec((R, n1, FNET_GROUP_WIDTH), lambda b, j: (j, 0, 0)),
                  pl.BlockSpec((R, n1, FNET_GROUP_WIDTH), lambda b, j: (j, 0, 0))],
        out_specs=[pl.BlockSpec((1, R, n1, W), lambda b, j: (b, j, 0, 0))] * 2,
        out_shape=[jax.ShapeDtypeStruct((B, n2, n1, W), F32)] * 2,
        scratch_shapes=[pltpu.VMEM((n1, R * W), F32)],
        compiler_params=_params("parallel", "parallel"),
    )(xb.reshape(B, n1, n2, W), f1, twc, tws)
    out = pl.pallas_call(
        functools.partial(_fft2_kernel, n2=n2, norm=float(1.0 / math.sqrt(L * FNET_GROUP_WIDTH))),
        name="fft2",
        grid=(B, n1 // R),
        in_specs=[pl.BlockSpec((1, n2, R, W), lambda b, j: (b, 0, j, 0))] * 2
        + [_resident((2 * n2, n2)), _resident((2 * n2, n2)), _resident((2 * FNET_GROUP_WIDTH, FNET_GROUP_WIDTH))],
        out_specs=pl.BlockSpec((1, n2, R, W), lambda b, j: (b, 0, j, 0)),
        out_shape=jax.ShapeDtypeStruct((B, n2, n1, W), F32),
        scratch_shapes=[pltpu.VMEM((2, n2, R * W), F32)],
        compiler_params=_params("parallel", "parallel"),
    )(yr, yi, g1, g2, cs)
    return out.reshape(B, L, W)


def _fnet_tables(L):
    n1, n2 = _fft_factors(L)
    c1, s1 = _dft_cos_sin(n1)
    c2, s2 = _dft_cos_sin(n2)
    cc, sc = _dft_cos_sin(FNET_GROUP_WIDTH)
    ang = 2.0 * np.pi * (np.arange(n2)[:, None] * np.arange(n1)[None, :]) / L
    bcast = lambda t: jnp.asarray(np.broadcast_to(t[:, :, None], (n2, n1, FNET_GROUP_WIDTH)), F32)
    return (jnp.asarray(np.concatenate([c1, -s1], axis=0), BF16), bcast(np.cos(ang)), bcast(np.sin(ang)),
            jnp.asarray(np.concatenate([c2, -s2], axis=0), BF16), jnp.asarray(np.concatenate([s2, c2], axis=0), BF16),
            jnp.asarray(np.concatenate([cc, sc], axis=0), BF16))


def _proj_c_kernel(h_ref, hp_ref, hn_ref, wqk_ref, wvo_ref, wg_ref, conv_ref, gb_ref,
                   q_ref, k_ref, v_ref, o_ref, g_ref):
    i = pl.program_id(1)
    T, W, halo = PROJ_TILE, BRANCH_WIDTH, CONV_HALO
    h = h_ref[0]
    has_prev = (i > 0).astype(F32)
    has_next = (i < pl.num_programs(1) - 1).astype(F32)
    pad = MLSTM_CONV // 2
    n_ext = T + 2 * halo
    for c in range(2 * W // CONV_COLS):
        cs = slice(c * CONV_COLS, (c + 1) * CONV_COLS)
        w = wqk_ref[:, cs]
        ext = jnp.concatenate([_dot(hp_ref[0], w) * has_prev, _dot(h, w), _dot(hn_ref[0], w) * has_next], axis=0)
        acc = ext[halo:halo + T] * conv_ref[pad:pad + 1, cs]
        for j in range(MLSTM_CONV):
            if j != pad:
                acc = acc + pltpu.roll(ext, (pad - j) % n_ext, 0)[halo:halo + T] * conv_ref[j:j + 1, cs]
        y = acc * _sigmoid(acc)
        if c < W // CONV_COLS:
            q_ref[0, :, cs] = (y * (HEAD_DIM ** -0.5)).astype(BF16)
        else:
            k_ref[0, :, c * CONV_COLS - W:(c + 1) * CONV_COLS - W] = y.astype(BF16)
    vo_t = _dot_nt(wvo_ref[...], h)
    v_ref[0] = vo_t[:W].astype(BF16)
    o_ref[0] = vo_t[W:].astype(BF16)
    g_ref[0] = _dot(h, wg_ref[...]) + gb_ref[...]


def _proj_c(h, w_qk, w_vo_t, w_g, conv, gate_bias):
    B, L, D = h.shape
    T, W, halo = PROJ_TILE, BRANCH_WIDTH, CONV_HALO
    step = T // halo
    nhb = L // halo
    tile = lambda c: pl.BlockSpec((1, T, c), lambda b, i: (b, i, 0))
    tile_t = pl.BlockSpec((1, W, T), lambda b, i: (b, 0, i))
    rows, cols = jax.ShapeDtypeStruct((B, L, W), BF16), jax.ShapeDtypeStruct((B, W, L), BF16)
    return pl.pallas_call(
        _proj_c_kernel,
        name="proj_c",
        grid=(B, L // T),
        in_specs=[tile(D),
                  pl.BlockSpec((1, halo, D), lambda b, i: (b, jnp.maximum(i * step - 1, 0), 0)),
                  pl.BlockSpec((1, halo, D), lambda b, i: (b, jnp.minimum((i + 1) * step, nhb - 1), 0)),
                  _resident((D, 2 * W)), _resident((2 * W, D)), _resident((D, MLSTM_GATES)),
                  _resident((MLSTM_CONV, 2 * W)), _resident((1, MLSTM_GATES))],
        out_specs=[tile(W), tile(W), tile_t, tile_t, tile(MLSTM_GATES)],
        out_shape=[rows, rows, cols, cols, jax.ShapeDtypeStruct((B, L, MLSTM_GATES), F32)],
        compiler_params=_params("parallel", "arbitrary"),
    )(h, h, h, w_qk, w_vo_t, w_g, conv, gate_bias.reshape(1, MLSTM_GATES))


GATE_U, GATE_M, GATE_W_INTER, GATE_FLOOR, GATE_WA, GATE_S_OLD = range(6)
N_GATE_TERMS = 6
STATE_ROWS = HEAD_DIM + 16


def _gate_terms_kernel(x_ref, out_ref, g_ref, pe_ref, mp_ref, *, reverse):
    nc, T, H = x_ref.shape[1], MLSTM_CHUNK, N_HEADS
    x = x_ref[0].reshape(nc * 8, T)
    sub = lax.broadcasted_iota(jnp.int32, (nc * 8, T), 0) % 8
    lane = lax.broadcasted_iota(jnp.int32, (nc * 8, T), 1)
    src = lax.broadcasted_iota(jnp.int32, (T, T), 0)
    tgt = lax.broadcasted_iota(jnp.int32, (T, T), 1)
    tri = ((src >= tgt) if reverse else (src <= tgt)).astype(F32)
    cs = jnp.dot(jnp.where(sub >= H, _log_sigmoid(x), x), tri,
                 preferred_element_type=F32, precision=lax.Precision.HIGHEST)
    b = pltpu.roll(cs, nc * 8 - H, 0)
    u = x - b
    p = u
    shift = 1
    while shift < T:
        if reverse:
            p = jnp.maximum(p, jnp.where(lane < T - shift, pltpu.roll(p, T - shift, 1), NEG))
        else:
            p = jnp.maximum(p, jnp.where(lane >= shift, pltpu.roll(p, shift, 1), NEG))
        shift *= 2
    last = 0 if reverse else T - 1
    g = jnp.broadcast_to(b[:, last:last + 1], (nc * 8, T))
    p_end = jnp.broadcast_to(p[:, last:last + 1], (nc * 8, T))
    g_ref[...] = g[:, :LANES].reshape(nc, 8, LANES)
    pe_ref[...] = p_end[:, :LANES].reshape(nc, 8, LANES)

    def step(c, m):
        cc = nc - 1 - c if reverse else c
        mp_ref[cc] = m
        return g_ref[cc] + jnp.maximum(m, pe_ref[cc])

    lax.fori_loop(0, nc, step, jnp.zeros((8, LANES), F32))
    m = jnp.concatenate([mp_ref[...].reshape(nc * 8, LANES)] * (T // LANES), axis=1)
    big_m = jnp.maximum(m, p)
    m_next = g + jnp.maximum(m, p_end)
    terms = {GATE_U: u, GATE_M: big_m, GATE_W_INTER: jnp.exp(m - big_m), GATE_FLOOR: jnp.exp(-(b + big_m)),
             GATE_WA: jnp.exp(g + u - m_next), GATE_S_OLD: jnp.exp(g + m - m_next)}
    for j, val in terms.items():
        out_ref[0, :, j] = jnp.where(sub < H, val, 0.0).reshape(nc, 8, T)


def _gate_terms(gates_dir, *, reverse):
    B, nc, _, T = gates_dir.shape
    return pl.pallas_call(
        functools.partial(_gate_terms_kernel, reverse=reverse),
        name="gate_terms",
        grid=(B,),
        in_specs=[pl.BlockSpec((1, nc, 8, T), lambda b: (b, 0, 0, 0))],
        out_specs=pl.BlockSpec((1, nc, N_GATE_TERMS, 8, T), lambda b: (b, 0, 0, 0, 0)),
        out_shape=jax.ShapeDtypeStruct((B, nc, N_GATE_TERMS, 8, T), F32),
        scratch_shapes=[pltpu.VMEM((nc, 8, LANES), F32)] * 3,
        compiler_params=_params("parallel"),
    )(gates_dir)


def _mlstm_kernel(*refs, reverse, finalize):
    if finalize:
        (q_ref, k_ref, v_ref, r_ref, hf_ref, oc_ref, gain_ref, out_ref, c_ref) = refs
    else:
        (q_ref, k_ref, v_ref, r_ref, out_ref, c_ref) = refs
    T, dh, H = MLSTM_CHUNK, HEAD_DIM, N_HEADS

    @pl.when(pl.program_id(1) == 0)
    def _():
        c_ref[...] = jnp.zeros_like(c_ref)

    src = lax.broadcasted_iota(jnp.int32, (T, T), 0)
    tgt = lax.broadcasted_iota(jnp.int32, (T, T), 1)
    feeds = (src >= tgt) if reverse else (src <= tgt)
    sub = lax.broadcasted_iota(jnp.int32, (STATE_ROWS - dh, T), 0)
    if finalize:
        eye = (src == tgt).astype(BF16)

    chunks = range(MLSTM_CHUNKS_PER_STEP)
    for ci in (reversed(chunks) if reverse else chunks):
        t0, t1 = ci * T, (ci + 1) * T
        u_cols = jnp.transpose(r_ref[0, ci, GATE_U])
        for h in range(H):
            c0, c1 = h * dh, (h + 1) * dh
            q, k, v_t = q_ref[0, t0:t1, c0:c1], k_ref[0, t0:t1, c0:c1], v_ref[0, c0:c1, t0:t1]
            row = lambda j: r_ref[0, ci, j, h:h + 1, :]
            w_inter, wa = row(GATE_W_INTER), row(GATE_WA)
            state = c_ref[h]

            logw = jnp.where(feeds, u_cols[:, h:h + 1] - row(GATE_M), NEG)
            sqk_t = jnp.exp(logw) * _dot_nt(k, q)
            inter = _dot_nt(state.astype(BF16), q)
            num_t = w_inter * inter[:dh] + _dot(v_t, sqk_t.astype(BF16))
            den = w_inter * inter[dh:dh + 1] + jnp.sum(sqk_t, axis=0, keepdims=True)
            hout_t = num_t * (1.0 / jnp.maximum(jnp.abs(den), row(GATE_FLOOR)))

            vw = jnp.concatenate([(v_t.astype(F32) * wa).astype(BF16),
                                  jnp.where(sub == 0, wa, 0.0).astype(BF16)], axis=0)
            c_ref[h] = row(GATE_S_OLD)[:, :dh] * state + _dot(vw, k)

            if finalize:
                hc = _sigmoid(oc_ref[0, c0:c1, t0:t1].astype(F32)) * (hout_t + hf_ref[0, c0:c1, t0:t1])
                mu = jnp.mean(hc, axis=0, keepdims=True)
                cen = hc - mu
                var = jnp.mean(cen * cen, axis=0, keepdims=True)
                gain = jnp.concatenate([gain_ref[c0:c1]] * (T // LANES), axis=1)
                yc_t = (cen * lax.rsqrt(var + EPS) * gain).astype(BF16)
                out_ref[0, t0:t1, c0:c1] = _dot_nt(eye, yc_t).astype(BF16)
            else:
                out_ref[0, c0:c1, t0:t1] = hout_t


def _mlstm_pass(q, k, v_t, terms, extra, *, reverse):
    B, L, W = k.shape
    cps = MLSTM_CHUNKS_PER_STEP
    T = MLSTM_CHUNK * cps
    n = L // T
    pos = (lambda i: n - 1 - i) if reverse else (lambda i: i)
    rows = pl.BlockSpec((1, T, W), lambda b, i: (b, pos(i), 0))
    cols = pl.BlockSpec((1, W, T), lambda b, i: (b, 0, pos(i)))
    in_specs = [rows, rows, cols,
                pl.BlockSpec((1, cps, N_GATE_TERMS, 8, MLSTM_CHUNK), lambda b, i: (b, pos(i), 0, 0, 0))]
    finalize = extra is not None
    args = [q, k, v_t, terms]
    if finalize:
        in_specs += [cols, cols, _resident((W, LANES))]
        args += list(extra)
    return pl.pallas_call(
        functools.partial(_mlstm_kernel, reverse=reverse, finalize=finalize),
        name="mlstm_bwd" if reverse else "mlstm_fwd",
        grid=(B, n),
        in_specs=in_specs,
        out_specs=rows if finalize else cols,
        out_shape=jax.ShapeDtypeStruct((B, L, W), BF16) if finalize else jax.ShapeDtypeStruct((B, W, L), F32),
        scratch_shapes=[pltpu.VMEM((N_HEADS, STATE_ROWS, HEAD_DIM), F32)],
        compiler_params=_params("parallel", "arbitrary"),
    )(*args)


def _merge_kernel(x_ref, h_ref, ya_ref, yb_ref, yc_ref, yd_ref, wgp_ref, wmg_ref, wbr_ref, wout_ref, g_ref, out_ref):
    W, D = BRANCH_WIDTH, D_MODEL
    h = h_ref[...]
    acc = None
    for g, br_ref in enumerate((ya_ref, yb_ref, yc_ref, yd_ref)):
        gp = _dot(h, wgp_ref[:, g * W:(g + 1) * W])
        gated = (br_ref[...].astype(F32) * (gp * _sigmoid(gp))).astype(BF16)
        proj = _dot(gated, wbr_ref[g])
        term = _sigmoid(_dot(h, wmg_ref[:, g * D:(g + 1) * D])) * proj
        acc = term if acc is None else acc + term
    out = _dot(acc.astype(BF16), wout_ref[...])
    ms = jnp.mean(out * out, axis=-1, keepdims=True)
    out_ref[...] = x_ref[...] + out * lax.rsqrt(ms + EPS) * g_ref[...]


def _merge(x_rows, h_rows, branches, w_gp, w_mg, w_br, w_out, gain):
    R, D = x_rows.shape
    T, W = ROW_TILE, BRANCH_WIDTH
    rows = lambda c: pl.BlockSpec((T, c), lambda i: (i, 0))
    return pl.pallas_call(
        _merge_kernel,
        name="merge",
        grid=(R // T,),
        in_specs=[rows(D), rows(D)] + [rows(W)] * N_BRANCHES
        + [_resident(w_gp.shape), _resident(w_mg.shape), _resident(w_br.shape), _resident(w_out.shape),
           _resident((1, D))],
        out_specs=rows(D),
        out_shape=jax.ShapeDtypeStruct((R, D), F32),
        compiler_params=_params("parallel"),
    )(x_rows, h_rows, *branches, w_gp, w_mg, w_br, w_out, gain.reshape(1, D))


def _t5_bucket(rel):
    nb = REL_BUCKETS // 2
    max_exact = nb // 2
    ret = (rel > 0).astype(np.int32) * nb
    n = np.abs(rel)
    large = max_exact + (np.log(np.maximum(n, 1) / max_exact) / np.log(REL_MAX_DIST / max_exact)
                         * (nb - max_exact)).astype(np.int32)
    large = np.minimum(large, nb - 1)
    return (ret + np.where(n < max_exact, n, large)).astype(np.int32)


def _band_bias(rel_bias, group, dilation):
    qb, hw = ATTN_Q_BLOCK, HALF_WINDOW
    rel = np.arange(qb + 2 * hw)[None, :] - hw - np.arange(qb)[:, None]
    heads = rel_bias[:, group * N_HEADS:(group + 1) * N_HEADS].astype(F32)
    onehot = (jnp.asarray(_t5_bucket(dilation * rel))[None, :, :] == jnp.arange(REL_BUCKETS)[:, None, None])
    bias = jnp.sum(jnp.where(onehot[:, None], heads[:, :, None, None], 0.0), axis=0)
    return jnp.where(jnp.asarray(np.abs(rel) <= hw)[None], bias * (HEAD_DIM ** 0.5), NEG)


def _split_weights(w_in, conv_qk, w_mem_kv, w_branch, w_out):
    W, D, A = BRANCH_WIDTH, D_MODEL, A_QKV_WIDTH
    off = np.cumsum([0, A, A, A, W, W, W, W, W, MLSTM_GATES, W, N_BRANCHES * W, N_BRANCHES * D])
    seg = lambda j: w_in[:, off[j]:off[j + 1]]
    qa, ka, va = seg(0), seg(1), seg(2)
    w_a = [jnp.concatenate([t[:, g * W:(g + 1) * W] for t in (qa, ka, va)], axis=1).astype(BF16)
           for g in range(len(DILATIONS))]
    return dict(
        a=w_a,
        bd=jnp.concatenate([seg(3), seg(9)], axis=1).astype(BF16),
        qk=jnp.concatenate([seg(4), seg(5)], axis=1).astype(BF16),
        vo_t=jnp.concatenate([seg(6), seg(7)], axis=1).T.astype(BF16),
        g=seg(8).astype(BF16),
        gp=seg(10).astype(BF16),
        mg=seg(11).astype(BF16),
        conv=conv_qk.astype(F32),
        memkv=w_mem_kv.astype(BF16),
        br=w_branch.astype(BF16),
        out=w_out.astype(BF16),
    )


def _layer(x, mem, biases, fft_tables, wts, norm_pre, gate_bias, head_gain, mem_norm, norm_post):
    B, L, D = x.shape
    W = BRANCH_WIDTH
    h, h4, h16 = _prenorm(x, norm_pre)

    outs, lses = [], []
    for g, (d, hp) in enumerate(zip(DILATIONS, (h, h4, h16))):
        q, kv = _proj_a(hp.reshape(B * L, D), wts["a"][g])
        o, lse = _attention(q.reshape(B * d, L // d, W), kv.reshape(B * d, L // d, 2 * W), biases[g])
        outs.append(o.reshape(B, d, L // d, W))
        lses.append(lse.reshape(B, d, L // d, LANES))
    ya = _combine(*outs, *lses)

    kv_mem = _memkv(mem, mem_norm, wts["memkv"])
    xb, yd = _proj_bd(h, wts["bd"], kv_mem)
    yb = _fnet(xb, fft_tables)

    qm, km, v_t, oc_t, gates = _proj_c(h, wts["qk"], wts["vo_t"], wts["g"], wts["conv"], gate_bias.reshape(-1))
    T = MLSTM_CHUNK
    gates_t = jnp.swapaxes(gates.reshape(B, L // T, T, MLSTM_GATES), 2, 3)
    terms_fwd = _gate_terms(gates_t[:, :, :8], reverse=False)
    terms_bwd = _gate_terms(gates_t[:, :, 8:], reverse=True)
    gain = jnp.broadcast_to(head_gain.astype(F32)[:, None], (W, LANES))
    h_fwd_t = _mlstm_pass(qm, km, v_t, terms_fwd, None, reverse=False)
    yc = _mlstm_pass(qm, km, v_t, terms_bwd, (h_fwd_t, oc_t, gain), reverse=True)

    rows = lambda t: t.reshape(B * L, t.shape[-1])
    y = _merge(rows(x), rows(h), [rows(ya), rows(yb), rows(yc), rows(yd)],
               wts["gp"], wts["mg"], wts["br"], wts["out"], norm_post)
    return y.reshape(B, L, D)


def kernel(x_prompt, x_sample, mem_prompt, mem_sample, rel_bias, norm_pre, w_in, conv_qk, mlstm_gate_bias,
           mlstm_head_gain, mem_norm, w_mem_kv, w_branch, w_out, norm_post):
    depth = w_in.shape[0]
    biases = [_band_bias(rel_bias, g, d) for g, d in enumerate(DILATIONS)]
    layer_wts = [_split_weights(w_in[l], conv_qk[l], w_mem_kv[l], w_branch[l], w_out[l]) for l in range(depth)]
    outs = []
    for x, mem in ((x_prompt, mem_prompt), (x_sample, mem_sample)):
        tables = _fnet_tables(x.shape[1])
        for l in range(depth):
            x = _layer(x, mem, biases, tables, layer_wts[l], norm_pre[l], mlstm_gate_bias[l],
                       mlstm_head_gain[l], mem_norm[l], norm_post[l])
        outs.append(x)
    return tuple(outs)
```

```python
import functools
import math

import numpy as np
import jax
import jax.numpy as jnp
from jax import lax
from jax.experimental import pallas as pl
from jax.experimental.pallas import tpu as pltpu

D_MODEL = 1024
BRANCH_WIDTH = 512
HEAD_DIM = 128
N_HEADS = 4
N_BRANCHES = 4
DILATIONS = (1, 4, 16)
HALF_WINDOW = 64
A_QKV_WIDTH = len(DILATIONS) * BRANCH_WIDTH
REL_BUCKETS = 32
REL_MAX_DIST = 1024
FNET_GROUP_WIDTH = 128
MLSTM_CONV = 5
MLSTM_GATES = 16
N_MEM = 256
EPS = 1e-6
NEG = -1e30
LOG2_E = math.log2(math.e)

F32 = jnp.float32
BF16 = jnp.bfloat16
LANES = 128
LSE_LANES_PER_HEAD = LANES // N_HEADS

VMEM_LIMIT_BYTES = 56 * 1024 * 1024
PERM_TILE = 1024
ROW_TILE = 512
PROJ_TILE = 2048
ATTN_TILE = 2048
ATTN_Q_BLOCK = 128
MLSTM_CHUNK = 256
MLSTM_CHUNKS_PER_STEP = 4
FFT_ROWS = 8
CONV_HALO = 16
CONV_COLS = 256


def _params(*sem):
    return pltpu.CompilerParams(dimension_semantics=sem, vmem_limit_bytes=VMEM_LIMIT_BYTES)


def _resident(shape):
    nd = len(shape)
    return pl.BlockSpec(shape, lambda *_: (0,) * nd, pipeline_mode=pl.Buffered(1))


def _sigmoid(x):
    return 0.5 * jnp.tanh(0.5 * x) + 0.5


def _log_sigmoid(x):
    return jnp.minimum(x, 0.0) - jnp.log(1.0 + jnp.exp(-jnp.abs(x)))


def _dot(a, b):
    return jnp.dot(a, b, preferred_element_type=F32)


def _dot_nt(a, b):
    return lax.dot_general(a, b, (((1,), (1,)), ((), ())), preferred_element_type=F32)


def _dot_tn(a, b):
    return lax.dot_general(a, b, (((0,), (0,)), ((), ())), preferred_element_type=F32)


def _prenorm_kernel(x_ref, g_ref, h_ref, h4_ref, h16_ref, xs_ref):
    T = PERM_TILE
    x = x_ref[0]
    ms = jnp.mean(x * x, axis=-1, keepdims=True)
    xn = x * lax.rsqrt(ms + EPS) * g_ref[...]
    h_ref[0] = xn.astype(BF16)
    for c in range(D_MODEL // LANES):
        xs_ref[c] = xn[:, c * LANES:(c + 1) * LANES]
    for d, out_ref in ((4, h4_ref), (16, h16_ref)):
        for r in range(d):
            for c in range(D_MODEL // LANES):
                out_ref[0, r, :, c * LANES:(c + 1) * LANES] = xs_ref[c, pl.ds(r, T // d, stride=d), :].astype(BF16)


def _prenorm(x, gain):
    B, L, D = x.shape
    T = PERM_TILE
    return pl.pallas_call(
        _prenorm_kernel,
        name="prenorm",
        grid=(B, L // T),
        in_specs=[pl.BlockSpec((1, T, D), lambda b, i: (b, i, 0)),
                  pl.BlockSpec((1, D), lambda b, i: (0, 0))],
        out_specs=[pl.BlockSpec((1, T, D), lambda b, i: (b, i, 0)),
                   pl.BlockSpec((1, 4, T // 4, D), lambda b, i: (b, 0, i, 0)),
                   pl.BlockSpec((1, 16, T // 16, D), lambda b, i: (b, 0, i, 0))],
        out_shape=[jax.ShapeDtypeStruct((B, L, D), BF16),
                   jax.ShapeDtypeStruct((B, 4, L // 4, D), BF16),
                   jax.ShapeDtypeStruct((B, 16, L // 16, D), BF16)],
        scratch_shapes=[pltpu.VMEM((D // LANES, T, LANES), F32)],
        compiler_params=_params("parallel", "parallel"),
    )(x, gain.reshape(1, D))


def _proj_a_kernel(h_ref, w_ref, q_ref, kv_ref):
    z = _dot(h_ref[...], w_ref[...])
    q_ref[...] = z[:, :BRANCH_WIDTH].astype(BF16)
    kv_ref[...] = z[:, BRANCH_WIDTH:].astype(BF16)


def _proj_a(h_rows, w_qkv):
    R, D = h_rows.shape
    T = PROJ_TILE
    W = BRANCH_WIDTH
    return pl.pallas_call(
        _proj_a_kernel,
        name="proj_a",
        grid=(R // T,),
        in_specs=[pl.BlockSpec((T, D), lambda i: (i, 0)), _resident((D, 3 * W))],
        out_specs=[pl.BlockSpec((T, W), lambda i: (i, 0)), pl.BlockSpec((T, 2 * W), lambda i: (i, 0))],
        out_shape=[jax.ShapeDtypeStruct((R, W), BF16), jax.ShapeDtypeStruct((R, 2 * W), BF16)],
        compiler_params=_params("parallel"),
    )(h_rows, w_qkv)


def _attn_kernel(q_ref, kv_ref, kvp_ref, kvn_ref, bias_ref, o_ref, lse_ref, win_ref, *, seq_len, tq):
    i = pl.program_id(1)
    hw, qb, W = HALF_WINDOW, ATTN_Q_BLOCK, BRANCH_WIDTH
    kb = qb + 2 * hw
    win_ref[0:hw] = kvp_ref[0]
    win_ref[hw:hw + tq] = kv_ref[0]
    win_ref[hw + tq:] = kvn_ref[0]
    scale = HEAD_DIM ** -0.5
    lane_head = lax.broadcasted_iota(jnp.int32, (qb, LANES), 1) // LSE_LANES_PER_HEAD
    n_blocks = tq // qb
    for j in range(n_blocks):
        a = j * qb
        kidx = i * tq + (a - hw) + lax.broadcasted_iota(jnp.int32, (1, kb), 1)
        valid = (kidx >= 0) & (kidx < seq_len) if j in (0, n_blocks - 1) else None
        lse_all = None
        for h in range(N_HEADS):
            c0, c1 = h * HEAD_DIM, (h + 1) * HEAD_DIM
            q = q_ref[0, a:a + qb, c0:c1]
            k = win_ref[a:a + kb, c0:c1]
            v = win_ref[a:a + kb, W + c0:W + c1]
            s = _dot_nt(q, k) + bias_ref[h]
            if valid is not None:
                s = jnp.where(valid, s, NEG)
            m = jnp.max(s, axis=-1, keepdims=True)
            p = jnp.exp2((s - m) * (scale * LOG2_E))
            l = jnp.sum(p, axis=-1, keepdims=True)
            o = _dot(p.astype(BF16), v) * (1.0 / l)
            o_ref[0, a:a + qb, c0:c1] = o.astype(BF16)
            lse = jnp.broadcast_to(m * scale + jnp.log(l), (qb, LANES))
            lse_all = lse if h == 0 else jnp.where(lane_head == h, lse, lse_all)
        lse_ref[0, a:a + qb, :] = lse_all


def _attention(q, kv, bias):
    S, M, W = q.shape
    hw = HALF_WINDOW
    tq = min(M, ATTN_TILE)
    nhb = M // hw
    step = tq // hw
    kern = functools.partial(_attn_kernel, seq_len=M, tq=tq)
    return pl.pallas_call(
        kern,
        name="attn",
        grid=(S, M // tq),
        in_specs=[pl.BlockSpec((1, tq, W), lambda s, i: (s, i, 0)),
                  pl.BlockSpec((1, tq, 2 * W), lambda s, i: (s, i, 0)),
                  pl.BlockSpec((1, hw, 2 * W), lambda s, i: (s, jnp.maximum(i * step - 1, 0), 0)),
                  pl.BlockSpec((1, hw, 2 * W), lambda s, i: (s, jnp.minimum((i + 1) * step, nhb - 1), 0)),
                  _resident(bias.shape)],
        out_specs=[pl.BlockSpec((1, tq, W), lambda s, i: (s, i, 0)),
                   pl.BlockSpec((1, tq, LANES), lambda s, i: (s, i, 0))],
        out_shape=[jax.ShapeDtypeStruct((S, M, W), BF16), jax.ShapeDtypeStruct((S, M, LANES), F32)],
        scratch_shapes=[pltpu.VMEM((tq + 2 * hw, 2 * W), BF16)],
        compiler_params=_params("parallel", "parallel"),
    )(q, kv, kv, kv, bias)


def _combine_kernel(o1_ref, o4_ref, o16_ref, l1_ref, l4_ref, l16_ref, ya_ref, os_ref, ls_ref):
    T, nc = PERM_TILE, BRANCH_WIDTH // LANES
    for g, (d, o_ref, l_ref) in enumerate(((4, o4_ref, l4_ref), (16, o16_ref, l16_ref))):
        for r in range(d):
            ls_ref[g, pl.ds(r, T // d, stride=d), :] = l_ref[0, r]
            for c in range(nc):
                os_ref[g, c, pl.ds(r, T // d, stride=d), :] = o_ref[0, r, :, c * LANES:(c + 1) * LANES].astype(F32)
    l1, l4, l16 = l1_ref[0, 0], ls_ref[0], ls_ref[1]
    m = jnp.maximum(jnp.maximum(l1, l4), l16)
    e1, e4, e16 = jnp.exp(l1 - m), jnp.exp(l4 - m), jnp.exp(l16 - m)
    inv = 1.0 / (e1 + e4 + e16)
    w1, w4, w16 = e1 * inv, e4 * inv, e16 * inv
    for h in range(N_HEADS):
        c0, c1 = h * HEAD_DIM, (h + 1) * HEAD_DIM
        lh = h * LSE_LANES_PER_HEAD
        ya = (w1[:, lh:lh + 1] * o1_ref[0, 0, :, c0:c1].astype(F32)
              + w4[:, lh:lh + 1] * os_ref[0, h] + w16[:, lh:lh + 1] * os_ref[1, h])
        ya_ref[0, :, c0:c1] = ya.astype(BF16)


def _combine(o1, o4, o16, l1, l4, l16):
    B, _, L, W = o1.shape
    T = PERM_TILE

    def spec(d, c):
        return pl.BlockSpec((1, d, T // d, c), lambda b, i: (b, 0, i, 0))

    return pl.pallas_call(
        _combine_kernel,
        name="combine",
        grid=(B, L // T),
        in_specs=[spec(1, W), spec(4, W), spec(16, W), spec(1, LANES), spec(4, LANES), spec(16, LANES)],
        out_specs=pl.BlockSpec((1, T, W), lambda b, i: (b, i, 0)),
        out_shape=jax.ShapeDtypeStruct((B, L, W), BF16),
        scratch_shapes=[pltpu.VMEM((2, W // LANES, T, LANES), F32), pltpu.VMEM((2, T, LANES), F32)],
        compiler_params=_params("parallel", "parallel"),
    )(o1, o4, o16, l1, l4, l16)


def _memkv_kernel(mem_ref, g_ref, w_ref, kv_ref):
    x = mem_ref[0]
    ms = jnp.mean(x * x, axis=-1, keepdims=True)
    xn = (x * lax.rsqrt(ms + EPS) * g_ref[...]).astype(BF16)
    kv_ref[0] = _dot(xn, w_ref[...]).astype(BF16)


def _memkv(mem, gain, w):
    B, N, D = mem.shape
    W2 = w.shape[1]
    return pl.pallas_call(
        _memkv_kernel,
        name="memkv",
        grid=(B,),
        in_specs=[pl.BlockSpec((1, N, D), lambda b: (b, 0, 0)), pl.BlockSpec((1, D), lambda b: (0, 0)),
                  _resident((D, W2))],
        out_specs=pl.BlockSpec((1, N, W2), lambda b: (b, 0, 0)),
        out_shape=jax.ShapeDtypeStruct((B, N, W2), BF16),
        compiler_params=_params("parallel"),
    )(mem, gain.reshape(1, D), w)


def _proj_bd_kernel(h_ref, w_ref, kv_ref, xb_ref, yd_ref):
    W = BRANCH_WIDTH
    z = _dot(h_ref[0], w_ref[...])
    xb_ref[0] = z[:, :W]
    qd = z[:, W:].astype(BF16)
    scale = HEAD_DIM ** -0.5
    for h in range(N_HEADS):
        c0, c1 = h * HEAD_DIM, (h + 1) * HEAD_DIM
        s = _dot_nt(qd[:, c0:c1], kv_ref[0, :, c0:c1])
        m = jnp.max(s, axis=-1, keepdims=True)
        p = jnp.exp2((s - m) * (scale * LOG2_E))
        l = jnp.sum(p, axis=-1, keepdims=True)
        o = _dot(p.astype(BF16), kv_ref[0, :, W + c0:W + c1]) * (1.0 / l)
        yd_ref[0, :, c0:c1] = o.astype(BF16)


def _proj_bd(h, w_bd, kv):
    B, L, D = h.shape
    T, W = PROJ_TILE, BRANCH_WIDTH
    return pl.pallas_call(
        _proj_bd_kernel,
        name="proj_bd",
        grid=(B, L // T),
        in_specs=[pl.BlockSpec((1, T, D), lambda b, i: (b, i, 0)), _resident((D, 2 * W)),
                  pl.BlockSpec((1, N_MEM, 2 * W), lambda b, i: (b, 0, 0))],
        out_specs=[pl.BlockSpec((1, T, W), lambda b, i: (b, i, 0)), pl.BlockSpec((1, T, W), lambda b, i: (b, i, 0))],
        out_shape=[jax.ShapeDtypeStruct((B, L, W), F32), jax.ShapeDtypeStruct((B, L, W), BF16)],
        compiler_params=_params("parallel", "parallel"),
    )(h, w_bd, kv)


def _fft_factors(L):
    n1 = 1 << ((L.bit_length() - 1 + 1) // 2)
    return n1, L // n1


def _dft_cos_sin(n):
    ang = 2.0 * np.pi * ((np.arange(n)[:, None] * np.arange(n)[None, :]) % n) / n
    return np.cos(ang), np.sin(ang)


def _fft1_kernel(x_ref, f_ref, tc_ref, ts_ref, yr_ref, yi_ref, xs_ref, *, n1):
    W = BRANCH_WIDTH
    for j in range(FFT_ROWS):
        xs_ref[:, j * W:(j + 1) * W] = x_ref[0, :, j, :]
    y = _dot(f_ref[...], xs_ref[...].astype(BF16))
    reps = W // FNET_GROUP_WIDTH
    for j in range(FFT_ROWS):
        yr, yi = y[:n1, j * W:(j + 1) * W], y[n1:, j * W:(j + 1) * W]
        tc = jnp.concatenate([tc_ref[j]] * reps, axis=1)
        ts = jnp.concatenate([ts_ref[j]] * reps, axis=1)
        yr_ref[0, j] = yr * tc + yi * ts
        yi_ref[0, j] = yi * tc - yr * ts


def _fft2_kernel(yr_ref, yi_ref, g1_ref, g2_ref, cs_ref, out_ref, ys_ref, *, n2, norm):
    W, gw = BRANCH_WIDTH, FNET_GROUP_WIDTH
    for j in range(FFT_ROWS):
        ys_ref[0, :, j * W:(j + 1) * W] = yr_ref[0, :, j, :]
        ys_ref[1, :, j * W:(j + 1) * W] = yi_ref[0, :, j, :]
    u = _dot(g1_ref[...], ys_ref[0].astype(BF16)) + _dot(g2_ref[...], ys_ref[1].astype(BF16))
    top, bot = u[:n2].astype(BF16), u[n2:].astype(BF16)
    ng = FFT_ROWS * W // gw
    lhs = jnp.concatenate(
        [jnp.concatenate([top[:, j * gw:(j + 1) * gw], bot[:, j * gw:(j + 1) * gw]], axis=1) for j in range(ng)],
        axis=0)
    res = _dot(lhs, cs_ref[...]) * norm
    for j in range(ng):
        k1, c = divmod(j, W // gw)
        out_ref[0, :, k1, c * gw:(c + 1) * gw] = res[j * n2:(j + 1) * n2]


def _fnet(xb, tables):
    B, L, W = xb.shape
    n1, n2 = _fft_factors(L)
    f1, twc, tws, g1, g2, cs = tables
    R = FFT_ROWS
    yr, yi = pl.pallas_call(
        functools.partial(_fft1_kernel, n1=n1),
        name="fft1",
        grid=(B, n2 // R),
        in_specs=[pl.BlockSpec((1, n1, R, W), lambda b, j: (b, 0, j, 0)), _resident((2 * n1, n1)),
                  pl.BlockSpec((R, n1, FNET_GROUP_WIDTH), lambda b, j: (j, 0, 0)),
                  pl.BlockSpec((R, n1, FNET_GROUP_WIDTH), lambda b, j: (j, 0, 0))],
        out_specs=[pl.BlockSpec((1, R, n1, W), lambda b, j: (b, j, 0, 0))] * 2,
        out_shape=[jax.ShapeDtypeStruct((B, n2, n1, W), F32)] * 2,
        scratch_shapes=[pltpu.VMEM((n1, R * W), F32)],
        compiler_params=_params("parallel", "parallel"),
    )(xb.reshape(B, n1, n2, W), f1, twc, tws)
    out = pl.pallas_call(
        functools.partial(_fft2_kernel, n2=n2, norm=float(1.0 / math.sqrt(L * FNET_GROUP_WIDTH))),
        name="fft2",
        grid=(B, n1 // R),
        in_specs=[pl.BlockSpec((1, n2, R, W), lambda b, j: (b, 0, j, 0))] * 2
        + [_resident((2 * n2, n2)), _resident((2 * n2, n2)), _resident((2 * FNET_GROUP_WIDTH, FNET_GROUP_WIDTH))],
        out_specs=pl.BlockSpec((1, n2, R, W), lambda b, j: (b, 0, j, 0)),
        out_shape=jax.ShapeDtypeStruct((B, n2, n1, W), F32),
        scratch_shapes=[pltpu.VMEM((2, n2, R * W), F32)],
        compiler_params=_params("parallel", "parallel"),
    )(yr, yi, g1, g2, cs)
    return out.reshape(B, L, W)


def _fnet_tables(L):
    n1, n2 = _fft_factors(L)
    c1, s1 = _dft_cos_sin(n1)
    c2, s2 = _dft_cos_sin(n2)
    cc, sc = _dft_cos_sin(FNET_GROUP_WIDTH)
    ang = 2.0 * np.pi * (np.arange(n2)[:, None] * np.arange(n1)[None, :]) / L
    bcast = lambda t: jnp.asarray(np.broadcast_to(t[:, :, None], (n2, n1, FNET_GROUP_WIDTH)), F32)
    return (jnp.asarray(np.concatenate([c1, -s1], axis=0), BF16), bcast(np.cos(ang)), bcast(np.sin(ang)),
            jnp.asarray(np.concatenate([c2, -s2], axis=0), BF16), jnp.asarray(np.concatenate([s2, c2], axis=0), BF16),
            jnp.asarray(np.concatenate([cc, sc], axis=0), BF16))


def _proj_c_kernel(h_ref, hp_ref, hn_ref, wqk_ref, wvo_ref, wg_ref, conv_ref, gb_ref,
                   q_ref, k_ref, v_ref, o_ref, g_ref):
    i = pl.program_id(1)
    T, W, halo = PROJ_TILE, BRANCH_WIDTH, CONV_HALO
    h = h_ref[0]
    has_prev = (i > 0).astype(F32)
    has_next = (i < pl.num_programs(1) - 1).astype(F32)
    pad = MLSTM_CONV // 2
    n_ext = T + 2 * halo
    for c in range(2 * W // CONV_COLS):
        cs = slice(c * CONV_COLS, (c + 1) * CONV_COLS)
        w = wqk_ref[:, cs]
        ext = jnp.concatenate([_dot(hp_ref[0], w) * has_prev, _dot(h, w), _dot(hn_ref[0], w) * has_next], axis=0)
        acc = ext[halo:halo + T] * conv_ref[pad:pad + 1, cs]
        for j in range(MLSTM_CONV):
            if j != pad:
                acc = acc + pltpu.roll(ext, (pad - j) % n_ext, 0)[halo:halo + T] * conv_ref[j:j + 1, cs]
        y = acc * _sigmoid(acc)
        if c < W // CONV_COLS:
            q_ref[0, :, cs] = (y * (HEAD_DIM ** -0.5)).astype(BF16)
        else:
            k_ref[0, :, c * CONV_COLS - W:(c + 1) * CONV_COLS - W] = y.astype(BF16)
    vo_t = _dot_nt(wvo_ref[...], h)
    v_ref[0] = vo_t[:W].astype(BF16)
    o_ref[0] = vo_t[W:].astype(BF16)
    g_ref[0] = _dot(h, wg_ref[...]) + gb_ref[...]


def _proj_c(h, w_qk, w_vo_t, w_g, conv, gate_bias):
    B, L, D = h.shape
    T, W, halo = PROJ_TILE, BRANCH_WIDTH, CONV_HALO
    step = T // halo
    nhb = L // halo
    tile = lambda c: pl.BlockSpec((1, T, c), lambda b, i: (b, i, 0))
    tile_t = pl.BlockSpec((1, W, T), lambda b, i: (b, 0, i))
    rows, cols = jax.ShapeDtypeStruct((B, L, W), BF16), jax.ShapeDtypeStruct((B, W, L), BF16)
    return pl.pallas_call(
        _proj_c_kernel,
        name="proj_c",
        grid=(B, L // T),
        in_specs=[tile(D),
                  pl.BlockSpec((1, halo, D), lambda b, i: (b, jnp.maximum(i * step - 1, 0), 0)),
                  pl.BlockSpec((1, halo, D), lambda b, i: (b, jnp.minimum((i + 1) * step, nhb - 1), 0)),
                  _resident((D, 2 * W)), _resident((2 * W, D)), _resident((D, MLSTM_GATES)),
                  _resident((MLSTM_CONV, 2 * W)), _resident((1, MLSTM_GATES))],
        out_specs=[tile(W), tile(W), tile_t, tile_t, tile(MLSTM_GATES)],
        out_shape=[rows, rows, cols, cols, jax.ShapeDtypeStruct((B, L, MLSTM_GATES), F32)],
        compiler_params=_params("parallel", "arbitrary"),
    )(h, h, h, w_qk, w_vo_t, w_g, conv, gate_bias.reshape(1, MLSTM_GATES))


GATE_U, GATE_M, GATE_W_INTER, GATE_FLOOR, GATE_WA, GATE_S_OLD = range(6)
N_GATE_TERMS = 6
STATE_ROWS = HEAD_DIM + 16


def _gate_terms_kernel(x_ref, out_ref, g_ref, pe_ref, mp_ref, *, reverse):
    nc, T, H = x_ref.shape[1], MLSTM_CHUNK, N_HEADS
    x = x_ref[0].reshape(nc * 8, T)
    sub = lax.broadcasted_iota(jnp.int32, (nc * 8, T), 0) % 8
    lane = lax.broadcasted_iota(jnp.int32, (nc * 8, T), 1)
    src = lax.broadcasted_iota(jnp.int32, (T, T), 0)
    tgt = lax.broadcasted_iota(jnp.int32, (T, T), 1)
    tri = ((src >= tgt) if reverse else (src <= tgt)).astype(F32)
    cs = jnp.dot(jnp.where(sub >= H, _log_sigmoid(x), x), tri,
                 preferred_element_type=F32, precision=lax.Precision.HIGHEST)
    b = pltpu.roll(cs, nc * 8 - H, 0)
    u = x - b
    p = u
    shift = 1
    while shift < T:
        if reverse:
            p = jnp.maximum(p, jnp.where(lane < T - shift, pltpu.roll(p, T - shift, 1), NEG))
        else:
            p = jnp.maximum(p, jnp.where(lane >= shift, pltpu.roll(p, shift, 1), NEG))
        shift *= 2
    last = 0 if reverse else T - 1
    g = jnp.broadcast_to(b[:, last:last + 1], (nc * 8, T))
    p_end = jnp.broadcast_to(p[:, last:last + 1], (nc * 8, T))
    g_ref[...] = g[:, :LANES].reshape(nc, 8, LANES)
    pe_ref[...] = p_end[:, :LANES].reshape(nc, 8, LANES)

    def step(c, m):
        cc = nc - 1 - c if reverse else c
        mp_ref[cc] = m
        return g_ref[cc] + jnp.maximum(m, pe_ref[cc])

    lax.fori_loop(0, nc, step, jnp.zeros((8, LANES), F32))
    m = jnp.concatenate([mp_ref[...].reshape(nc * 8, LANES)] * (T // LANES), axis=1)
    big_m = jnp.maximum(m, p)
    m_next = g + jnp.maximum(m, p_end)
    terms = {GATE_U: u, GATE_M: big_m, GATE_W_INTER: jnp.exp(m - big_m), GATE_FLOOR: jnp.exp(-(b + big_m)),
             GATE_WA: jnp.exp(g + u - m_next), GATE_S_OLD: jnp.exp(g + m - m_next)}
    for j, val in terms.items():
        out_ref[0, :, j] = jnp.where(sub < H, val, 0.0).reshape(nc, 8, T)


def _gate_terms(gates_dir, *, reverse):
    B, nc, _, T = gates_dir.shape
    return pl.pallas_call(
        functools.partial(_gate_terms_kernel, reverse=reverse),
        name="gate_terms",
        grid=(B,),
        in_specs=[pl.BlockSpec((1, nc, 8, T), lambda b: (b, 0, 0, 0))],
        out_specs=pl.BlockSpec((1, nc, N_GATE_TERMS, 8, T), lambda b: (b, 0, 0, 0, 0)),
        out_shape=jax.ShapeDtypeStruct((B, nc, N_GATE_TERMS, 8, T), F32),
        scratch_shapes=[pltpu.VMEM((nc, 8, LANES), F32)] * 3,
        compiler_params=_params("parallel"),
    )(gates_dir)


def _mlstm_kernel(*refs, reverse, finalize):
    if finalize:
        (q_ref, k_ref, v_ref, r_ref, hf_ref, oc_ref, gain_ref, out_ref, c_ref) = refs
    else:
        (q_ref, k_ref, v_ref, r_ref, out_ref, c_ref) = refs
    T, dh, H = MLSTM_CHUNK, HEAD_DIM, N_HEADS

    @pl.when(pl.program_id(1) == 0)
    def _():
        c_ref[...] = jnp.zeros_like(c_ref)

    src = lax.broadcasted_iota(jnp.int32, (T, T), 0)
    tgt = lax.broadcasted_iota(jnp.int32, (T, T), 1)
    feeds = (src >= tgt) if reverse else (src <= tgt)
    sub = lax.broadcasted_iota(jnp.int32, (STATE_ROWS - dh, T), 0)
    if finalize:
        eye = (src == tgt).astype(BF16)

    chunks = range(MLSTM_CHUNKS_PER_STEP)
    for ci in (reversed(chunks) if reverse else chunks):
        t0, t1 = ci * T, (ci + 1) * T
        u_cols = jnp.transpose(r_ref[0, ci, GATE_U])
        for h in range(H):
            c0, c1 = h * dh, (h + 1) * dh
            q, k, v_t = q_ref[0, t0:t1, c0:c1], k_ref[0, t0:t1, c0:c1], v_ref[0, c0:c1, t0:t1]
            row = lambda j: r_ref[0, ci, j, h:h + 1, :]
            w_inter, wa = row(GATE_W_INTER), row(GATE_WA)
            state = c_ref[h]

            logw = jnp.where(feeds, u_cols[:, h:h + 1] - row(GATE_M), NEG)
            sqk_t = jnp.exp(logw) * _dot_nt(k, q)
            inter = _dot_nt(state.astype(BF16), q)
            num_t = w_inter * inter[:dh] + _dot(v_t, sqk_t.astype(BF16))
            den = w_inter * inter[dh:dh + 1] + jnp.sum(sqk_t, axis=0, keepdims=True)
            hout_t = num_t * (1.0 / jnp.maximum(jnp.abs(den), row(GATE_FLOOR)))

            vw = jnp.concatenate([(v_t.astype(F32) * wa).astype(BF16),
                                  jnp.where(sub == 0, wa, 0.0).astype(BF16)], axis=0)
            c_ref[h] = row(GATE_S_OLD)[:, :dh] * state + _dot(vw, k)

            if finalize:
                hc = _sigmoid(oc_ref[0, c0:c1, t0:t1].astype(F32)) * (hout_t + hf_ref[0, c0:c1, t0:t1])
                mu = jnp.mean(hc, axis=0, keepdims=True)
                cen = hc - mu
                var = jnp.mean(cen * cen, axis=0, keepdims=True)
                gain = jnp.concatenate([gain_ref[c0:c1]] * (T // LANES), axis=1)
                yc_t = (cen * lax.rsqrt(var + EPS) * gain).astype(BF16)
                out_ref[0, t0:t1, c0:c1] = _dot_nt(eye, yc_t).astype(BF16)
            else:
                out_ref[0, c0:c1, t0:t1] = hout_t


def _mlstm_pass(q, k, v_t, terms, extra, *, reverse):
    B, L, W = k.shape
    cps = MLSTM_CHUNKS_PER_STEP
    T = MLSTM_CHUNK * cps
    n = L // T
    pos = (lambda i: n - 1 - i) if reverse else (lambda i: i)
    rows = pl.BlockSpec((1, T, W), lambda b, i: (b, pos(i), 0))
    cols = pl.BlockSpec((1, W, T), lambda b, i: (b, 0, pos(i)))
    in_specs = [rows, rows, cols,
                pl.BlockSpec((1, cps, N_GATE_TERMS, 8, MLSTM_CHUNK), lambda b, i: (b, pos(i), 0, 0, 0))]
    finalize = extra is not None
    args = [q, k, v_t, terms]
    if finalize:
        in_specs += [cols, cols, _resident((W, LANES))]
        args += list(extra)
    return pl.pallas_call(
        functools.partial(_mlstm_kernel, reverse=reverse, finalize=finalize),
        name="mlstm_bwd" if reverse else "mlstm_fwd",
        grid=(B, n),
        in_specs=in_specs,
        out_specs=rows if finalize else cols,
        out_shape=jax.ShapeDtypeStruct((B, L, W), BF16) if finalize else jax.ShapeDtypeStruct((B, W, L), F32),
        scratch_shapes=[pltpu.VMEM((N_HEADS, STATE_ROWS, HEAD_DIM), F32)],
        compiler_params=_params("parallel", "arbitrary"),
    )(*args)


def _merge_kernel(x_ref, h_ref, ya_ref, yb_ref, yc_ref, yd_ref, wgp_ref, wmg_ref, wbr_ref, wout_ref, g_ref, out_ref):
    W, D = BRANCH_WIDTH, D_MODEL
    h = h_ref[...]
    acc = None
    for g, br_ref in enumerate((ya_ref, yb_ref, yc_ref, yd_ref)):
        gp = _dot(h, wgp_ref[:, g * W:(g + 1) * W])
        gated = (br_ref[...].astype(F32) * (gp * _sigmoid(gp))).astype(BF16)
        proj = _dot(gated, wbr_ref[g])
        term = _sigmoid(_dot(h, wmg_ref[:, g * D:(g + 1) * D])) * proj
        acc = term if acc is None else acc + term
    out = _dot(acc.astype(BF16), wout_ref[...])
    ms = jnp.mean(out * out, axis=-1, keepdims=True)
    out_ref[...] = x_ref[...] + out * lax.rsqrt(ms + EPS) * g_ref[...]


def _merge(x_rows, h_rows, branches, w_gp, w_mg, w_br, w_out, gain):
    R, D = x_rows.shape
    T, W = ROW_TILE, BRANCH_WIDTH
    rows = lambda c: pl.BlockSpec((T, c), lambda i: (i, 0))
    return pl.pallas_call(
        _merge_kernel,
        name="merge",
        grid=(R // T,),
        in_specs=[rows(D), rows(D)] + [rows(W)] * N_BRANCHES
        + [_resident(w_gp.shape), _resident(w_mg.shape), _resident(w_br.shape), _resident(w_out.shape),
           _resident((1, D))],
        out_specs=rows(D),
        out_shape=jax.ShapeDtypeStruct((R, D), F32),
        compiler_params=_params("parallel"),
    )(x_rows, h_rows, *branches, w_gp, w_mg, w_br, w_out, gain.reshape(1, D))


def _t5_bucket(rel):
    nb = REL_BUCKETS // 2
    max_exact = nb // 2
    ret = (rel > 0).astype(np.int32) * nb
    n = np.abs(rel)
    large = max_exact + (np.log(np.maximum(n, 1) / max_exact) / np.log(REL_MAX_DIST / max_exact)
                         * (nb - max_exact)).astype(np.int32)
    large = np.minimum(large, nb - 1)
    return (ret + np.where(n < max_exact, n, large)).astype(np.int32)


def _band_bias(rel_bias, group, dilation):
    qb, hw = ATTN_Q_BLOCK, HALF_WINDOW
    rel = np.arange(qb + 2 * hw)[None, :] - hw - np.arange(qb)[:, None]
    heads = rel_bias[:, group * N_HEADS:(group + 1) * N_HEADS].astype(F32)
    onehot = (jnp.asarray(_t5_bucket(dilation * rel))[None, :, :] == jnp.arange(REL_BUCKETS)[:, None, None])
    bias = jnp.sum(jnp.where(onehot[:, None], heads[:, :, None, None], 0.0), axis=0)
    return jnp.where(jnp.asarray(np.abs(rel) <= hw)[None], bias * (HEAD_DIM ** 0.5), NEG)


def _split_weights(w_in, conv_qk, w_mem_kv, w_branch, w_out):
    W, D, A = BRANCH_WIDTH, D_MODEL, A_QKV_WIDTH
    off = np.cumsum([0, A, A, A, W, W, W, W, W, MLSTM_GATES, W, N_BRANCHES * W, N_BRANCHES * D])
    seg = lambda j: w_in[:, off[j]:off[j + 1]]
    qa, ka, va = seg(0), seg(1), seg(2)
    w_a = [jnp.concatenate([t[:, g * W:(g + 1) * W] for t in (qa, ka, va)], axis=1).astype(BF16)
           for g in range(len(DILATIONS))]
    return dict(
        a=w_a,
        bd=jnp.concatenate([seg(3), seg(9)], axis=1).astype(BF16),
        qk=jnp.concatenate([seg(4), seg(5)], axis=1).astype(BF16),
        vo_t=jnp.concatenate([seg(6), seg(7)], axis=1).T.astype(BF16),
        g=seg(8).astype(BF16),
        gp=seg(10).astype(BF16),
        mg=seg(11).astype(BF16),
        conv=conv_qk.astype(F32),
        memkv=w_mem_kv.astype(BF16),
        br=w_branch.astype(BF16),
        out=w_out.astype(BF16),
    )


def _layer(x, mem, biases, fft_tables, wts, norm_pre, gate_bias, head_gain, mem_norm, norm_post):
    B, L, D = x.shape
    W = BRANCH_WIDTH
    h, h4, h16 = _prenorm(x, norm_pre)

    outs, lses = [], []
    for g, (d, hp) in enumerate(zip(DILATIONS, (h, h4, h16))):
        q, kv = _proj_a(hp.reshape(B * L, D), wts["a"][g])
        o, lse = _attention(q.reshape(B * d, L // d, W), kv.reshape(B * d, L // d, 2 * W), biases[g])
        outs.append(o.reshape(B, d, L // d, W))
        lses.append(lse.reshape(B, d, L // d, LANES))
    ya = _combine(*outs, *lses)

    kv_mem = _memkv(mem, mem_norm, wts["memkv"])
    xb, yd = _proj_bd(h, wts["bd"], kv_mem)
    yb = _fnet(xb, fft_tables)

    qm, km, v_t, oc_t, gates = _proj_c(h, wts["qk"], wts["vo_t"], wts["g"], wts["conv"], gate_bias.reshape(-1))
    T = MLSTM_CHUNK
    gates_t = jnp.swapaxes(gates.reshape(B, L // T, T, MLSTM_GATES), 2, 3)
    terms_fwd = _gate_terms(gates_t[:, :, :8], reverse=False)
    terms_bwd = _gate_terms(gates_t[:, :, 8:], reverse=True)
    gain = jnp.broadcast_to(head_gain.astype(F32)[:, None], (W, LANES))
    h_fwd_t = _mlstm_pass(qm, km, v_t, terms_fwd, None, reverse=False)
    yc = _mlstm_pass(qm, km, v_t, terms_bwd, (h_fwd_t, oc_t, gain), reverse=True)

    rows = lambda t: t.reshape(B * L, t.shape[-1])
    y = _merge(rows(x), rows(h), [rows(ya), rows(yb), rows(yc), rows(yd)],
               wts["gp"], wts["mg"], wts["br"], wts["out"], norm_post)
    return y.reshape(B, L, D)


def kernel(x_prompt, x_sample, mem_prompt, mem_sample, rel_bias, norm_pre, w_in, conv_qk, mlstm_gate_bias,
           mlstm_head_gain, mem_norm, w_mem_kv, w_branch, w_out, norm_post):
    depth = w_in.shape[0]
    biases = [_band_bias(rel_bias, g, d) for g, d in enumerate(DILATIONS)]
    layer_wts = [_split_weights(w_in[l], conv_qk[l], w_mem_kv[l], w_branch[l], w_out[l]) for l in range(depth)]
    outs = []
    for x, mem in ((x_prompt, mem_prompt), (x_sample, mem_sample)):
        tables = _fnet_tables(x.shape[1])
        for l in range(depth):
            x = _layer(x, mem, biases, tables, layer_wts[l], norm_pre[l], mlstm_gate_bias[l],
                       mlstm_head_gain[l], mem_norm[l], norm_post[l])
        outs.append(x)
    return tuple(outs)
```

```python
import functools
import math

import numpy as np
import jax
import jax.numpy as jnp
from jax import lax
from jax.experimental import pallas as pl
from jax.experimental.pallas import tpu as pltpu

D_MODEL = 1024
BRANCH_WIDTH = 512
HEAD_DIM = 128
N_HEADS = 4
N_BRANCHES = 4
DILATIONS = (1, 4, 16)
HALF_WINDOW = 64
A_QKV_WIDTH = len(DILATIONS) * BRANCH_WIDTH
REL_BUCKETS = 32
REL_MAX_DIST = 1024
FNET_GROUP_WIDTH = 128
MLSTM_CONV = 5
MLSTM_GATES = 16
N_MEM = 256
EPS = 1e-6
NEG = -1e30
LOG2_E = math.log2(math.e)

F32 = jnp.float32
BF16 = jnp.bfloat16
LANES = 128
LSE_LANES_PER_HEAD = LANES // N_HEADS

VMEM_LIMIT_BYTES = 56 * 1024 * 1024
PERM_TILE = 1024
ROW_TILE = 512
PROJ_TILE = 2048
ATTN_TILE = 2048
ATTN_Q_BLOCK = 128
MLSTM_CHUNK = 256
MLSTM_CHUNKS_PER_STEP = 4
FFT_ROWS = 8
CONV_HALO = 16
CONV_COLS = 256


def _params(*sem):
    return pltpu.CompilerParams(dimension_semantics=sem, vmem_limit_bytes=VMEM_LIMIT_BYTES)


def _resident(shape):
    nd = len(shape)
    return pl.BlockSpec(shape, lambda *_: (0,) * nd, pipeline_mode=pl.Buffered(1))


def _sigmoid(x):
    return 0.5 * jnp.tanh(0.5 * x) + 0.5


def _log_sigmoid(x):
    return jnp.minimum(x, 0.0) - jnp.log(1.0 + jnp.exp(-jnp.abs(x)))


def _dot(a, b):
    return jnp.dot(a, b, preferred_element_type=F32)


def _dot_nt(a, b):
    return lax.dot_general(a, b, (((1,), (1,)), ((), ())), preferred_element_type=F32)


def _dot_tn(a, b):
    return lax.dot_general(a, b, (((0,), (0,)), ((), ())), preferred_element_type=F32)


def _prenorm_kernel(x_ref, g_ref, h_ref, h4_ref, h16_ref, xs_ref):
    T = PERM_TILE
    x = x_ref[0]
    ms = jnp.mean(x * x, axis=-1, keepdims=True)
    xn = x * lax.rsqrt(ms + EPS) * g_ref[...]
    h_ref[0] = xn.astype(BF16)
    for c in range(D_MODEL // LANES):
        xs_ref[c] = xn[:, c * LANES:(c + 1) * LANES]
    for d, out_ref in ((4, h4_ref), (16, h16_ref)):
        for r in range(d):
            for c in range(D_MODEL // LANES):
                out_ref[0, r, :, c * LANES:(c + 1) * LANES] = xs_ref[c, pl.ds(r, T // d, stride=d), :].astype(BF16)


def _prenorm(x, gain):
    B, L, D = x.shape
    T = PERM_TILE
    return pl.pallas_call(
        _prenorm_kernel,
        name="prenorm",
        grid=(B, L // T),
        in_specs=[pl.BlockSpec((1, T, D), lambda b, i: (b, i, 0)),
                  pl.BlockSpec((1, D), lambda b, i: (0, 0))],
        out_specs=[pl.BlockSpec((1, T, D), lambda b, i: (b, i, 0)),
                   pl.BlockSpec((1, 4, T // 4, D), lambda b, i: (b, 0, i, 0)),
                   pl.BlockSpec((1, 16, T // 16, D), lambda b, i: (b, 0, i, 0))],
        out_shape=[jax.ShapeDtypeStruct((B, L, D), BF16),
                   jax.ShapeDtypeStruct((B, 4, L // 4, D), BF16),
                   jax.ShapeDtypeStruct((B, 16, L // 16, D), BF16)],
        scratch_shapes=[pltpu.VMEM((D // LANES, T, LANES), F32)],
        compiler_params=_params("parallel", "parallel"),
    )(x, gain.reshape(1, D))


def _proj_a_kernel(h_ref, w_ref, q_ref, kv_ref):
    z = _dot(h_ref[...], w_ref[...])
    q_ref[...] = z[:, :BRANCH_WIDTH].astype(BF16)
    kv_ref[...] = z[:, BRANCH_WIDTH:].astype(BF16)


def _proj_a(h_rows, w_qkv):
    R, D = h_rows.shape
    T = PROJ_TILE
    W = BRANCH_WIDTH
    return pl.pallas_call(
        _proj_a_kernel,
        name="proj_a",
        grid=(R // T,),
        in_specs=[pl.BlockSpec((T, D), lambda i: (i, 0)), _resident((D, 3 * W))],
        out_specs=[pl.BlockSpec((T, W), lambda i: (i, 0)), pl.BlockSpec((T, 2 * W), lambda i: (i, 0))],
        out_shape=[jax.ShapeDtypeStruct((R, W), BF16), jax.ShapeDtypeStruct((R, 2 * W), BF16)],
        compiler_params=_params("parallel"),
    )(h_rows, w_qkv)


def _attn_kernel(q_ref, kv_ref, kvp_ref, kvn_ref, bias_ref, o_ref, lse_ref, win_ref, *, seq_len, tq):
    i = pl.program_id(1)
    hw, qb, W = HALF_WINDOW, ATTN_Q_BLOCK, BRANCH_WIDTH
    kb = qb + 2 * hw
    win_ref[0:hw] = kvp_ref[0]
    win_ref[hw:hw + tq] = kv_ref[0]
    win_ref[hw + tq:] = kvn_ref[0]
    scale = HEAD_DIM ** -0.5
    lane_head = lax.broadcasted_iota(jnp.int32, (qb, LANES), 1) // LSE_LANES_PER_HEAD
    n_blocks = tq // qb
    for j in range(n_blocks):
        a = j * qb
        kidx = i * tq + (a - hw) + lax.broadcasted_iota(jnp.int32, (1, kb), 1)
        valid = (kidx >= 0) & (kidx < seq_len) if j in (0, n_blocks - 1) else None
        lse_all = None
        for h in range(N_HEADS):
            c0, c1 = h * HEAD_DIM, (h + 1) * HEAD_DIM
            q = q_ref[0, a:a + qb, c0:c1]
            k = win_ref[a:a + kb, c0:c1]
            v = win_ref[a:a + kb, W + c0:W + c1]
            s = _dot_nt(q, k) + bias_ref[h]
            if valid is not None:
                s = jnp.where(valid, s, NEG)
            m = jnp.max(s, axis=-1, keepdims=True)
            p = jnp.exp2((s - m) * (scale * LOG2_E))
            l = jnp.sum(p, axis=-1, keepdims=True)
            o = _dot(p.astype(BF16), v) * (1.0 / l)
            o_ref[0, a:a + qb, c0:c1] = o.astype(BF16)
            lse = jnp.broadcast_to(m * scale + jnp.log(l), (qb, LANES))
            lse_all = lse if h == 0 else jnp.where(lane_head == h, lse, lse_all)
        lse_ref[0, a:a + qb, :] = lse_all


def _attention(q, kv, bias):
    S, M, W = q.shape
    hw = HALF_WINDOW
    tq = min(M, ATTN_TILE)
    nhb = M // hw
    step = tq // hw
    kern = functools.partial(_attn_kernel, seq_len=M, tq=tq)
    return pl.pallas_call(
        kern,
        name="attn",
        grid=(S, M // tq),
        in_specs=[pl.BlockSpec((1, tq, W), lambda s, i: (s, i, 0)),
                  pl.BlockSpec((1, tq, 2 * W), lambda s, i: (s, i, 0)),
                  pl.BlockSpec((1, hw, 2 * W), lambda s, i: (s, jnp.maximum(i * step - 1, 0), 0)),
                  pl.BlockSpec((1, hw, 2 * W), lambda s, i: (s, jnp.minimum((i + 1) * step, nhb - 1), 0)),
                  _resident(bias.shape)],
        out_specs=[pl.BlockSpec((1, tq, W), lambda s, i: (s, i, 0)),
                   pl.BlockSpec((1, tq, LANES), lambda s, i: (s, i, 0))],
        out_shape=[jax.ShapeDtypeStruct((S, M, W), BF16), jax.ShapeDtypeStruct((S, M, LANES), F32)],
        scratch_shapes=[pltpu.VMEM((tq + 2 * hw, 2 * W), BF16)],
        compiler_params=_params("parallel", "parallel"),
    )(q, kv, kv, kv, bias)


def _combine_kernel(o1_ref, o4_ref, o16_ref, l1_ref, l4_ref, l16_ref, ya_ref, os_ref, ls_ref):
    T, nc = PERM_TILE, BRANCH_WIDTH // LANES
    for g, (d, o_ref, l_ref) in enumerate(((4, o4_ref, l4_ref), (16, o16_ref, l16_ref))):
        for r in range(d):
            ls_ref[g, pl.ds(r, T // d, stride=d), :] = l_ref[0, r]
            for c in range(nc):
                os_ref[g, c, pl.ds(r, T // d, stride=d), :] = o_ref[0, r, :, c * LANES:(c + 1) * LANES].astype(F32)
    l1, l4, l16 = l1_ref[0, 0], ls_ref[0], ls_ref[1]
    m = jnp.maximum(jnp.maximum(l1, l4), l16)
    e1, e4, e16 = jnp.exp(l1 - m), jnp.exp(l4 - m), jnp.exp(l16 - m)
    inv = 1.0 / (e1 + e4 + e16)
    w1, w4, w16 = e1 * inv, e4 * inv, e16 * inv
    for h in range(N_HEADS):
        c0, c1 = h * HEAD_DIM, (h + 1) * HEAD_DIM
        lh = h * LSE_LANES_PER_HEAD
        ya = (w1[:, lh:lh + 1] * o1_ref[0, 0, :, c0:c1].astype(F32)
              + w4[:, lh:lh + 1] * os_ref[0, h] + w16[:, lh:lh + 1] * os_ref[1, h])
        ya_ref[0, :, c0:c1] = ya.astype(BF16)


def _combine(o1, o4, o16, l1, l4, l16):
    B, _, L, W = o1.shape
    T = PERM_TILE

    def spec(d, c):
        return pl.BlockSpec((1, d, T // d, c), lambda b, i: (b, 0, i, 0))

    return pl.pallas_call(
        _combine_kernel,
        name="combine",
        grid=(B, L // T),
        in_specs=[spec(1, W), spec(4, W), spec(16, W), spec(1, LANES), spec(4, LANES), spec(16, LANES)],
        out_specs=pl.BlockSpec((1, T, W), lambda b, i: (b, i, 0)),
        out_shape=jax.ShapeDtypeStruct((B, L, W), BF16),
        scratch_shapes=[pltpu.VMEM((2, W // LANES, T, LANES), F32), pltpu.VMEM((2, T, LANES), F32)],
        compiler_params=_params("parallel", "parallel"),
    )(o1, o4, o16, l1, l4, l16)


def _memkv_kernel(mem_ref, g_ref, w_ref, kv_ref):
    x = mem_ref[0]
    ms = jnp.mean(x * x, axis=-1, keepdims=True)
    xn = (x * lax.rsqrt(ms + EPS) * g_ref[...]).astype(BF16)
    kv_ref[0] = _dot(xn, w_ref[...]).astype(BF16)


def _memkv(mem, gain, w):
    B, N, D = mem.shape
    W2 = w.shape[1]
    return pl.pallas_call(
        _memkv_kernel,
        name="memkv",
        grid=(B,),
        in_specs=[pl.BlockSpec((1, N, D), lambda b: (b, 0, 0)), pl.BlockSpec((1, D), lambda b: (0, 0)),
                  _resident((D, W2))],
        out_specs=pl.BlockSpec((1, N, W2), lambda b: (b, 0, 0)),
        out_shape=jax.ShapeDtypeStruct((B, N, W2), BF16),
        compiler_params=_params("parallel"),
    )(mem, gain.reshape(1, D), w)


def _proj_bd_kernel(h_ref, w_ref, kv_ref, xb_ref, yd_ref):
    W = BRANCH_WIDTH
    z = _dot(h_ref[0], w_ref[...])
    xb_ref[0] = z[:, :W]
    qd = z[:, W:].astype(BF16)
    scale = HEAD_DIM ** -0.5
    for h in range(N_HEADS):
        c0, c1 = h * HEAD_DIM, (h + 1) * HEAD_DIM
        s = _dot_nt(qd[:, c0:c1], kv_ref[0, :, c0:c1])
        m = jnp.max(s, axis=-1, keepdims=True)
        p = jnp.exp2((s - m) * (scale * LOG2_E))
        l = jnp.sum(p, axis=-1, keepdims=True)
        o = _dot(p.astype(BF16), kv_ref[0, :, W + c0:W + c1]) * (1.0 / l)
        yd_ref[0, :, c0:c1] = o.astype(BF16)


def _proj_bd(h, w_bd, kv):
    B, L, D = h.shape
    T, W = PROJ_TILE, BRANCH_WIDTH
    return pl.pallas_call(
        _proj_bd_kernel,
        name="proj_bd",
        grid=(B, L // T),
        in_specs=[pl.BlockSpec((1, T, D), lambda b, i: (b, i, 0)), _resident((D, 2 * W)),
                  pl.BlockSpec((1, N_MEM, 2 * W), lambda b, i: (b, 0, 0))],
        out_specs=[pl.BlockSpec((1, T, W), lambda b, i: (b, i, 0)), pl.BlockSpec((1, T, W), lambda b, i: (b, i, 0))],
        out_shape=[jax.ShapeDtypeStruct((B, L, W), F32), jax.ShapeDtypeStruct((B, L, W), BF16)],
        compiler_params=_params("parallel", "parallel"),
    )(h, w_bd, kv)


def _fft_factors(L):
    n1 = 1 << ((L.bit_length() - 1 + 1) // 2)
    return n1, L // n1


def _dft_cos_sin(n):
    ang = 2.0 * np.pi * ((np.arange(n)[:, None] * np.arange(n)[None, :]) % n) / n
    return np.cos(ang), np.sin(ang)


def _fft1_kernel(x_ref, f_ref, tc_ref, ts_ref, yr_ref, yi_ref, xs_ref, *, n1):
    W = BRANCH_WIDTH
    for j in range(FFT_ROWS):
        xs_ref[:, j * W:(j + 1) * W] = x_ref[0, :, j, :]
    y = _dot(f_ref[...], xs_ref[...].astype(BF16))
    reps = W // FNET_GROUP_WIDTH
    for j in range(FFT_ROWS):
        yr, yi = y[:n1, j * W:(j + 1) * W], y[n1:, j * W:(j + 1) * W]
        tc = jnp.concatenate([tc_ref[j]] * reps, axis=1)
        ts = jnp.concatenate([ts_ref[j]] * reps, axis=1)
        yr_ref[0, j] = yr * tc + yi * ts
        yi_ref[0, j] = yi * tc - yr * ts


def _fft2_kernel(yr_ref, yi_ref, g1_ref, g2_ref, cs_ref, out_ref, ys_ref, *, n2, norm):
    W, gw = BRANCH_WIDTH, FNET_GROUP_WIDTH
    for j in range(FFT_ROWS):
        ys_ref[0, :, j * W:(j + 1) * W] = yr_ref[0, :, j, :]
        ys_ref[1, :, j * W:(j + 1) * W] = yi_ref[0, :, j, :]
    u = _dot(g1_ref[...], ys_ref[0].astype(BF16)) + _dot(g2_ref[...], ys_ref[1].astype(BF16))
    top, bot = u[:n2].astype(BF16), u[n2:].astype(BF16)
    ng = FFT_ROWS * W // gw
    lhs = jnp.concatenate(
        [jnp.concatenate([top[:, j * gw:(j + 1) * gw], bot[:, j * gw:(j + 1) * gw]], axis=1) for j in range(ng)],
        axis=0)
    res = _dot(lhs, cs_ref[...]) * norm
    for j in range(ng):
        k1, c = divmod(j, W // gw)
        out_ref[0, :, k1, c * gw:(c + 1) * gw] = res[j * n2:(j + 1) * n2]


def _fnet(xb, tables):
    B, L, W = xb.shape
    n1, n2 = _fft_factors(L)
    f1, twc, tws, g1, g2, cs = tables
    R = FFT_ROWS
    yr, yi = pl.pallas_call(
        functools.partial(_fft1_kernel, n1=n1),
        name="fft1",
        grid=(B, n2 // R),
        in_specs=[pl.BlockSpec((1, n1, R, W), lambda b, j: (b, 0, j, 0)), _resident((2 * n1, n1)),
                  pl.BlockSpec((R, n1, FNET_GROUP_WIDTH), lambda b, j: (j, 0, 0)),
                  pl.BlockSpec((R, n1, FNET_GROUP_WIDTH), lambda b, j: (j, 0, 0))],
        out_specs=[pl.BlockSpec((1, R, n1, W), lambda b, j: (b, j, 0, 0))] * 2,
        out_shape=[jax.ShapeDtypeStruct((B, n2, n1, W), F32)] * 2,
        scratch_shapes=[pltpu.VMEM((n1, R * W), F32)],
        compiler_params=_params("parallel", "parallel"),
    )(xb.reshape(B, n1, n2, W), f1, twc, tws)
    out = pl.pallas_call(
        functools.partial(_fft2_kernel, n2=n2, norm=float(1.0 / math.sqrt(L * FNET_GROUP_WIDTH))),
        name="fft2",
        grid=(B, n1 // R),
        in_specs=[pl.BlockSpec((1, n2, R, W), lambda b, j: (b, 0, j, 0))] * 2
        + [_resident((2 * n2, n2)), _resident((2 * n2, n2)), _resident((2 * FNET_GROUP_WIDTH, FNET_GROUP_WIDTH))],
        out_specs=pl.BlockSpec((1, n2, R, W), lambda b, j: (b, 0, j, 0)),
        out_shape=jax.ShapeDtypeStruct((B, n2, n1, W), F32),
        scratch_shapes=[pltpu.VMEM((2, n2, R * W), F32)],
        compiler_params=_params("parallel", "parallel"),
    )(yr, yi, g1, g2, cs)
    return out.reshape(B, L, W)


def _fnet_tables(L):
    n1, n2 = _fft_factors(L)
    c1, s1 = _dft_cos_sin(n1)
    c2, s2 = _dft_cos_sin(n2)
    cc, sc = _dft_cos_sin(FNET_GROUP_WIDTH)
    ang = 2.0 * np.pi * (np.arange(n2)[:, None] * np.arange(n1)[None, :]) / L
    bcast = lambda t: jnp.asarray(np.broadcast_to(t[:, :, None], (n2, n1, FNET_GROUP_WIDTH)), F32)
    return (jnp.asarray(np.concatenate([c1, -s1], axis=0), BF16), bcast(np.cos(ang)), bcast(np.sin(ang)),
            jnp.asarray(np.concatenate([c2, -s2], axis=0), BF16), jnp.asarray(np.concatenate([s2, c2], axis=0), BF16),
            jnp.asarray(np.concatenate([cc, sc], axis=0), BF16))


def _proj_c_kernel(h_ref, hp_ref, hn_ref, wqk_ref, wvo_ref, wg_ref, conv_ref, gb_ref,
                   q_ref, k_ref, v_ref, o_ref, g_ref):
    i = pl.program_id(1)
    T, W, halo = PROJ_TILE, BRANCH_WIDTH, CONV_HALO
    h = h_ref[0]
    has_prev = (i > 0).astype(F32)
    has_next = (i < pl.num_programs(1) - 1).astype(F32)
    pad = MLSTM_CONV // 2
    n_ext = T + 2 * halo
    for c in range(2 * W // CONV_COLS):
        cs = slice(c * CONV_COLS, (c + 1) * CONV_COLS)
        w = wqk_ref[:, cs]
        ext = jnp.concatenate([_dot(hp_ref[0], w) * has_prev, _dot(h, w), _dot(hn_ref[0], w) * has_next], axis=0)
        acc = ext[halo:halo + T] * conv_ref[pad:pad + 1, cs]
        for j in range(MLSTM_CONV):
            if j != pad:
                acc = acc + pltpu.roll(ext, (pad - j) % n_ext, 0)[halo:halo + T] * conv_ref[j:j + 1, cs]
        y = acc * _sigmoid(acc)
        if c < W // CONV_COLS:
            q_ref[0, :, cs] = (y * (HEAD_DIM ** -0.5)).astype(BF16)
        else:
            k_ref[0, :, c * CONV_COLS - W:(c + 1) * CONV_COLS - W] = y.astype(BF16)
    vo_t = _dot_nt(wvo_ref[...], h)
    v_ref[0] = vo_t[:W].astype(BF16)
    o_ref[0] = vo_t[W:].astype(BF16)
    g_ref[0] = _dot(h, wg_ref[...]) + gb_ref[...]


def _proj_c(h, w_qk, w_vo_t, w_g, conv, gate_bias):
    B, L, D = h.shape
    T, W, halo = PROJ_TILE, BRANCH_WIDTH, CONV_HALO
    step = T // halo
    nhb = L // halo
    tile = lambda c: pl.BlockSpec((1, T, c), lambda b, i: (b, i, 0))
    tile_t = pl.BlockSpec((1, W, T), lambda b, i: (b, 0, i))
    rows, cols = jax.ShapeDtypeStruct((B, L, W), BF16), jax.ShapeDtypeStruct((B, W, L), BF16)
    return pl.pallas_call(
        _proj_c_kernel,
        name="proj_c",
        grid=(B, L // T),
        in_specs=[tile(D),
                  pl.BlockSpec((1, halo, D), lambda b, i: (b, jnp.maximum(i * step - 1, 0), 0)),
                  pl.BlockSpec((1, halo, D), lambda b, i: (b, jnp.minimum((i + 1) * step, nhb - 1), 0)),
                  _resident((D, 2 * W)), _resident((2 * W, D)), _resident((D, MLSTM_GATES)),
                  _resident((MLSTM_CONV, 2 * W)), _resident((1, MLSTM_GATES))],
        out_specs=[tile(W), tile(W), tile_t, tile_t, tile(MLSTM_GATES)],
        out_shape=[rows, rows, cols, cols, jax.ShapeDtypeStruct((B, L, MLSTM_GATES), F32)],
        compiler_params=_params("parallel", "arbitrary"),
    )(h, h, h, w_qk, w_vo_t, w_g, conv, gate_bias.reshape(1, MLSTM_GATES))


GATE_U, GATE_M, GATE_W_INTER, GATE_FLOOR, GATE_WA, GATE_S_OLD = range(6)
N_GATE_TERMS = 6
STATE_ROWS = HEAD_DIM + 16


def _gate_terms_kernel(x_ref, out_ref, g_ref, pe_ref, mp_ref, *, reverse):
    nc, T, H = x_ref.shape[1], MLSTM_CHUNK, N_HEADS
    x = x_ref[0].reshape(nc * 8, T)
    sub = lax.broadcasted_iota(jnp.int32, (nc * 8, T), 0) % 8
    lane = lax.broadcasted_iota(jnp.int32, (nc * 8, T), 1)
    src = lax.broadcasted_iota(jnp.int32, (T, T), 0)
    tgt = lax.broadcasted_iota(jnp.int32, (T, T), 1)
    tri = ((src >= tgt) if reverse else (src <= tgt)).astype(F32)
    cs = jnp.dot(jnp.where(sub >= H, _log_sigmoid(x), x), tri,
                 preferred_element_type=F32, precision=lax.Precision.HIGHEST)
    b = pltpu.roll(cs, nc * 8 - H, 0)
    u = x - b
    p = u
    shift = 1
    while shift < T:
        if reverse:
            p = jnp.maximum(p, jnp.where(lane < T - shift, pltpu.roll(p, T - shift, 1), NEG))
        else:
            p = jnp.maximum(p, jnp.where(lane >= shift, pltpu.roll(p, shift, 1), NEG))
        shift *= 2
    last = 0 if reverse else T - 1
    g = jnp.broadcast_to(b[:, last:last + 1], (nc * 8, T))
    p_end = jnp.broadcast_to(p[:, last:last + 1], (nc * 8, T))
    g_ref[...] = g[:, :LANES].reshape(nc, 8, LANES)
    pe_ref[...] = p_end[:, :LANES].reshape(nc, 8, LANES)

    def step(c, m):
        cc = nc - 1 - c if reverse else c
        mp_ref[cc] = m
        return g_ref[cc] + jnp.maximum(m, pe_ref[cc])

    lax.fori_loop(0, nc, step, jnp.zeros((8, LANES), F32))
    m = jnp.concatenate([mp_ref[...].reshape(nc * 8, LANES)] * (T // LANES), axis=1)
    big_m = jnp.maximum(m, p)
    m_next = g + jnp.maximum(m, p_end)
    terms = {GATE_U: u, GATE_M: big_m, GATE_W_INTER: jnp.exp(m - big_m), GATE_FLOOR: jnp.exp(-(b + big_m)),
             GATE_WA: jnp.exp(g + u - m_next), GATE_S_OLD: jnp.exp(g + m - m_next)}
    for j, val in terms.items():
        out_ref[0, :, j] = jnp.where(sub < H, val, 0.0).reshape(nc, 8, T)


def _gate_terms(gates_dir, *, reverse):
    B, nc, _, T = gates_dir.shape
    return pl.pallas_call(
        functools.partial(_gate_terms_kernel, reverse=reverse),
        name="gate_terms",
        grid=(B,),
        in_specs=[pl.BlockSpec((1, nc, 8, T), lambda b: (b, 0, 0, 0))],
        out_specs=pl.BlockSpec((1, nc, N_GATE_TERMS, 8, T), lambda b: (b, 0, 0, 0, 0)),
        out_shape=jax.ShapeDtypeStruct((B, nc, N_GATE_TERMS, 8, T), F32),
        scratch_shapes=[pltpu.VMEM((nc, 8, LANES), F32)] * 3,
        compiler_params=_params("parallel"),
    )(gates_dir)


def _mlstm_kernel(*refs, reverse, finalize):
    if finalize:
        (q_ref, k_ref, v_ref, r_ref, hf_ref, oc_ref, gain_ref, out_ref, c_ref) = refs
    else:
        (q_ref, k_ref, v_ref, r_ref, out_ref, c_ref) = refs
    T, dh, H = MLSTM_CHUNK, HEAD_DIM, N_HEADS

    @pl.when(pl.program_id(1) == 0)
    def _():
        c_ref[...] = jnp.zeros_like(c_ref)

    src = lax.broadcasted_iota(jnp.int32, (T, T), 0)
    tgt = lax.broadcasted_iota(jnp.int32, (T, T), 1)
    feeds = (src >= tgt) if reverse else (src <= tgt)
    sub = lax.broadcasted_iota(jnp.int32, (STATE_ROWS - dh, T), 0)
    if finalize:
        eye = (src == tgt).astype(BF16)

    chunks = range(MLSTM_CHUNKS_PER_STEP)
    for ci in (reversed(chunks) if reverse else chunks):
        t0, t1 = ci * T, (ci + 1) * T
        u_cols = jnp.transpose(r_ref[0, ci, GATE_U])
        for h in range(H):
            c0, c1 = h * dh, (h + 1) * dh
            q, k, v_t = q_ref[0, t0:t1, c0:c1], k_ref[0, t0:t1, c0:c1], v_ref[0, c0:c1, t0:t1]
            row = lambda j: r_ref[0, ci, j, h:h + 1, :]
            w_inter, wa = row(GATE_W_INTER), row(GATE_WA)
            state = c_ref[h]

            logw = jnp.where(feeds, u_cols[:, h:h + 1] - row(GATE_M), NEG)
            sqk_t = jnp.exp(logw) * _dot_nt(k, q)
            inter = _dot_nt(state.astype(BF16), q)
            num_t = w_inter * inter[:dh] + _dot(v_t, sqk_t.astype(BF16))
            den = w_inter * inter[dh:dh + 1] + jnp.sum(sqk_t, axis=0, keepdims=True)
            hout_t = num_t * (1.0 / jnp.maximum(jnp.abs(den), row(GATE_FLOOR)))

            vw = jnp.concatenate([(v_t.astype(F32) * wa).astype(BF16),
                                  jnp.where(sub == 0, wa, 0.0).astype(BF16)], axis=0)
            c_ref[h] = row(GATE_S_OLD)[:, :dh] * state + _dot(vw, k)

            if finalize:
                hc = _sigmoid(oc_ref[0, c0:c1, t0:t1].astype(F32)) * (hout_t + hf_ref[0, c0:c1, t0:t1])
                mu = jnp.mean(hc, axis=0, keepdims=True)
                cen = hc - mu
                var = jnp.mean(cen * cen, axis=0, keepdims=True)
                gain = jnp.concatenate([gain_ref[c0:c1]] * (T // LANES), axis=1)
                yc_t = cen * lax.rsqrt(var + EPS) * gain
                out_ref[0, t0:t1, c0:c1] = jnp.transpose(yc_t).astype(BF16)
            else:
                out_ref[0, c0:c1, t0:t1] = hout_t


def _mlstm_pass(q, k, v_t, terms, extra, *, reverse):
    B, L, W = k.shape
    cps = MLSTM_CHUNKS_PER_STEP
    T = MLSTM_CHUNK * cps
    n = L // T
    pos = (lambda i: n - 1 - i) if reverse else (lambda i: i)
    rows = pl.BlockSpec((1, T, W), lambda b, i: (b, pos(i), 0))
    cols = pl.BlockSpec((1, W, T), lambda b, i: (b, 0, pos(i)))
    in_specs = [rows, rows, cols,
                pl.BlockSpec((1, cps, N_GATE_TERMS, 8, MLSTM_CHUNK), lambda b, i: (b, pos(i), 0, 0, 0))]
    finalize = extra is not None
    args = [q, k, v_t, terms]
    if finalize:
        in_specs += [cols, cols, _resident((W, LANES))]
        args += list(extra)
    return pl.pallas_call(
        functools.partial(_mlstm_kernel, reverse=reverse, finalize=finalize),
        name="mlstm_bwd" if reverse else "mlstm_fwd",
        grid=(B, n),
        in_specs=in_specs,
        out_specs=rows if finalize else cols,
        out_shape=jax.ShapeDtypeStruct((B, L, W), BF16) if finalize else jax.ShapeDtypeStruct((B, W, L), F32),
        scratch_shapes=[pltpu.VMEM((N_HEADS, STATE_ROWS, HEAD_DIM), F32)],
        compiler_params=_params("parallel", "arbitrary"),
    )(*args)


def _merge_kernel(x_ref, h_ref, ya_ref, yb_ref, yc_ref, yd_ref, wgp_ref, wmg_ref, wbr_ref, wout_ref, g_ref, out_ref):
    W, D = BRANCH_WIDTH, D_MODEL
    h = h_ref[...]
    acc = None
    for g, br_ref in enumerate((ya_ref, yb_ref, yc_ref, yd_ref)):
        gp = _dot(h, wgp_ref[:, g * W:(g + 1) * W])
        gated = (br_ref[...].astype(F32) * (gp * _sigmoid(gp))).astype(BF16)
        proj = _dot(gated, wbr_ref[g])
        term = _sigmoid(_dot(h, wmg_ref[:, g * D:(g + 1) * D])) * proj
        acc = term if acc is None else acc + term
    out = _dot(acc.astype(BF16), wout_ref[...])
    ms = jnp.mean(out * out, axis=-1, keepdims=True)
    out_ref[...] = x_ref[...] + out * lax.rsqrt(ms + EPS) * g_ref[...]


def _merge(x_rows, h_rows, branches, w_gp, w_mg, w_br, w_out, gain):
    R, D = x_rows.shape
    T, W = ROW_TILE, BRANCH_WIDTH
    rows = lambda c: pl.BlockSpec((T, c), lambda i: (i, 0))
    return pl.pallas_call(
        _merge_kernel,
        name="merge",
        grid=(R // T,),
        in_specs=[rows(D), rows(D)] + [rows(W)] * N_BRANCHES
        + [_resident(w_gp.shape), _resident(w_mg.shape), _resident(w_br.shape), _resident(w_out.shape),
           _resident((1, D))],
        out_specs=rows(D),
        out_shape=jax.ShapeDtypeStruct((R, D), F32),
        compiler_params=_params("parallel"),
    )(x_rows, h_rows, *branches, w_gp, w_mg, w_br, w_out, gain.reshape(1, D))


def _t5_bucket(rel):
    nb = REL_BUCKETS // 2
    max_exact = nb // 2
    ret = (rel > 0).astype(np.int32) * nb
    n = np.abs(rel)
    large = max_exact + (np.log(np.maximum(n, 1) / max_exact) / np.log(REL_MAX_DIST / max_exact)
                         * (nb - max_exact)).astype(np.int32)
    large = np.minimum(large, nb - 1)
    return (ret + np.where(n < max_exact, n, large)).astype(np.int32)


def _band_bias(rel_bias, group, dilation):
    qb, hw = ATTN_Q_BLOCK, HALF_WINDOW
    rel = np.arange(qb + 2 * hw)[None, :] - hw - np.arange(qb)[:, None]
    heads = rel_bias[:, group * N_HEADS:(group + 1) * N_HEADS].astype(F32)
    onehot = (jnp.asarray(_t5_bucket(dilation * rel))[None, :, :] == jnp.arange(REL_BUCKETS)[:, None, None])
    bias = jnp.sum(jnp.where(onehot[:, None], heads[:, :, None, None], 0.0), axis=0)
    return jnp.where(jnp.asarray(np.abs(rel) <= hw)[None], bias * (HEAD_DIM ** 0.5), NEG)


def _split_weights(w_in, conv_qk, w_mem_kv, w_branch, w_out):
    W, D, A = BRANCH_WIDTH, D_MODEL, A_QKV_WIDTH
    off = np.cumsum([0, A, A, A, W, W, W, W, W, MLSTM_GATES, W, N_BRANCHES * W, N_BRANCHES * D])
    seg = lambda j: w_in[:, off[j]:off[j + 1]]
    qa, ka, va = seg(0), seg(1), seg(2)
    w_a = [jnp.concatenate([t[:, g * W:(g + 1) * W] for t in (qa, ka, va)], axis=1).astype(BF16)
           for g in range(len(DILATIONS))]
    return dict(
        a=w_a,
        bd=jnp.concatenate([seg(3), seg(9)], axis=1).astype(BF16),
        qk=jnp.concatenate([seg(4), seg(5)], axis=1).astype(BF16),
        vo_t=jnp.concatenate([seg(6), seg(7)], axis=1).T.astype(BF16),
        g=seg(8).astype(BF16),
        gp=seg(10).astype(BF16),
        mg=seg(11).astype(BF16),
        conv=conv_qk.astype(F32),
        memkv=w_mem_kv.astype(BF16),
        br=w_branch.astype(BF16),
        out=w_out.astype(BF16),
    )


def _layer(x, mem, biases, fft_tables, wts, norm_pre, gate_bias, head_gain, mem_norm, norm_post):
    B, L, D = x.shape
    W = BRANCH_WIDTH
    h, h4, h16 = _prenorm(x, norm_pre)

    outs, lses = [], []
    for g, (d, hp) in enumerate(zip(DILATIONS, (h, h4, h16))):
        q, kv = _proj_a(hp.reshape(B * L, D), wts["a"][g])
        o, lse = _attention(q.reshape(B * d, L // d, W), kv.reshape(B * d, L // d, 2 * W), biases[g])
        outs.append(o.reshape(B, d, L // d, W))
        lses.append(lse.reshape(B, d, L // d, LANES))
    ya = _combine(*outs, *lses)

    kv_mem = _memkv(mem, mem_norm, wts["memkv"])
    xb, yd = _proj_bd(h, wts["bd"], kv_mem)
    yb = _fnet(xb, fft_tables)

    qm, km, v_t, oc_t, gates = _proj_c(h, wts["qk"], wts["vo_t"], wts["g"], wts["conv"], gate_bias.reshape(-1))
    T = MLSTM_CHUNK
    gates_t = jnp.swapaxes(gates.reshape(B, L // T, T, MLSTM_GATES), 2, 3)
    terms_fwd = _gate_terms(gates_t[:, :, :8], reverse=False)
    terms_bwd = _gate_terms(gates_t[:, :, 8:], reverse=True)
    gain = jnp.broadcast_to(head_gain.astype(F32)[:, None], (W, LANES))
    h_fwd_t = _mlstm_pass(qm, km, v_t, terms_fwd, None, reverse=False)
    yc = _mlstm_pass(qm, km, v_t, terms_bwd, (h_fwd_t, oc_t, gain), reverse=True)

    rows = lambda t: t.reshape(B * L, t.shape[-1])
    y = _merge(rows(x), rows(h), [rows(ya), rows(yb), rows(yc), rows(yd)],
               wts["gp"], wts["mg"], wts["br"], wts["out"], norm_post)
    return y.reshape(B, L, D)


def kernel(x_prompt, x_sample, mem_prompt, mem_sample, rel_bias, norm_pre, w_in, conv_qk, mlstm_gate_bias,
           mlstm_head_gain, mem_norm, w_mem_kv, w_branch, w_out, norm_post):
    depth = w_in.shape[0]
    biases = [_band_bias(rel_bias, g, d) for g, d in enumerate(DILATIONS)]
    layer_wts = [_split_weights(w_in[l], conv_qk[l], w_mem_kv[l], w_branch[l], w_out[l]) for l in range(depth)]
    outs = []
    for x, mem in ((x_prompt, mem_prompt), (x_sample, mem_sample)):
        tables = _fnet_tables(x.shape[1])
        for l in range(depth):
            x = _layer(x, mem, biases, tables, layer_wts[l], norm_pre[l], mlstm_gate_bias[l],
                       mlstm_head_gain[l], mem_norm[l], norm_post[l])
        outs.append(x)
    return tuple(outs)
```

```python
import functools
import math

import numpy as np
import jax
import jax.numpy as jnp
from jax import lax
from jax.experimental import pallas as pl
from jax.experimental.pallas import tpu as pltpu

D_MODEL = 1024
BRANCH_WIDTH = 512
HEAD_DIM = 128
N_HEADS = 4
N_BRANCHES = 4
DILATIONS = (1, 4, 16)
HALF_WINDOW = 64
A_QKV_WIDTH = len(DILATIONS) * BRANCH_WIDTH
REL_BUCKETS = 32
REL_MAX_DIST = 1024
FNET_GROUP_WIDTH = 128
MLSTM_CONV = 5
MLSTM_GATES = 16
N_MEM = 256
EPS = 1e-6
NEG = -1e30
LOG2_E = math.log2(math.e)

F32 = jnp.float32
BF16 = jnp.bfloat16
LANES = 128
LSE_LANES_PER_HEAD = LANES // N_HEADS

VMEM_LIMIT_BYTES = 56 * 1024 * 1024
PERM_TILE = 1024
ROW_TILE = 512
PROJ_TILE = 2048
ATTN_TILE = 2048
ATTN_Q_BLOCK = 128
MLSTM_CHUNK = 256
MLSTM_CHUNKS_PER_STEP = 4
FFT_ROWS = 8
CONV_HALO = 16
CONV_COLS = 256


def _params(*sem):
    return pltpu.CompilerParams(dimension_semantics=sem, vmem_limit_bytes=VMEM_LIMIT_BYTES)


def _resident(shape):
    nd = len(shape)
    return pl.BlockSpec(shape, lambda *_: (0,) * nd, pipeline_mode=pl.Buffered(1))


def _sigmoid(x):
    return 0.5 * jnp.tanh(0.5 * x) + 0.5


def _log_sigmoid(x):
    return jnp.minimum(x, 0.0) - jnp.log(1.0 + jnp.exp(-jnp.abs(x)))


def _dot(a, b):
    return jnp.dot(a, b, preferred_element_type=F32)


def _dot_nt(a, b):
    return lax.dot_general(a, b, (((1,), (1,)), ((), ())), preferred_element_type=F32)


def _dot_tn(a, b):
    return lax.dot_general(a, b, (((0,), (0,)), ((), ())), preferred_element_type=F32)


def _prenorm_kernel(x_ref, g_ref, h_ref, h4_ref, h16_ref, xs_ref):
    T = PERM_TILE
    x = x_ref[0]
    ms = jnp.mean(x * x, axis=-1, keepdims=True)
    xn = x * lax.rsqrt(ms + EPS) * g_ref[...]
    h_ref[0] = xn.astype(BF16)
    for c in range(D_MODEL // LANES):
        xs_ref[c] = xn[:, c * LANES:(c + 1) * LANES]
    for d, out_ref in ((4, h4_ref), (16, h16_ref)):
        for r in range(d):
            for c in range(D_MODEL // LANES):
                out_ref[0, r, :, c * LANES:(c + 1) * LANES] = xs_ref[c, pl.ds(r, T // d, stride=d), :].astype(BF16)


def _prenorm(x, gain):
    B, L, D = x.shape
    T = PERM_TILE
    return pl.pallas_call(
        _prenorm_kernel,
        name="prenorm",
        grid=(B, L // T),
        in_specs=[pl.BlockSpec((1, T, D), lambda b, i: (b, i, 0)),
                  pl.BlockSpec((1, D), lambda b, i: (0, 0))],
        out_specs=[pl.BlockSpec((1, T, D), lambda b, i: (b, i, 0)),
                   pl.BlockSpec((1, 4, T // 4, D), lambda b, i: (b, 0, i, 0)),
                   pl.BlockSpec((1, 16, T // 16, D), lambda b, i: (b, 0, i, 0))],
        out_shape=[jax.ShapeDtypeStruct((B, L, D), BF16),
                   jax.ShapeDtypeStruct((B, 4, L // 4, D), BF16),
                   jax.ShapeDtypeStruct((B, 16, L // 16, D), BF16)],
        scratch_shapes=[pltpu.VMEM((D // LANES, T, LANES), F32)],
        compiler_params=_params("parallel", "parallel"),
    )(x, gain.reshape(1, D))


def _proj_a_kernel(h_ref, w_ref, q_ref, kv_ref):
    z = _dot(h_ref[...], w_ref[...])
    q_ref[...] = z[:, :BRANCH_WIDTH].astype(BF16)
    kv_ref[...] = z[:, BRANCH_WIDTH:].astype(BF16)


def _proj_a(h_rows, w_qkv):
    R, D = h_rows.shape
    T = PROJ_TILE
    W = BRANCH_WIDTH
    return pl.pallas_call(
        _proj_a_kernel,
        name="proj_a",
        grid=(R // T,),
        in_specs=[pl.BlockSpec((T, D), lambda i: (i, 0)), _resident((D, 3 * W))],
        out_specs=[pl.BlockSpec((T, W), lambda i: (i, 0)), pl.BlockSpec((T, 2 * W), lambda i: (i, 0))],
        out_shape=[jax.ShapeDtypeStruct((R, W), BF16), jax.ShapeDtypeStruct((R, 2 * W), BF16)],
        compiler_params=_params("parallel"),
    )(h_rows, w_qkv)


def _attn_kernel(q_ref, kv_ref, kvp_ref, kvn_ref, bias_ref, o_ref, lse_ref, win_ref, *, seq_len, tq):
    i = pl.program_id(1)
    hw, qb, W = HALF_WINDOW, ATTN_Q_BLOCK, BRANCH_WIDTH
    kb = qb + 2 * hw
    win_ref[0:hw] = kvp_ref[0]
    win_ref[hw:hw + tq] = kv_ref[0]
    win_ref[hw + tq:] = kvn_ref[0]
    scale = HEAD_DIM ** -0.5
    lane_head = lax.broadcasted_iota(jnp.int32, (qb, LANES), 1) // LSE_LANES_PER_HEAD
    n_blocks = tq // qb
    for j in range(n_blocks):
        a = j * qb
        kidx = i * tq + (a - hw) + lax.broadcasted_iota(jnp.int32, (1, kb), 1)
        valid = (kidx >= 0) & (kidx < seq_len) if j in (0, n_blocks - 1) else None
        lse_all = None
        for h in range(N_HEADS):
            c0, c1 = h * HEAD_DIM, (h + 1) * HEAD_DIM
            q = q_ref[0, a:a + qb, c0:c1]
            k = win_ref[a:a + kb, c0:c1]
            v = win_ref[a:a + kb, W + c0:W + c1]
            s = _dot_nt(q, k) + bias_ref[h]
            if valid is not None:
                s = jnp.where(valid, s, NEG)
            m = jnp.max(s, axis=-1, keepdims=True)
            p = jnp.exp2((s - m) * (scale * LOG2_E))
            l = jnp.sum(p, axis=-1, keepdims=True)
            o = _dot(p.astype(BF16), v) * (1.0 / l)
            o_ref[0, a:a + qb, c0:c1] = o.astype(BF16)
            lse = jnp.broadcast_to(m * scale + jnp.log(l), (qb, LANES))
            lse_all = lse if h == 0 else jnp.where(lane_head == h, lse, lse_all)
        lse_ref[0, a:a + qb, :] = lse_all


def _attention(q, kv, bias):
    S, M, W = q.shape
    hw = HALF_WINDOW
    tq = min(M, ATTN_TILE)
    nhb = M // hw
    step = tq // hw
    kern = functools.partial(_attn_kernel, seq_len=M, tq=tq)
    return pl.pallas_call(
        kern,
        name="attn",
        grid=(S, M // tq),
        in_specs=[pl.BlockSpec((1, tq, W), lambda s, i: (s, i, 0)),
                  pl.BlockSpec((1, tq, 2 * W), lambda s, i: (s, i, 0)),
                  pl.BlockSpec((1, hw, 2 * W), lambda s, i: (s, jnp.maximum(i * step - 1, 0), 0)),
                  pl.BlockSpec((1, hw, 2 * W), lambda s, i: (s, jnp.minimum((i + 1) * step, nhb - 1), 0)),
                  _resident(bias.shape)],
        out_specs=[pl.BlockSpec((1, tq, W), lambda s, i: (s, i, 0)),
                   pl.BlockSpec((1, tq, LANES), lambda s, i: (s, i, 0))],
        out_shape=[jax.ShapeDtypeStruct((S, M, W), BF16), jax.ShapeDtypeStruct((S, M, LANES), F32)],
        scratch_shapes=[pltpu.VMEM((tq + 2 * hw, 2 * W), BF16)],
        compiler_params=_params("parallel", "parallel"),
    )(q, kv, kv, kv, bias)


def _combine_kernel(o1_ref, o4_ref, o16_ref, l1_ref, l4_ref, l16_ref, ya_ref, os_ref, ls_ref):
    T, nc = PERM_TILE, BRANCH_WIDTH // LANES
    for g, (d, o_ref, l_ref) in enumerate(((4, o4_ref, l4_ref), (16, o16_ref, l16_ref))):
        for r in range(d):
            ls_ref[g, pl.ds(r, T // d, stride=d), :] = l_ref[0, r]
            for c in range(nc):
                os_ref[g, c, pl.ds(r, T // d, stride=d), :] = o_ref[0, r, :, c * LANES:(c + 1) * LANES].astype(F32)
    l1, l4, l16 = l1_ref[0, 0], ls_ref[0], ls_ref[1]
    m = jnp.maximum(jnp.maximum(l1, l4), l16)
    e1, e4, e16 = jnp.exp(l1 - m), jnp.exp(l4 - m), jnp.exp(l16 - m)
    inv = 1.0 / (e1 + e4 + e16)
    w1, w4, w16 = e1 * inv, e4 * inv, e16 * inv
    for h in range(N_HEADS):
        c0, c1 = h * HEAD_DIM, (h + 1) * HEAD_DIM
        lh = h * LSE_LANES_PER_HEAD
        ya = (w1[:, lh:lh + 1] * o1_ref[0, 0, :, c0:c1].astype(F32)
              + w4[:, lh:lh + 1] * os_ref[0, h] + w16[:, lh:lh + 1] * os_ref[1, h])
        ya_ref[0, :, c0:c1] = ya.astype(BF16)


def _combine(o1, o4, o16, l1, l4, l16):
    B, _, L, W = o1.shape
    T = PERM_TILE

    def spec(d, c):
        return pl.BlockSpec((1, d, T // d, c), lambda b, i: (b, 0, i, 0))

    return pl.pallas_call(
        _combine_kernel,
        name="combine",
        grid=(B, L // T),
        in_specs=[spec(1, W), spec(4, W), spec(16, W), spec(1, LANES), spec(4, LANES), spec(16, LANES)],
        out_specs=pl.BlockSpec((1, T, W), lambda b, i: (b, i, 0)),
        out_shape=jax.ShapeDtypeStruct((B, L, W), BF16),
        scratch_shapes=[pltpu.VMEM((2, W // LANES, T, LANES), F32), pltpu.VMEM((2, T, LANES), F32)],
        compiler_params=_params("parallel", "parallel"),
    )(o1, o4, o16, l1, l4, l16)


def _memkv_kernel(mem_ref, g_ref, w_ref, kv_ref):
    x = mem_ref[0]
    ms = jnp.mean(x * x, axis=-1, keepdims=True)
    xn = (x * lax.rsqrt(ms + EPS) * g_ref[...]).astype(BF16)
    kv_ref[0] = _dot(xn, w_ref[...]).astype(BF16)


def _memkv(mem, gain, w):
    B, N, D = mem.shape
    W2 = w.shape[1]
    return pl.pallas_call(
        _memkv_kernel,
        name="memkv",
        grid=(B,),
        in_specs=[pl.BlockSpec((1, N, D), lambda b: (b, 0, 0)), pl.BlockSpec((1, D), lambda b: (0, 0)),
                  _resident((D, W2))],
        out_specs=pl.BlockSpec((1, N, W2), lambda b: (b, 0, 0)),
        out_shape=jax.ShapeDtypeStruct((B, N, W2), BF16),
        compiler_params=_params("parallel"),
    )(mem, gain.reshape(1, D), w)


def _proj_bd_kernel(h_ref, w_ref, kv_ref, xb_ref, yd_ref):
    W = BRANCH_WIDTH
    z = _dot(h_ref[0], w_ref[...])
    xb_ref[0] = z[:, :W]
    qd = z[:, W:].astype(BF16)
    scale = HEAD_DIM ** -0.5
    for h in range(N_HEADS):
        c0, c1 = h * HEAD_DIM, (h + 1) * HEAD_DIM
        s = _dot_nt(qd[:, c0:c1], kv_ref[0, :, c0:c1])
        m = jnp.max(s, axis=-1, keepdims=True)
        p = jnp.exp2((s - m) * (scale * LOG2_E))
        l = jnp.sum(p, axis=-1, keepdims=True)
        o = _dot(p.astype(BF16), kv_ref[0, :, W + c0:W + c1]) * (1.0 / l)
        yd_ref[0, :, c0:c1] = o.astype(BF16)


def _proj_bd(h, w_bd, kv):
    B, L, D = h.shape
    T, W = PROJ_TILE, BRANCH_WIDTH
    return pl.pallas_call(
        _proj_bd_kernel,
        name="proj_bd",
        grid=(B, L // T),
        in_specs=[pl.BlockSpec((1, T, D), lambda b, i: (b, i, 0)), _resident((D, 2 * W)),
                  pl.BlockSpec((1, N_MEM, 2 * W), lambda b, i: (b, 0, 0))],
        out_specs=[pl.BlockSpec((1, T, W), lambda b, i: (b, i, 0)), pl.BlockSpec((1, T, W), lambda b, i: (b, i, 0))],
        out_shape=[jax.ShapeDtypeStruct((B, L, W), F32), jax.ShapeDtypeStruct((B, L, W), BF16)],
        compiler_params=_params("parallel", "parallel"),
    )(h, w_bd, kv)


def _fft_factors(L):
    n1 = 1 << ((L.bit_length() - 1 + 1) // 2)
    return n1, L // n1


def _dft_cos_sin(n):
    ang = 2.0 * np.pi * ((np.arange(n)[:, None] * np.arange(n)[None, :]) % n) / n
    return np.cos(ang), np.sin(ang)


def _fft1_kernel(x_ref, f_ref, tc_ref, ts_ref, yr_ref, yi_ref, xs_ref, *, n1):
    W = BRANCH_WIDTH
    for j in range(FFT_ROWS):
        xs_ref[:, j * W:(j + 1) * W] = x_ref[0, :, j, :]
    y = _dot(f_ref[...], xs_ref[...].astype(BF16))
    reps = W // FNET_GROUP_WIDTH
    for j in range(FFT_ROWS):
        yr, yi = y[:n1, j * W:(j + 1) * W], y[n1:, j * W:(j + 1) * W]
        tc = jnp.concatenate([tc_ref[j]] * reps, axis=1)
        ts = jnp.concatenate([ts_ref[j]] * reps, axis=1)
        yr_ref[0, j] = yr * tc + yi * ts
        yi_ref[0, j] = yi * tc - yr * ts


def _fft2_kernel(yr_ref, yi_ref, g1_ref, g2_ref, cs_ref, out_ref, ys_ref, *, n2, norm):
    W, gw = BRANCH_WIDTH, FNET_GROUP_WIDTH
    for j in range(FFT_ROWS):
        ys_ref[0, :, j * W:(j + 1) * W] = yr_ref[0, :, j, :]
        ys_ref[1, :, j * W:(j + 1) * W] = yi_ref[0, :, j, :]
    u = _dot(g1_ref[...], ys_ref[0].astype(BF16)) + _dot(g2_ref[...], ys_ref[1].astype(BF16))
    top, bot = u[:n2].astype(BF16), u[n2:].astype(BF16)
    ng = FFT_ROWS * W // gw
    lhs = jnp.concatenate(
        [jnp.concatenate([top[:, j * gw:(j + 1) * gw], bot[:, j * gw:(j + 1) * gw]], axis=1) for j in range(ng)],
        axis=0)
    res = _dot(lhs, cs_ref[...]) * norm
    for j in range(ng):
        k1, c = divmod(j, W // gw)
        out_ref[0, :, k1, c * gw:(c + 1) * gw] = res[j * n2:(j + 1) * n2]


def _fnet(xb, tables):
    B, L, W = xb.shape
    n1, n2 = _fft_factors(L)
    f1, twc, tws, g1, g2, cs = tables
    R = FFT_ROWS
    yr, yi = pl.pallas_call(
        functools.partial(_fft1_kernel, n1=n1),
        name="fft1",
        grid=(B, n2 // R),
        in_specs=[pl.BlockSpec((1, n1, R, W), lambda b, j: (b, 0, j, 0)), _resident((2 * n1, n1)),
                  pl.BlockSpec((R, n1, FNET_GROUP_WIDTH), lambda b, j: (j, 0, 0)),
                  pl.BlockSpec((R, n1, FNET_GROUP_WIDTH), lambda b, j: (j, 0, 0))],
        out_specs=[pl.BlockSpec((1, R, n1, W), lambda b, j: (b, j, 0, 0))] * 2,
        out_shape=[jax.ShapeDtypeStruct((B, n2, n1, W), F32)] * 2,
        scratch_shapes=[pltpu.VMEM((n1, R * W), F32)],
        compiler_params=_params("parallel", "parallel"),
    )(xb.reshape(B, n1, n2, W), f1, twc, tws)
    out = pl.pallas_call(
        functools.partial(_fft2_kernel, n2=n2, norm=float(1.0 / math.sqrt(L * FNET_GROUP_WIDTH))),
        name="fft2",
        grid=(B, n1 // R),
        in_specs=[pl.BlockSpec((1, n2, R, W), lambda b, j: (b, 0, j, 0))] * 2
        + [_resident((2 * n2, n2)), _resident((2 * n2, n2)), _resident((2 * FNET_GROUP_WIDTH, FNET_GROUP_WIDTH))],
        out_specs=pl.BlockSpec((1, n2, R, W), lambda b, j: (b, 0, j, 0)),
        out_shape=jax.ShapeDtypeStruct((B, n2, n1, W), F32),
        scratch_shapes=[pltpu.VMEM((2, n2, R * W), F32)],
        compiler_params=_params("parallel", "parallel"),
    )(yr, yi, g1, g2, cs)
    return out.reshape(B, L, W)


def _fnet_tables(L):
    n1, n2 = _fft_factors(L)
    c1, s1 = _dft_cos_sin(n1)
    c2, s2 = _dft_cos_sin(n2)
    cc, sc = _dft_cos_sin(FNET_GROUP_WIDTH)
    ang = 2.0 * np.pi * (np.arange(n2)[:, None] * np.arange(n1)[None, :]) / L
    bcast = lambda t: jnp.asarray(np.broadcast_to(t[:, :, None], (n2, n1, FNET_GROUP_WIDTH)), F32)
    return (jnp.asarray(np.concatenate([c1, -s1], axis=0), BF16), bcast(np.cos(ang)), bcast(np.sin(ang)),
            jnp.asarray(np.concatenate([c2, -s2], axis=0), BF16), jnp.asarray(np.concatenate([s2, c2], axis=0), BF16),
            jnp.asarray(np.concatenate([cc, sc], axis=0), BF16))


def _proj_c_kernel(h_ref, hp_ref, hn_ref, wqk_ref, wvo_ref, wg_ref, conv_ref, gb_ref,
                   q_ref, k_ref, v_ref, o_ref, g_ref):
    i = pl.program_id(1)
    T, W, halo = PROJ_TILE, BRANCH_WIDTH, CONV_HALO
    h = h_ref[0]
    has_prev = (i > 0).astype(F32)
    has_next = (i < pl.num_programs(1) - 1).astype(F32)
    pad = MLSTM_CONV // 2
    n_ext = T + 2 * halo
    for c in range(2 * W // CONV_COLS):
        cs = slice(c * CONV_COLS, (c + 1) * CONV_COLS)
        w = wqk_ref[:, cs]
        ext = jnp.concatenate([_dot(hp_ref[0], w) * has_prev, _dot(h, w), _dot(hn_ref[0], w) * has_next], axis=0)
        acc = ext[halo:halo + T] * conv_ref[pad:pad + 1, cs]
        for j in range(MLSTM_CONV):
            if j != pad:
                acc = acc + pltpu.roll(ext, (pad - j) % n_ext, 0)[halo:halo + T] * conv_ref[j:j + 1, cs]
        y = acc * (jnp.tanh(acc) + 1.0)
        if c < W // CONV_COLS:
            q_ref[0, :, cs] = (y * (HEAD_DIM ** -0.5)).astype(BF16)
        else:
            k_ref[0, :, c * CONV_COLS - W:(c + 1) * CONV_COLS - W] = y.astype(BF16)
    vo_t = _dot_nt(wvo_ref[...], h)
    v_ref[0] = vo_t[:W].astype(BF16)
    o_ref[0] = vo_t[W:].astype(BF16)
    g_ref[0] = _dot(h, wg_ref[...]) + gb_ref[...]


def _proj_c(h, w_qk, w_vo_t, w_g, conv, gate_bias):
    B, L, D = h.shape
    T, W, halo = PROJ_TILE, BRANCH_WIDTH, CONV_HALO
    step = T // halo
    nhb = L // halo
    tile = lambda c: pl.BlockSpec((1, T, c), lambda b, i: (b, i, 0))
    tile_t = pl.BlockSpec((1, W, T), lambda b, i: (b, 0, i))
    rows, cols = jax.ShapeDtypeStruct((B, L, W), BF16), jax.ShapeDtypeStruct((B, W, L), BF16)
    return pl.pallas_call(
        _proj_c_kernel,
        name="proj_c",
        grid=(B, L // T),
        in_specs=[tile(D),
                  pl.BlockSpec((1, halo, D), lambda b, i: (b, jnp.maximum(i * step - 1, 0), 0)),
                  pl.BlockSpec((1, halo, D), lambda b, i: (b, jnp.minimum((i + 1) * step, nhb - 1), 0)),
                  _resident((D, 2 * W)), _resident((2 * W, D)), _resident((D, MLSTM_GATES)),
                  _resident((MLSTM_CONV, 2 * W)), _resident((1, MLSTM_GATES))],
        out_specs=[tile(W), tile(W), tile_t, tile_t, tile(MLSTM_GATES)],
        out_shape=[rows, rows, cols, cols, jax.ShapeDtypeStruct((B, L, MLSTM_GATES), F32)],
        compiler_params=_params("parallel", "arbitrary"),
    )(h, h, h, w_qk, w_vo_t, w_g, conv, gate_bias.reshape(1, MLSTM_GATES))


GATE_U, GATE_M, GATE_W_INTER, GATE_FLOOR, GATE_WA, GATE_S_OLD = range(6)
N_GATE_TERMS = 6
STATE_ROWS = HEAD_DIM + 16


def _gate_terms_kernel(x_ref, out_ref, g_ref, pe_ref, mp_ref, *, reverse):
    nc, T, H = x_ref.shape[1], MLSTM_CHUNK, N_HEADS
    x = x_ref[0].reshape(nc * 8, T)
    sub = lax.broadcasted_iota(jnp.int32, (nc * 8, T), 0) % 8
    lane = lax.broadcasted_iota(jnp.int32, (nc * 8, T), 1)
    src = lax.broadcasted_iota(jnp.int32, (T, T), 0)
    tgt = lax.broadcasted_iota(jnp.int32, (T, T), 1)
    tri = ((src >= tgt) if reverse else (src <= tgt)).astype(F32)
    cs = jnp.dot(jnp.where(sub >= H, _log_sigmoid(x), x), tri,
                 preferred_element_type=F32, precision=lax.Precision.HIGHEST)
    b = pltpu.roll(cs, nc * 8 - H, 0)
    u = x - b
    p = u
    shift = 1
    while shift < T:
        if reverse:
            p = jnp.maximum(p, jnp.where(lane < T - shift, pltpu.roll(p, T - shift, 1), NEG))
        else:
            p = jnp.maximum(p, jnp.where(lane >= shift, pltpu.roll(p, shift, 1), NEG))
        shift *= 2
    last = 0 if reverse else T - 1
    g = jnp.broadcast_to(b[:, last:last + 1], (nc * 8, T))
    p_end = jnp.broadcast_to(p[:, last:last + 1], (nc * 8, T))
    g_ref[...] = g[:, :LANES].reshape(nc, 8, LANES)
    pe_ref[...] = p_end[:, :LANES].reshape(nc, 8, LANES)

    def step(c, m):
        cc = nc - 1 - c if reverse else c
        mp_ref[cc] = m
        return g_ref[cc] + jnp.maximum(m, pe_ref[cc])

    lax.fori_loop(0, nc, step, jnp.zeros((8, LANES), F32))
    m = jnp.concatenate([mp_ref[...].reshape(nc * 8, LANES)] * (T // LANES), axis=1)
    big_m = jnp.maximum(m, p)
    m_next = g + jnp.maximum(m, p_end)
    terms = {GATE_U: u, GATE_M: big_m, GATE_W_INTER: jnp.exp(m - big_m), GATE_FLOOR: jnp.exp(-(b + big_m)),
             GATE_WA: jnp.exp(g + u - m_next), GATE_S_OLD: jnp.exp(g + m - m_next)}
    for j, val in terms.items():
        out_ref[0, :, j] = jnp.where(sub < H, val, 0.0).reshape(nc, 8, T)


def _gate_terms(gates_dir, *, reverse):
    B, nc, _, T = gates_dir.shape
    return pl.pallas_call(
        functools.partial(_gate_terms_kernel, reverse=reverse),
        name="gate_terms",
        grid=(B,),
        in_specs=[pl.BlockSpec((1, nc, 8, T), lambda b: (b, 0, 0, 0))],
        out_specs=pl.BlockSpec((1, nc, N_GATE_TERMS, 8, T), lambda b: (b, 0, 0, 0, 0)),
        out_shape=jax.ShapeDtypeStruct((B, nc, N_GATE_TERMS, 8, T), F32),
        scratch_shapes=[pltpu.VMEM((nc, 8, LANES), F32)] * 3,
        compiler_params=_params("parallel"),
    )(gates_dir)


def _mlstm_kernel(*refs, reverse, finalize):
    if finalize:
        (q_ref, k_ref, v_ref, r_ref, hf_ref, oc_ref, gain_ref, out_ref, c_ref) = refs
    else:
        (q_ref, k_ref, v_ref, r_ref, out_ref, c_ref) = refs
    T, dh, H = MLSTM_CHUNK, HEAD_DIM, N_HEADS

    @pl.when(pl.program_id(1) == 0)
    def _():
        c_ref[...] = jnp.zeros_like(c_ref)

    src = lax.broadcasted_iota(jnp.int32, (T, T), 0)
    tgt = lax.broadcasted_iota(jnp.int32, (T, T), 1)
    feeds = (src >= tgt) if reverse else (src <= tgt)
    sub = lax.broadcasted_iota(jnp.int32, (STATE_ROWS - dh, T), 0)
    if finalize:
        eye = (src == tgt).astype(BF16)

    chunks = range(MLSTM_CHUNKS_PER_STEP)
    for ci in (reversed(chunks) if reverse else chunks):
        t0, t1 = ci * T, (ci + 1) * T
        u_cols = jnp.transpose(r_ref[0, ci, GATE_U])
        for h in range(H):
            c0, c1 = h * dh, (h + 1) * dh
            q, k, v_t = q_ref[0, t0:t1, c0:c1], k_ref[0, t0:t1, c0:c1], v_ref[0, c0:c1, t0:t1]
            row = lambda j: r_ref[0, ci, j, h:h + 1, :]
            w_inter, wa = row(GATE_W_INTER), row(GATE_WA)
            state = c_ref[h]

            logw = jnp.where(feeds, u_cols[:, h:h + 1] - row(GATE_M), NEG)
            sqk_t = jnp.exp(logw) * _dot_nt(k, q)
            inter = _dot_nt(state.astype(BF16), q)
            num_t = w_inter * inter[:dh] + _dot(v_t, sqk_t.astype(BF16))
            den = w_inter * inter[dh:dh + 1] + jnp.sum(sqk_t, axis=0, keepdims=True)
            hout_t = num_t * (1.0 / jnp.maximum(jnp.abs(den), row(GATE_FLOOR)))

            vw = jnp.concatenate([(v_t.astype(F32) * wa).astype(BF16),
                                  jnp.where(sub == 0, wa, 0.0).astype(BF16)], axis=0)
            c_ref[h] = row(GATE_S_OLD)[:, :dh] * state + _dot(vw, k)

            if finalize:
                hc = _sigmoid(oc_ref[0, c0:c1, t0:t1].astype(F32)) * (hout_t + hf_ref[0, c0:c1, t0:t1])
                mu = jnp.mean(hc, axis=0, keepdims=True)
                cen = hc - mu
                var = jnp.mean(cen * cen, axis=0, keepdims=True)
                gain = jnp.concatenate([gain_ref[c0:c1]] * (T // LANES), axis=1)
                yc_t = cen * lax.rsqrt(var + EPS) * gain
                out_ref[0, t0:t1, c0:c1] = jnp.transpose(yc_t).astype(BF16)
            else:
                out_ref[0, c0:c1, t0:t1] = hout_t


def _mlstm_pass(q, k, v_t, terms, extra, *, reverse):
    B, L, W = k.shape
    cps = MLSTM_CHUNKS_PER_STEP
    T = MLSTM_CHUNK * cps
    n = L // T
    pos = (lambda i: n - 1 - i) if reverse else (lambda i: i)
    rows = pl.BlockSpec((1, T, W), lambda b, i: (b, pos(i), 0))
    cols = pl.BlockSpec((1, W, T), lambda b, i: (b, 0, pos(i)))
    in_specs = [rows, rows, cols,
                pl.BlockSpec((1, cps, N_GATE_TERMS, 8, MLSTM_CHUNK), lambda b, i: (b, pos(i), 0, 0, 0))]
    finalize = extra is not None
    args = [q, k, v_t, terms]
    if finalize:
        in_specs += [cols, cols, _resident((W, LANES))]
        args += list(extra)
    return pl.pallas_call(
        functools.partial(_mlstm_kernel, reverse=reverse, finalize=finalize),
        name="mlstm_bwd" if reverse else "mlstm_fwd",
        grid=(B, n),
        in_specs=in_specs,
        out_specs=rows if finalize else cols,
        out_shape=jax.ShapeDtypeStruct((B, L, W), BF16) if finalize else jax.ShapeDtypeStruct((B, W, L), F32),
        scratch_shapes=[pltpu.VMEM((N_HEADS, STATE_ROWS, HEAD_DIM), F32)],
        compiler_params=_params("parallel", "arbitrary"),
    )(*args)


def _merge_kernel(x_ref, h_ref, ya_ref, yb_ref, yc_ref, yd_ref, wgp_ref, wmg_ref, wbr_ref, wout_ref, g_ref, out_ref):
    W, D = BRANCH_WIDTH, D_MODEL
    h = h_ref[...]
    acc = None
    for g, br_ref in enumerate((ya_ref, yb_ref, yc_ref, yd_ref)):
        gp_half = _dot(h, wgp_ref[:, g * W:(g + 1) * W])
        gated = (br_ref[...].astype(F32) * (gp_half * (jnp.tanh(gp_half) + 1.0))).astype(BF16)
        proj_half = _dot(gated, wbr_ref[g])
        term = (jnp.tanh(_dot(h, wmg_ref[:, g * D:(g + 1) * D])) + 1.0) * proj_half
        acc = term if acc is None else acc + term
    out = _dot(acc.astype(BF16), wout_ref[...])
    ms = jnp.mean(out * out, axis=-1, keepdims=True)
    out_ref[...] = x_ref[...] + out * lax.rsqrt(ms + EPS) * g_ref[...]


def _merge(x_rows, h_rows, branches, w_gp, w_mg, w_br, w_out, gain):
    R, D = x_rows.shape
    T, W = ROW_TILE, BRANCH_WIDTH
    rows = lambda c: pl.BlockSpec((T, c), lambda i: (i, 0))
    return pl.pallas_call(
        _merge_kernel,
        name="merge",
        grid=(R // T,),
        in_specs=[rows(D), rows(D)] + [rows(W)] * N_BRANCHES
        + [_resident(w_gp.shape), _resident(w_mg.shape), _resident(w_br.shape), _resident(w_out.shape),
           _resident((1, D))],
        out_specs=rows(D),
        out_shape=jax.ShapeDtypeStruct((R, D), F32),
        compiler_params=_params("parallel"),
    )(x_rows, h_rows, *branches, w_gp, w_mg, w_br, w_out, gain.reshape(1, D))


def _t5_bucket(rel):
    nb = REL_BUCKETS // 2
    max_exact = nb // 2
    ret = (rel > 0).astype(np.int32) * nb
    n = np.abs(rel)
    large = max_exact + (np.log(np.maximum(n, 1) / max_exact) / np.log(REL_MAX_DIST / max_exact)
                         * (nb - max_exact)).astype(np.int32)
    large = np.minimum(large, nb - 1)
    return (ret + np.where(n < max_exact, n, large)).astype(np.int32)


def _band_bias(rel_bias, group, dilation):
    qb, hw = ATTN_Q_BLOCK, HALF_WINDOW
    rel = np.arange(qb + 2 * hw)[None, :] - hw - np.arange(qb)[:, None]
    heads = rel_bias[:, group * N_HEADS:(group + 1) * N_HEADS].astype(F32)
    onehot = (jnp.asarray(_t5_bucket(dilation * rel))[None, :, :] == jnp.arange(REL_BUCKETS)[:, None, None])
    bias = jnp.sum(jnp.where(onehot[:, None], heads[:, :, None, None], 0.0), axis=0)
    return jnp.where(jnp.asarray(np.abs(rel) <= hw)[None], bias * (HEAD_DIM ** 0.5), NEG)


def _split_weights(w_in, conv_qk, w_mem_kv, w_branch, w_out):
    W, D, A = BRANCH_WIDTH, D_MODEL, A_QKV_WIDTH
    off = np.cumsum([0, A, A, A, W, W, W, W, W, MLSTM_GATES, W, N_BRANCHES * W, N_BRANCHES * D])
    seg = lambda j: w_in[:, off[j]:off[j + 1]]
    qa, ka, va = seg(0), seg(1), seg(2)
    w_a = [jnp.concatenate([t[:, g * W:(g + 1) * W] for t in (qa, ka, va)], axis=1).astype(BF16)
           for g in range(len(DILATIONS))]
    return dict(
        a=w_a,
        bd=jnp.concatenate([seg(3), seg(9)], axis=1).astype(BF16),
        qk=jnp.concatenate([seg(4), seg(5)], axis=1).astype(BF16),
        vo_t=jnp.concatenate([seg(6), seg(7)], axis=1).T.astype(BF16),
        g=seg(8).astype(BF16),
        gp=(0.5 * seg(10)).astype(BF16),
        mg=(0.5 * seg(11)).astype(BF16),
        conv=0.5 * conv_qk.astype(F32),
        memkv=w_mem_kv.astype(BF16),
        br=(0.5 * w_branch).astype(BF16),
        out=w_out.astype(BF16),
    )


def _layer(x, mem, biases, fft_tables, wts, norm_pre, gate_bias, head_gain, mem_norm, norm_post):
    B, L, D = x.shape
    W = BRANCH_WIDTH
    h, h4, h16 = _prenorm(x, norm_pre)

    outs, lses = [], []
    for g, (d, hp) in enumerate(zip(DILATIONS, (h, h4, h16))):
        q, kv = _proj_a(hp.reshape(B * L, D), wts["a"][g])
        o, lse = _attention(q.reshape(B * d, L // d, W), kv.reshape(B * d, L // d, 2 * W), biases[g])
        outs.append(o.reshape(B, d, L // d, W))
        lses.append(lse.reshape(B, d, L // d, LANES))
    ya = _combine(*outs, *lses)

    kv_mem = _memkv(mem, mem_norm, wts["memkv"])
    xb, yd = _proj_bd(h, wts["bd"], kv_mem)
    yb = _fnet(xb, fft_tables)

    qm, km, v_t, oc_t, gates = _proj_c(h, wts["qk"], wts["vo_t"], wts["g"], wts["conv"], gate_bias.reshape(-1))
    T = MLSTM_CHUNK
    gates_t = jnp.swapaxes(gates.reshape(B, L // T, T, MLSTM_GATES), 2, 3)
    terms_fwd = _gate_terms(gates_t[:, :, :8], reverse=False)
    terms_bwd = _gate_terms(gates_t[:, :, 8:], reverse=True)
    gain = jnp.broadcast_to(head_gain.astype(F32)[:, None], (W, LANES))
    h_fwd_t = _mlstm_pass(qm, km, v_t, terms_fwd, None, reverse=False)
    yc = _mlstm_pass(qm, km, v_t, terms_bwd, (h_fwd_t, oc_t, gain), reverse=True)

    rows = lambda t: t.reshape(B * L, t.shape[-1])
    y = _merge(rows(x), rows(h), [rows(ya), rows(yb), rows(yc), rows(yd)],
               wts["gp"], wts["mg"], wts["br"], wts["out"], norm_post)
    return y.reshape(B, L, D)


def kernel(x_prompt, x_sample, mem_prompt, mem_sample, rel_bias, norm_pre, w_in, conv_qk, mlstm_gate_bias,
           mlstm_head_gain, mem_norm, w_mem_kv, w_branch, w_out, norm_post):
    depth = w_in.shape[0]
    biases = [_band_bias(rel_bias, g, d) for g, d in enumerate(DILATIONS)]
    layer_wts = [_split_weights(w_in[l], conv_qk[l], w_mem_kv[l], w_branch[l], w_out[l]) for l in range(depth)]
    outs = []
    for x, mem in ((x_prompt, mem_prompt), (x_sample, mem_sample)):
        tables = _fnet_tables(x.shape[1])
        for l in range(depth):
            x = _layer(x, mem, biases, tables, layer_wts[l], norm_pre[l], mlstm_gate_bias[l],
                       mlstm_head_gain[l], mem_norm[l], norm_post[l])
        outs.append(x)
    return tuple(outs)
```

```python
import functools
import math

import numpy as np
import jax
import jax.numpy as jnp
from jax import lax
from jax.experimental import pallas as pl
from jax.experimental.pallas import tpu as pltpu

D_MODEL = 1024
BRANCH_WIDTH = 512
HEAD_DIM = 128
N_HEADS = 4
N_BRANCHES = 4
DILATIONS = (1, 4, 16)
HALF_WINDOW = 64
A_QKV_WIDTH = len(DILATIONS) * BRANCH_WIDTH
REL_BUCKETS = 32
REL_MAX_DIST = 1024
FNET_GROUP_WIDTH = 128
MLSTM_CONV = 5
MLSTM_GATES = 16
N_MEM = 256
EPS = 1e-6
NEG = -1e30
LOG2_E = math.log2(math.e)

F32 = jnp.float32
BF16 = jnp.bfloat16
LANES = 128
LSE_LANES_PER_HEAD = LANES // N_HEADS

VMEM_LIMIT_BYTES = 56 * 1024 * 1024
PERM_TILE = 1024
ROW_TILE = 512
PROJ_TILE = 2048
ATTN_TILE = 2048
ATTN_Q_BLOCK = 128
MLSTM_CHUNK = 256
MLSTM_CHUNKS_PER_STEP = 4
FFT_ROWS = 8
CONV_HALO = 16
CONV_COLS = 256


def _params(*sem):
    return pltpu.CompilerParams(dimension_semantics=sem, vmem_limit_bytes=VMEM_LIMIT_BYTES)


def _resident(shape):
    nd = len(shape)
    return pl.BlockSpec(shape, lambda *_: (0,) * nd, pipeline_mode=pl.Buffered(1))


def _sigmoid(x):
    return 0.5 * jnp.tanh(0.5 * x) + 0.5


def _log_sigmoid(x):
    return jnp.minimum(x, 0.0) - jnp.log(1.0 + jnp.exp(-jnp.abs(x)))


def _dot(a, b):
    return jnp.dot(a, b, preferred_element_type=F32)


def _dot_nt(a, b):
    return lax.dot_general(a, b, (((1,), (1,)), ((), ())), preferred_element_type=F32)


def _dot_tn(a, b):
    return lax.dot_general(a, b, (((0,), (0,)), ((), ())), preferred_element_type=F32)


def _prenorm_kernel(x_ref, g_ref, h_ref, h4_ref, h16_ref, xs_ref):
    T = PERM_TILE
    x = x_ref[0]
    ms = jnp.mean(x * x, axis=-1, keepdims=True)
    xn = x * lax.rsqrt(ms + EPS) * g_ref[...]
    h_ref[0] = xn.astype(BF16)
    for c in range(D_MODEL // LANES):
        xs_ref[c] = xn[:, c * LANES:(c + 1) * LANES]
    for d, out_ref in ((4, h4_ref), (16, h16_ref)):
        for r in range(d):
            for c in range(D_MODEL // LANES):
                out_ref[0, r, :, c * LANES:(c + 1) * LANES] = xs_ref[c, pl.ds(r, T // d, stride=d), :].astype(BF16)


def _prenorm(x, gain):
    B, L, D = x.shape
    T = PERM_TILE
    return pl.pallas_call(
        _prenorm_kernel,
        name="prenorm",
        grid=(B, L // T),
        in_specs=[pl.BlockSpec((1, T, D), lambda b, i: (b, i, 0)),
                  pl.BlockSpec((1, D), lambda b, i: (0, 0))],
        out_specs=[pl.BlockSpec((1, T, D), lambda b, i: (b, i, 0)),
                   pl.BlockSpec((1, 4, T // 4, D), lambda b, i: (b, 0, i, 0)),
                   pl.BlockSpec((1, 16, T // 16, D), lambda b, i: (b, 0, i, 0))],
        out_shape=[jax.ShapeDtypeStruct((B, L, D), BF16),
                   jax.ShapeDtypeStruct((B, 4, L // 4, D), BF16),
                   jax.ShapeDtypeStruct((B, 16, L // 16, D), BF16)],
        scratch_shapes=[pltpu.VMEM((D // LANES, T, LANES), F32)],
        compiler_params=_params("parallel", "parallel"),
    )(x, gain.reshape(1, D))


def _proj_a_kernel(h_ref, w_ref, q_ref, kv_ref):
    z = _dot(h_ref[...], w_ref[...])
    q_ref[...] = z[:, :BRANCH_WIDTH].astype(BF16)
    kv_ref[...] = z[:, BRANCH_WIDTH:].astype(BF16)


def _proj_a(h_rows, w_qkv):
    R, D = h_rows.shape
    T = PROJ_TILE
    W = BRANCH_WIDTH
    return pl.pallas_call(
        _proj_a_kernel,
        name="proj_a",
        grid=(R // T,),
        in_specs=[pl.BlockSpec((T, D), lambda i: (i, 0)), _resident((D, 3 * W))],
        out_specs=[pl.BlockSpec((T, W), lambda i: (i, 0)), pl.BlockSpec((T, 2 * W), lambda i: (i, 0))],
        out_shape=[jax.ShapeDtypeStruct((R, W), BF16), jax.ShapeDtypeStruct((R, 2 * W), BF16)],
        compiler_params=_params("parallel"),
    )(h_rows, w_qkv)


def _attn_kernel(q_ref, kv_ref, kvp_ref, kvn_ref, bias_ref, o_ref, lse_ref, win_ref, *, seq_len, tq):
    i = pl.program_id(1)
    hw, qb, W = HALF_WINDOW, ATTN_Q_BLOCK, BRANCH_WIDTH
    kb = qb + 2 * hw
    win_ref[0:hw] = kvp_ref[0]
    win_ref[hw:hw + tq] = kv_ref[0]
    win_ref[hw + tq:] = kvn_ref[0]
    scale = HEAD_DIM ** -0.5
    lane_head = lax.broadcasted_iota(jnp.int32, (qb, LANES), 1) // LSE_LANES_PER_HEAD
    n_blocks = tq // qb
    for j in range(n_blocks):
        a = j * qb
        kidx = i * tq + (a - hw) + lax.broadcasted_iota(jnp.int32, (1, kb), 1)
        valid = (kidx >= 0) & (kidx < seq_len) if j in (0, n_blocks - 1) else None
        lse_all = None
        for h in range(N_HEADS):
            c0, c1 = h * HEAD_DIM, (h + 1) * HEAD_DIM
            q = q_ref[0, a:a + qb, c0:c1]
            k = win_ref[a:a + kb, c0:c1]
            v = win_ref[a:a + kb, W + c0:W + c1]
            s = _dot_nt(q, k) + bias_ref[h]
            if valid is not None:
                s = jnp.where(valid, s, NEG)
            m = jnp.max(s, axis=-1, keepdims=True)
            p = jnp.exp2((s - m) * (scale * LOG2_E))
            l = jnp.sum(p, axis=-1, keepdims=True)
            o = _dot(p.astype(BF16), v) * (1.0 / l)
            o_ref[0, a:a + qb, c0:c1] = o.astype(BF16)
            lse = jnp.broadcast_to(m * scale + jnp.log(l), (qb, LANES))
            lse_all = lse if h == 0 else jnp.where(lane_head == h, lse, lse_all)
        lse_ref[0, a:a + qb, :] = lse_all


def _attention(q, kv, bias):
    S, M, W = q.shape
    hw = HALF_WINDOW
    tq = min(M, ATTN_TILE)
    nhb = M // hw
    step = tq // hw
    kern = functools.partial(_attn_kernel, seq_len=M, tq=tq)
    return pl.pallas_call(
        kern,
        name="attn",
        grid=(S, M // tq),
        in_specs=[pl.BlockSpec((1, tq, W), lambda s, i: (s, i, 0)),
                  pl.BlockSpec((1, tq, 2 * W), lambda s, i: (s, i, 0)),
                  pl.BlockSpec((1, hw, 2 * W), lambda s, i: (s, jnp.maximum(i * step - 1, 0), 0)),
                  pl.BlockSpec((1, hw, 2 * W), lambda s, i: (s, jnp.minimum((i + 1) * step, nhb - 1), 0)),
                  _resident(bias.shape)],
        out_specs=[pl.BlockSpec((1, tq, W), lambda s, i: (s, i, 0)),
                   pl.BlockSpec((1, tq, LANES), lambda s, i: (s, i, 0))],
        out_shape=[jax.ShapeDtypeStruct((S, M, W), BF16), jax.ShapeDtypeStruct((S, M, LANES), F32)],
        scratch_shapes=[pltpu.VMEM((tq + 2 * hw, 2 * W), BF16)],
        compiler_params=_params("parallel", "parallel"),
    )(q, kv, kv, kv, bias)


def _combine_kernel(o1_ref, o4_ref, o16_ref, l1_ref, l4_ref, l16_ref, ya_ref, os_ref, ls_ref):
    T, nc = PERM_TILE, BRANCH_WIDTH // LANES
    for g, (d, o_ref, l_ref) in enumerate(((4, o4_ref, l4_ref), (16, o16_ref, l16_ref))):
        for r in range(d):
            ls_ref[g, pl.ds(r, T // d, stride=d), :] = l_ref[0, r]
            for c in range(nc):
                os_ref[g, c, pl.ds(r, T // d, stride=d), :] = o_ref[0, r, :, c * LANES:(c + 1) * LANES].astype(F32)
    l1, l4, l16 = l1_ref[0, 0], ls_ref[0], ls_ref[1]
    m = jnp.maximum(jnp.maximum(l1, l4), l16)
    e1, e4, e16 = jnp.exp(l1 - m), jnp.exp(l4 - m), jnp.exp(l16 - m)
    inv = 1.0 / (e1 + e4 + e16)
    w1, w4, w16 = e1 * inv, e4 * inv, e16 * inv
    for h in range(N_HEADS):
        c0, c1 = h * HEAD_DIM, (h + 1) * HEAD_DIM
        lh = h * LSE_LANES_PER_HEAD
        ya = (w1[:, lh:lh + 1] * o1_ref[0, 0, :, c0:c1].astype(F32)
              + w4[:, lh:lh + 1] * os_ref[0, h] + w16[:, lh:lh + 1] * os_ref[1, h])
        ya_ref[0, :, c0:c1] = ya.astype(BF16)


def _combine(o1, o4, o16, l1, l4, l16):
    B, _, L, W = o1.shape
    T = PERM_TILE

    def spec(d, c):
        return pl.BlockSpec((1, d, T // d, c), lambda b, i: (b, 0, i, 0))

    return pl.pallas_call(
        _combine_kernel,
        name="combine",
        grid=(B, L // T),
        in_specs=[spec(1, W), spec(4, W), spec(16, W), spec(1, LANES), spec(4, LANES), spec(16, LANES)],
        out_specs=pl.BlockSpec((1, T, W), lambda b, i: (b, i, 0)),
        out_shape=jax.ShapeDtypeStruct((B, L, W), BF16),
        scratch_shapes=[pltpu.VMEM((2, W // LANES, T, LANES), F32), pltpu.VMEM((2, T, LANES), F32)],
        compiler_params=_params("parallel", "parallel"),
    )(o1, o4, o16, l1, l4, l16)


def _memkv_kernel(mem_ref, g_ref, w_ref, kv_ref):
    x = mem_ref[0]
    ms = jnp.mean(x * x, axis=-1, keepdims=True)
    xn = (x * lax.rsqrt(ms + EPS) * g_ref[...]).astype(BF16)
    kv_ref[0] = _dot(xn, w_ref[...]).astype(BF16)


def _memkv(mem, gain, w):
    B, N, D = mem.shape
    W2 = w.shape[1]
    return pl.pallas_call(
        _memkv_kernel,
        name="memkv",
        grid=(B,),
        in_specs=[pl.BlockSpec((1, N, D), lambda b: (b, 0, 0)), pl.BlockSpec((1, D), lambda b: (0, 0)),
                  _resident((D, W2))],
        out_specs=pl.BlockSpec((1, N, W2), lambda b: (b, 0, 0)),
        out_shape=jax.ShapeDtypeStruct((B, N, W2), BF16),
        compiler_params=_params("parallel"),
    )(mem, gain.reshape(1, D), w)


def _proj_bd_kernel(h_ref, w_ref, kv_ref, xb_ref, yd_ref):
    W = BRANCH_WIDTH
    z = _dot(h_ref[0], w_ref[...])
    xb_ref[0] = z[:, :W]
    qd = z[:, W:].astype(BF16)
    scale = HEAD_DIM ** -0.5
    for h in range(N_HEADS):
        c0, c1 = h * HEAD_DIM, (h + 1) * HEAD_DIM
        s = _dot_nt(qd[:, c0:c1], kv_ref[0, :, c0:c1])
        m = jnp.max(s, axis=-1, keepdims=True)
        p = jnp.exp2((s - m) * (scale * LOG2_E))
        l = jnp.sum(p, axis=-1, keepdims=True)
        o = _dot(p.astype(BF16), kv_ref[0, :, W + c0:W + c1]) * (1.0 / l)
        yd_ref[0, :, c0:c1] = o.astype(BF16)


def _proj_bd(h, w_bd, kv):
    B, L, D = h.shape
    T, W = PROJ_TILE, BRANCH_WIDTH
    return pl.pallas_call(
        _proj_bd_kernel,
        name="proj_bd",
        grid=(B, L // T),
        in_specs=[pl.BlockSpec((1, T, D), lambda b, i: (b, i, 0)), _resident((D, 2 * W)),
                  pl.BlockSpec((1, N_MEM, 2 * W), lambda b, i: (b, 0, 0))],
        out_specs=[pl.BlockSpec((1, T, W), lambda b, i: (b, i, 0)), pl.BlockSpec((1, T, W), lambda b, i: (b, i, 0))],
        out_shape=[jax.ShapeDtypeStruct((B, L, W), F32), jax.ShapeDtypeStruct((B, L, W), BF16)],
        compiler_params=_params("parallel", "parallel"),
    )(h, w_bd, kv)


def _fft_factors(L):
    n1 = 1 << ((L.bit_length() - 1 + 1) // 2)
    return n1, L // n1


def _dft_cos_sin(n):
    ang = 2.0 * np.pi * ((np.arange(n)[:, None] * np.arange(n)[None, :]) % n) / n
    return np.cos(ang), np.sin(ang)


def _fft1_kernel(x_ref, f_ref, tc_ref, ts_ref, yr_ref, yi_ref, xs_ref, *, n1):
    W = BRANCH_WIDTH
    for j in range(FFT_ROWS):
        xs_ref[:, j * W:(j + 1) * W] = x_ref[0, :, j, :]
    y = _dot(f_ref[...], xs_ref[...].astype(BF16))
    reps = W // FNET_GROUP_WIDTH
    for j in range(FFT_ROWS):
        yr, yi = y[:n1, j * W:(j + 1) * W], y[n1:, j * W:(j + 1) * W]
        tc = jnp.concatenate([tc_ref[j]] * reps, axis=1)
        ts = jnp.concatenate([ts_ref[j]] * reps, axis=1)
        yr_ref[0, j] = yr * tc + yi * ts
        yi_ref[0, j] = yi * tc - yr * ts


def _fft2_kernel(yr_ref, yi_ref, g1_ref, g2_ref, cs_ref, out_ref, ys_ref, *, n2, norm):
    W, gw = BRANCH_WIDTH, FNET_GROUP_WIDTH
    for j in range(FFT_ROWS):
        ys_ref[0, :, j * W:(j + 1) * W] = yr_ref[0, :, j, :]
        ys_ref[1, :, j * W:(j + 1) * W] = yi_ref[0, :, j, :]
    u = _dot(g1_ref[...], ys_ref[0].astype(BF16)) + _dot(g2_ref[...], ys_ref[1].astype(BF16))
    top, bot = u[:n2].astype(BF16), u[n2:].astype(BF16)
    ng = FFT_ROWS * W // gw
    lhs = jnp.concatenate(
        [jnp.concatenate([top[:, j * gw:(j + 1) * gw], bot[:, j * gw:(j + 1) * gw]], axis=1) for j in range(ng)],
        axis=0)
    res = _dot(lhs, cs_ref[...]) * norm
    for j in range(ng):
        k1, c = divmod(j, W // gw)
        out_ref[0, :, k1, c * gw:(c + 1) * gw] = res[j * n2:(j + 1) * n2]


def _fnet(xb, tables):
    B, L, W = xb.shape
    n1, n2 = _fft_factors(L)
    f1, twc, tws, g1, g2, cs = tables
    R = FFT_ROWS
    yr, yi = pl.pallas_call(
        functools.partial(_fft1_kernel, n1=n1),
        name="fft1",
        grid=(B, n2 // R),
        in_specs=[pl.BlockSpec((1, n1, R, W), lambda b, j: (b, 0, j, 0)), _resident((2 * n1, n1)),
                  pl.BlockSpec((R, n1, FNET_GROUP_WIDTH), lambda b, j: (j, 0, 0)),
                  pl.BlockSpec((R, n1, FNET_GROUP_WIDTH), lambda b, j: (j, 0, 0))],
        out_specs=[pl.BlockSpec((1, R, n1, W), lambda b, j: (b, j, 0, 0))] * 2,
        out_shape=[jax.ShapeDtypeStruct((B, n2, n1, W), F32)] * 2,
        scratch_shapes=[pltpu.VMEM((n1, R * W), F32)],
        compiler_params=_params("parallel", "parallel"),
    )(xb.reshape(B, n1, n2, W), f1, twc, tws)
    out = pl.pallas_call(
        functools.partial(_fft2_kernel, n2=n2, norm=float(1.0 / math.sqrt(L * FNET_GROUP_WIDTH))),
        name="fft2",
        grid=(B, n1 // R),
        in_specs=[pl.BlockSpec((1, n2, R, W), lambda b, j: (b, 0, j, 0))] * 2
        + [_resident((2 * n2, n2)), _resident((2 * n2, n2)), _resident((2 * FNET_GROUP_WIDTH, FNET_GROUP_WIDTH))],
        out_specs=pl.BlockSpec((1, n2, R, W), lambda b, j: (b, 0, j, 0)),
        out_shape=jax.ShapeDtypeStruct((B, n2, n1, W), F32),
        scratch_shapes=[pltpu.VMEM((2, n2, R * W), F32)],
        compiler_params=_params("parallel", "parallel"),
    )(yr, yi, g1, g2, cs)
    return out.reshape(B, L, W)


def _fnet_tables(L):
    n1, n2 = _fft_factors(L)
    c1, s1 = _dft_cos_sin(n1)
    c2, s2 = _dft_cos_sin(n2)
    cc, sc = _dft_cos_sin(FNET_GROUP_WIDTH)
    ang = 2.0 * np.pi * (np.arange(n2)[:, None] * np.arange(n1)[None, :]) / L
    bcast = lambda t: jnp.asarray(np.broadcast_to(t[:, :, None], (n2, n1, FNET_GROUP_WIDTH)), F32)
    return (jnp.asarray(np.concatenate([c1, -s1], axis=0), BF16), bcast(np.cos(ang)), bcast(np.sin(ang)),
            jnp.asarray(np.concatenate([c2, -s2], axis=0), BF16), jnp.asarray(np.concatenate([s2, c2], axis=0), BF16),
            jnp.asarray(np.concatenate([cc, sc], axis=0), BF16))


def _proj_c_kernel(h_ref, hp_ref, hn_ref, wqk_ref, wvo_ref, wg_ref, conv_ref, gb_ref,
                   q_ref, k_ref, v_ref, o_ref, g_ref):
    i = pl.program_id(1)
    T, W, halo = PROJ_TILE, BRANCH_WIDTH, CONV_HALO
    h = h_ref[0]
    has_prev = (i > 0).astype(F32)
    has_next = (i < pl.num_programs(1) - 1).astype(F32)
    pad = MLSTM_CONV // 2
    n_ext = T + 2 * halo
    for c in range(2 * W // CONV_COLS):
        cs = slice(c * CONV_COLS, (c + 1) * CONV_COLS)
        w = wqk_ref[:, cs]
        ext = jnp.concatenate([_dot(hp_ref[0], w) * has_prev, _dot(h, w), _dot(hn_ref[0], w) * has_next], axis=0)
        acc = ext[halo:halo + T] * conv_ref[pad:pad + 1, cs]
        for j in range(MLSTM_CONV):
            if j != pad:
                acc = acc + pltpu.roll(ext, (pad - j) % n_ext, 0)[halo:halo + T] * conv_ref[j:j + 1, cs]
        y = acc * (jnp.tanh(acc) + 1.0)
        if c < W // CONV_COLS:
            q_ref[0, :, cs] = (y * (HEAD_DIM ** -0.5)).astype(BF16)
        else:
            k_ref[0, :, c * CONV_COLS - W:(c + 1) * CONV_COLS - W] = y.astype(BF16)
    vo_t = _dot_nt(wvo_ref[...], h)
    v_ref[0] = vo_t[:W].astype(BF16)
    o_ref[0] = vo_t[W:].astype(BF16)
    g_ref[0] = _dot(h, wg_ref[...]) + gb_ref[...]


def _proj_c(h, w_qk, w_vo_t, w_g, conv, gate_bias):
    B, L, D = h.shape
    T, W, halo = PROJ_TILE, BRANCH_WIDTH, CONV_HALO
    step = T // halo
    nhb = L // halo
    tile = lambda c: pl.BlockSpec((1, T, c), lambda b, i: (b, i, 0))
    tile_t = pl.BlockSpec((1, W, T), lambda b, i: (b, 0, i))
    rows, cols = jax.ShapeDtypeStruct((B, L, W), BF16), jax.ShapeDtypeStruct((B, W, L), BF16)
    return pl.pallas_call(
        _proj_c_kernel,
        name="proj_c",
        grid=(B, L // T),
        in_specs=[tile(D),
                  pl.BlockSpec((1, halo, D), lambda b, i: (b, jnp.maximum(i * step - 1, 0), 0)),
                  pl.BlockSpec((1, halo, D), lambda b, i: (b, jnp.minimum((i + 1) * step, nhb - 1), 0)),
                  _resident((D, 2 * W)), _resident((2 * W, D)), _resident((D, MLSTM_GATES)),
                  _resident((MLSTM_CONV, 2 * W)), _resident((1, MLSTM_GATES))],
        out_specs=[tile(W), tile(W), tile_t, tile_t, tile(MLSTM_GATES)],
        out_shape=[rows, rows, cols, cols, jax.ShapeDtypeStruct((B, L, MLSTM_GATES), F32)],
        compiler_params=_params("parallel", "arbitrary"),
    )(h, h, h, w_qk, w_vo_t, w_g, conv, gate_bias.reshape(1, MLSTM_GATES))


GATE_U, GATE_M, GATE_W_INTER, GATE_FLOOR, GATE_WA, GATE_S_OLD = range(6)
N_GATE_TERMS = 6
STATE_ROWS = HEAD_DIM + 16


def _gate_terms_kernel(x_ref, out_ref, g_ref, pe_ref, mp_ref, *, reverse):
    nc, T, H = x_ref.shape[1], MLSTM_CHUNK, N_HEADS
    x = x_ref[0].reshape(nc * 8, T)
    sub = lax.broadcasted_iota(jnp.int32, (nc * 8, T), 0) % 8
    lane = lax.broadcasted_iota(jnp.int32, (nc * 8, T), 1)
    src = lax.broadcasted_iota(jnp.int32, (T, T), 0)
    tgt = lax.broadcasted_iota(jnp.int32, (T, T), 1)
    tri = ((src >= tgt) if reverse else (src <= tgt)).astype(F32)
    cs = jnp.dot(jnp.where(sub >= H, _log_sigmoid(x), x), tri,
                 preferred_element_type=F32, precision=lax.Precision.HIGHEST)
    b = pltpu.roll(cs, nc * 8 - H, 0)
    u = x - b
    p = u
    shift = 1
    while shift < T:
        if reverse:
            p = jnp.maximum(p, jnp.where(lane < T - shift, pltpu.roll(p, T - shift, 1), NEG))
        else:
            p = jnp.maximum(p, jnp.where(lane >= shift, pltpu.roll(p, shift, 1), NEG))
        shift *= 2
    last = 0 if reverse else T - 1
    g = jnp.broadcast_to(b[:, last:last + 1], (nc * 8, T))
    p_end = jnp.broadcast_to(p[:, last:last + 1], (nc * 8, T))
    g_ref[...] = g[:, :LANES].reshape(nc, 8, LANES)
    pe_ref[...] = p_end[:, :LANES].reshape(nc, 8, LANES)

    def step(c, m):
        cc = nc - 1 - c if reverse else c
        mp_ref[cc] = m
        return g_ref[cc] + jnp.maximum(m, pe_ref[cc])

    lax.fori_loop(0, nc, step, jnp.zeros((8, LANES), F32))
    m = jnp.concatenate([mp_ref[...].reshape(nc * 8, LANES)] * (T // LANES), axis=1)
    big_m = jnp.maximum(m, p)
    m_next = g + jnp.maximum(m, p_end)
    terms = {GATE_U: u, GATE_M: big_m, GATE_W_INTER: jnp.exp(m - big_m), GATE_FLOOR: jnp.exp(-(b + big_m)),
             GATE_WA: jnp.exp(g + u - m_next), GATE_S_OLD: jnp.exp(g + m - m_next)}
    for j, val in terms.items():
        out_ref[0, :, j] = jnp.where(sub < H, val, 0.0).reshape(nc, 8, T)


def _gate_terms(gates_dir, *, reverse):
    B, nc, _, T = gates_dir.shape
    return pl.pallas_call(
        functools.partial(_gate_terms_kernel, reverse=reverse),
        name="gate_terms",
        grid=(B,),
        in_specs=[pl.BlockSpec((1, nc, 8, T), lambda b: (b, 0, 0, 0))],
        out_specs=pl.BlockSpec((1, nc, N_GATE_TERMS, 8, T), lambda b: (b, 0, 0, 0, 0)),
        out_shape=jax.ShapeDtypeStruct((B, nc, N_GATE_TERMS, 8, T), F32),
        scratch_shapes=[pltpu.VMEM((nc, 8, LANES), F32)] * 3,
        compiler_params=_params("parallel"),
    )(gates_dir)


def _mlstm_kernel(*refs, reverse, finalize):
    if finalize:
        (q_ref, k_ref, v_ref, r_ref, hf_ref, oc_ref, gain_ref, out_ref, c_ref) = refs
    else:
        (q_ref, k_ref, v_ref, r_ref, out_ref, c_ref) = refs
    T, dh, H = MLSTM_CHUNK, HEAD_DIM, N_HEADS

    @pl.when(pl.program_id(1) == 0)
    def _():
        c_ref[...] = jnp.zeros_like(c_ref)

    src = lax.broadcasted_iota(jnp.int32, (T, T), 0)
    tgt = lax.broadcasted_iota(jnp.int32, (T, T), 1)
    feeds = (src >= tgt) if reverse else (src <= tgt)
    sub = lax.broadcasted_iota(jnp.int32, (STATE_ROWS - dh, T), 0)
    if finalize:
        eye = (src == tgt).astype(BF16)

    chunks = range(MLSTM_CHUNKS_PER_STEP)
    for ci in (reversed(chunks) if reverse else chunks):
        t0, t1 = ci * T, (ci + 1) * T
        u_cols = jnp.transpose(r_ref[0, ci, GATE_U])
        for h in range(H):
            c0, c1 = h * dh, (h + 1) * dh
            q, k, v_t = q_ref[0, t0:t1, c0:c1], k_ref[0, t0:t1, c0:c1], v_ref[0, c0:c1, t0:t1]
            row = lambda j: r_ref[0, ci, j, h:h + 1, :]
            w_inter, wa = row(GATE_W_INTER), row(GATE_WA)
            state = c_ref[h]

            logw = jnp.where(feeds, u_cols[:, h:h + 1] - row(GATE_M), NEG)
            sqk_t = jnp.exp(logw) * _dot_nt(k, q)
            inter = _dot_nt(state.astype(BF16), q)
            num_t = w_inter * inter[:dh] + _dot(v_t, sqk_t.astype(BF16))
            den = w_inter * inter[dh:dh + 1] + jnp.sum(sqk_t, axis=0, keepdims=True)
            hout_t = num_t * (1.0 / jnp.maximum(jnp.abs(den), row(GATE_FLOOR)))

            vw = jnp.concatenate([(v_t.astype(F32) * wa).astype(BF16),
                                  jnp.where(sub == 0, wa, 0.0).astype(BF16)], axis=0)
            c_ref[h] = row(GATE_S_OLD)[:, :dh] * state + _dot(vw, k)

            if finalize:
                hc = _sigmoid(oc_ref[0, c0:c1, t0:t1].astype(F32)) * (hout_t + hf_ref[0, c0:c1, t0:t1])
                mu = jnp.mean(hc, axis=0, keepdims=True)
                cen = hc - mu
                var = jnp.mean(cen * cen, axis=0, keepdims=True)
                gain = jnp.concatenate([gain_ref[c0:c1]] * (T // LANES), axis=1)
                yc_t = cen * lax.rsqrt(var + EPS) * gain
                out_ref[0, t0:t1, c0:c1] = jnp.transpose(yc_t).astype(BF16)
            else:
                out_ref[0, c0:c1, t0:t1] = hout_t


def _mlstm_pass(q, k, v_t, terms, extra, *, reverse):
    B, L, W = k.shape
    cps = MLSTM_CHUNKS_PER_STEP
    T = MLSTM_CHUNK * cps
    n = L // T
    pos = (lambda i: n - 1 - i) if reverse else (lambda i: i)
    rows = pl.BlockSpec((1, T, W), lambda b, i: (b, pos(i), 0))
    cols = pl.BlockSpec((1, W, T), lambda b, i: (b, 0, pos(i)))
    in_specs = [rows, rows, cols,
                pl.BlockSpec((1, cps, N_GATE_TERMS, 8, MLSTM_CHUNK), lambda b, i: (b, pos(i), 0, 0, 0))]
    finalize = extra is not None
    args = [q, k, v_t, terms]
    if finalize:
        in_specs += [cols, cols, _resident((W, LANES))]
        args += list(extra)
    return pl.pallas_call(
        functools.partial(_mlstm_kernel, reverse=reverse, finalize=finalize),
        name="mlstm_bwd" if reverse else "mlstm_fwd",
        grid=(B, n),
        in_specs=in_specs,
        out_specs=rows if finalize else cols,
        out_shape=jax.ShapeDtypeStruct((B, L, W), BF16) if finalize else jax.ShapeDtypeStruct((B, W, L), F32),
        scratch_shapes=[pltpu.VMEM((N_HEADS, STATE_ROWS, HEAD_DIM), F32)],
        compiler_params=_params("parallel", "arbitrary"),
    )(*args)


def _merge_kernel(x_ref, h_ref, ya_ref, yb_ref, yc_ref, yd_ref, wgp_ref, wmg_ref, wbr_ref, wout_ref, g_ref, out_ref):
    W, D = BRANCH_WIDTH, D_MODEL
    h_half = h_ref[...] * 0.5
    acc = None
    for g, br_ref in enumerate((ya_ref, yb_ref, yc_ref, yd_ref)):
        gp_half = _dot(h_half, wgp_ref[:, g * W:(g + 1) * W])
        gated = (br_ref[...].astype(F32) * (gp_half * (jnp.tanh(gp_half) + 1.0))).astype(BF16)
        proj = _dot(gated, wbr_ref[g])
        term = (jnp.tanh(_dot(h_half, wmg_ref[:, g * D:(g + 1) * D])) + 1.0) * proj
        acc = term if acc is None else acc + term
    out = _dot((0.5 * acc).astype(BF16), wout_ref[...])
    ms = jnp.mean(out * out, axis=-1, keepdims=True)
    out_ref[...] = x_ref[...] + out * lax.rsqrt(ms + EPS) * g_ref[...]


def _merge(x_rows, h_rows, branches, w_gp, w_mg, w_br, w_out, gain):
    R, D = x_rows.shape
    T, W = ROW_TILE, BRANCH_WIDTH
    rows = lambda c: pl.BlockSpec((T, c), lambda i: (i, 0))
    return pl.pallas_call(
        _merge_kernel,
        name="merge",
        grid=(R // T,),
        in_specs=[rows(D), rows(D)] + [rows(W)] * N_BRANCHES
        + [_resident(w_gp.shape), _resident(w_mg.shape), _resident(w_br.shape), _resident(w_out.shape),
           _resident((1, D))],
        out_specs=rows(D),
        out_shape=jax.ShapeDtypeStruct((R, D), F32),
        compiler_params=_params("parallel"),
    )(x_rows, h_rows, *branches, w_gp, w_mg, w_br, w_out, gain.reshape(1, D))


def _t5_bucket(rel):
    nb = REL_BUCKETS // 2
    max_exact = nb // 2
    ret = (rel > 0).astype(np.int32) * nb
    n = np.abs(rel)
    large = max_exact + (np.log(np.maximum(n, 1) / max_exact) / np.log(REL_MAX_DIST / max_exact)
                         * (nb - max_exact)).astype(np.int32)
    large = np.minimum(large, nb - 1)
    return (ret + np.where(n < max_exact, n, large)).astype(np.int32)


def _band_bias(rel_bias, group, dilation):
    qb, hw = ATTN_Q_BLOCK, HALF_WINDOW
    rel = np.arange(qb + 2 * hw)[None, :] - hw - np.arange(qb)[:, None]
    heads = rel_bias[:, group * N_HEADS:(group + 1) * N_HEADS].astype(F32)
    onehot = (jnp.asarray(_t5_bucket(dilation * rel))[None, :, :] == jnp.arange(REL_BUCKETS)[:, None, None])
    bias = jnp.sum(jnp.where(onehot[:, None], heads[:, :, None, None], 0.0), axis=0)
    return jnp.where(jnp.asarray(np.abs(rel) <= hw)[None], bias * (HEAD_DIM ** 0.5), NEG)


def _split_weights(w_in, conv_qk, w_mem_kv, w_branch, w_out):
    W, D, A = BRANCH_WIDTH, D_MODEL, A_QKV_WIDTH
    off = np.cumsum([0, A, A, A, W, W, W, W, W, MLSTM_GATES, W, N_BRANCHES * W, N_BRANCHES * D])
    seg = lambda j: w_in[:, off[j]:off[j + 1]]
    qa, ka, va = seg(0), seg(1), seg(2)
    w_a = [jnp.concatenate([t[:, g * W:(g + 1) * W] for t in (qa, ka, va)], axis=1).astype(BF16)
           for g in range(len(DILATIONS))]
    return dict(
        a=w_a,
        bd=jnp.concatenate([seg(3), seg(9)], axis=1).astype(BF16),
        qk=jnp.concatenate([seg(4), seg(5)], axis=1).astype(BF16),
        vo_t=jnp.concatenate([seg(6), seg(7)], axis=1).T.astype(BF16),
        g=seg(8).astype(BF16),
        gp=seg(10).astype(BF16),
        mg=seg(11).astype(BF16),
        conv=0.5 * conv_qk.astype(F32),
        memkv=w_mem_kv.astype(BF16),
        br=w_branch.astype(BF16),
        out=w_out.astype(BF16),
    )


def _layer(x, mem, biases, fft_tables, wts, norm_pre, gate_bias, head_gain, mem_norm, norm_post):
    B, L, D = x.shape
    W = BRANCH_WIDTH
    h, h4, h16 = _prenorm(x, norm_pre)

    outs, lses = [], []
    for g, (d, hp) in enumerate(zip(DILATIONS, (h, h4, h16))):
        q, kv = _proj_a(hp.reshape(B * L, D), wts["a"][g])
        o, lse = _attention(q.reshape(B * d, L // d, W), kv.reshape(B * d, L // d, 2 * W), biases[g])
        outs.append(o.reshape(B, d, L // d, W))
        lses.append(lse.reshape(B, d, L // d, LANES))
    ya = _combine(*outs, *lses)

    kv_mem = _memkv(mem, mem_norm, wts["memkv"])
    xb, yd = _proj_bd(h, wts["bd"], kv_mem)
    yb = _fnet(xb, fft_tables)

    qm, km, v_t, oc_t, gates = _proj_c(h, wts["qk"], wts["vo_t"], wts["g"], wts["conv"], gate_bias.reshape(-1))
    T = MLSTM_CHUNK
    gates_t = jnp.swapaxes(gates.reshape(B, L // T, T, MLSTM_GATES), 2, 3)
    terms_fwd = _gate_terms(gates_t[:, :, :8], reverse=False)
    terms_bwd = _gate_terms(gates_t[:, :, 8:], reverse=True)
    gain = jnp.broadcast_to(head_gain.astype(F32)[:, None], (W, LANES))
    h_fwd_t = _mlstm_pass(qm, km, v_t, terms_fwd, None, reverse=False)
    yc = _mlstm_pass(qm, km, v_t, terms_bwd, (h_fwd_t, oc_t, gain), reverse=True)

    rows = lambda t: t.reshape(B * L, t.shape[-1])
    y = _merge(rows(x), rows(h), [rows(ya), rows(yb), rows(yc), rows(yd)],
               wts["gp"], wts["mg"], wts["br"], wts["out"], norm_post)
    return y.reshape(B, L, D)


def kernel(x_prompt, x_sample, mem_prompt, mem_sample, rel_bias, norm_pre, w_in, conv_qk, mlstm_gate_bias,
           mlstm_head_gain, mem_norm, w_mem_kv, w_branch, w_out, norm_post):
    depth = w_in.shape[0]
    biases = [_band_bias(rel_bias, g, d) for g, d in enumerate(DILATIONS)]
    layer_wts = [_split_weights(w_in[l], conv_qk[l], w_mem_kv[l], w_branch[l], w_out[l]) for l in range(depth)]
    outs = []
    for x, mem in ((x_prompt, mem_prompt), (x_sample, mem_sample)):
        tables = _fnet_tables(x.shape[1])
        for l in range(depth):
            x = _layer(x, mem, biases, tables, layer_wts[l], norm_pre[l], mlstm_gate_bias[l],
                       mlstm_head_gain[l], mem_norm[l], norm_post[l])
        outs.append(x)
    return tuple(outs)
```

```python
import functools
import math

import numpy as np
import jax
import jax.numpy as jnp
from jax import lax
from jax.experimental import pallas as pl
from jax.experimental.pallas import tpu as pltpu

D_MODEL = 1024
BRANCH_WIDTH = 512
HEAD_DIM = 128
N_HEADS = 4
N_BRANCHES = 4
DILATIONS = (1, 4, 16)
HALF_WINDOW = 64
A_QKV_WIDTH = len(DILATIONS) * BRANCH_WIDTH
REL_BUCKETS = 32
REL_MAX_DIST = 1024
FNET_GROUP_WIDTH = 128
MLSTM_CONV = 5
MLSTM_GATES = 16
N_MEM = 256
EPS = 1e-6
NEG = -1e30
LOG2_E = math.log2(math.e)

F32 = jnp.float32
BF16 = jnp.bfloat16
LANES = 128
LSE_LANES_PER_HEAD = LANES // N_HEADS

VMEM_LIMIT_BYTES = 56 * 1024 * 1024
PERM_TILE = 1024
ROW_TILE = 512
PROJ_TILE = 2048
ATTN_TILE = 2048
ATTN_Q_BLOCK = 128
MLSTM_CHUNK = 256
MLSTM_CHUNKS_PER_STEP = 8
FFT_ROWS = 8
CONV_HALO = 16
CONV_COLS = 256


def _params(*sem):
    return pltpu.CompilerParams(dimension_semantics=sem, vmem_limit_bytes=VMEM_LIMIT_BYTES)


def _resident(shape):
    nd = len(shape)
    return pl.BlockSpec(shape, lambda *_: (0,) * nd, pipeline_mode=pl.Buffered(1))


def _sigmoid(x):
    return 0.5 * jnp.tanh(0.5 * x) + 0.5


def _log_sigmoid(x):
    return jnp.minimum(x, 0.0) - jnp.log(1.0 + jnp.exp(-jnp.abs(x)))


def _dot(a, b):
    return jnp.dot(a, b, preferred_element_type=F32)


def _dot_nt(a, b):
    return lax.dot_general(a, b, (((1,), (1,)), ((), ())), preferred_element_type=F32)


def _dot_tn(a, b):
    return lax.dot_general(a, b, (((0,), (0,)), ((), ())), preferred_element_type=F32)


def _prenorm_kernel(x_ref, g_ref, h_ref, h4_ref, h16_ref, xs_ref):
    T = PERM_TILE
    x = x_ref[0]
    ms = jnp.mean(x * x, axis=-1, keepdims=True)
    xn = x * lax.rsqrt(ms + EPS) * g_ref[...]
    h_ref[0] = xn.astype(BF16)
    for c in range(D_MODEL // LANES):
        xs_ref[c] = xn[:, c * LANES:(c + 1) * LANES]
    for d, out_ref in ((4, h4_ref), (16, h16_ref)):
        for r in range(d):
            for c in range(D_MODEL // LANES):
                out_ref[0, r, :, c * LANES:(c + 1) * LANES] = xs_ref[c, pl.ds(r, T // d, stride=d), :].astype(BF16)


def _prenorm(x, gain):
    B, L, D = x.shape
    T = PERM_TILE
    return pl.pallas_call(
        _prenorm_kernel,
        name="prenorm",
        grid=(B, L // T),
        in_specs=[pl.BlockSpec((1, T, D), lambda b, i: (b, i, 0)),
                  pl.BlockSpec((1, D), lambda b, i: (0, 0))],
        out_specs=[pl.BlockSpec((1, T, D), lambda b, i: (b, i, 0)),
                   pl.BlockSpec((1, 4, T // 4, D), lambda b, i: (b, 0, i, 0)),
                   pl.BlockSpec((1, 16, T // 16, D), lambda b, i: (b, 0, i, 0))],
        out_shape=[jax.ShapeDtypeStruct((B, L, D), BF16),
                   jax.ShapeDtypeStruct((B, 4, L // 4, D), BF16),
                   jax.ShapeDtypeStruct((B, 16, L // 16, D), BF16)],
        scratch_shapes=[pltpu.VMEM((D // LANES, T, LANES), F32)],
        compiler_params=_params("parallel", "parallel"),
    )(x, gain.reshape(1, D))


def _proj_a_kernel(h_ref, w_ref, q_ref, kv_ref):
    z = _dot(h_ref[...], w_ref[...])
    q_ref[...] = z[:, :BRANCH_WIDTH].astype(BF16)
    kv_ref[...] = z[:, BRANCH_WIDTH:].astype(BF16)


def _proj_a(h_rows, w_qkv):
    R, D = h_rows.shape
    T = PROJ_TILE
    W = BRANCH_WIDTH
    return pl.pallas_call(
        _proj_a_kernel,
        name="proj_a",
        grid=(R // T,),
        in_specs=[pl.BlockSpec((T, D), lambda i: (i, 0)), _resident((D, 3 * W))],
        out_specs=[pl.BlockSpec((T, W), lambda i: (i, 0)), pl.BlockSpec((T, 2 * W), lambda i: (i, 0))],
        out_shape=[jax.ShapeDtypeStruct((R, W), BF16), jax.ShapeDtypeStruct((R, 2 * W), BF16)],
        compiler_params=_params("parallel"),
    )(h_rows, w_qkv)


def _attn_kernel(q_ref, kv_ref, kvp_ref, kvn_ref, bias_ref, o_ref, lse_ref, win_ref, *, seq_len, tq):
    i = pl.program_id(1)
    hw, qb, W = HALF_WINDOW, ATTN_Q_BLOCK, BRANCH_WIDTH
    kb = qb + 2 * hw
    win_ref[0:hw] = kvp_ref[0]
    win_ref[hw:hw + tq] = kv_ref[0]
    win_ref[hw + tq:] = kvn_ref[0]
    scale = HEAD_DIM ** -0.5
    lane_head = lax.broadcasted_iota(jnp.int32, (qb, LANES), 1) // LSE_LANES_PER_HEAD
    n_blocks = tq // qb
    for j in range(n_blocks):
        a = j * qb
        kidx = i * tq + (a - hw) + lax.broadcasted_iota(jnp.int32, (1, kb), 1)
        valid = (kidx >= 0) & (kidx < seq_len) if j in (0, n_blocks - 1) else None
        lse_all = None
        for h in range(N_HEADS):
            c0, c1 = h * HEAD_DIM, (h + 1) * HEAD_DIM
            q = q_ref[0, a:a + qb, c0:c1]
            k = win_ref[a:a + kb, c0:c1]
            v = win_ref[a:a + kb, W + c0:W + c1]
            s = _dot_nt(q, k) + bias_ref[h]
            if valid is not None:
                s = jnp.where(valid, s, NEG)
            m = jnp.max(s, axis=-1, keepdims=True)
            p = jnp.exp2((s - m) * (scale * LOG2_E))
            l = jnp.sum(p, axis=-1, keepdims=True)
            o = _dot(p.astype(BF16), v) * (1.0 / l)
            o_ref[0, a:a + qb, c0:c1] = o.astype(BF16)
            lse = jnp.broadcast_to(m * scale + jnp.log(l), (qb, LANES))
            lse_all = lse if h == 0 else jnp.where(lane_head == h, lse, lse_all)
        lse_ref[0, a:a + qb, :] = lse_all


def _attention(q, kv, bias):
    S, M, W = q.shape
    hw = HALF_WINDOW
    tq = min(M, ATTN_TILE)
    nhb = M // hw
    step = tq // hw
    kern = functools.partial(_attn_kernel, seq_len=M, tq=tq)
    return pl.pallas_call(
        kern,
        name="attn",
        grid=(S, M // tq),
        in_specs=[pl.BlockSpec((1, tq, W), lambda s, i: (s, i, 0)),
                  pl.BlockSpec((1, tq, 2 * W), lambda s, i: (s, i, 0)),
                  pl.BlockSpec((1, hw, 2 * W), lambda s, i: (s, jnp.maximum(i * step - 1, 0), 0)),
                  pl.BlockSpec((1, hw, 2 * W), lambda s, i: (s, jnp.minimum((i + 1) * step, nhb - 1), 0)),
                  _resident(bias.shape)],
        out_specs=[pl.BlockSpec((1, tq, W), lambda s, i: (s, i, 0)),
                   pl.BlockSpec((1, tq, LANES), lambda s, i: (s, i, 0))],
        out_shape=[jax.ShapeDtypeStruct((S, M, W), BF16), jax.ShapeDtypeStruct((S, M, LANES), F32)],
        scratch_shapes=[pltpu.VMEM((tq + 2 * hw, 2 * W), BF16)],
        compiler_params=_params("parallel", "parallel"),
    )(q, kv, kv, kv, bias)


def _combine_kernel(o1_ref, o4_ref, o16_ref, l1_ref, l4_ref, l16_ref, ya_ref, os_ref, ls_ref):
    T, nc = PERM_TILE, BRANCH_WIDTH // LANES
    for g, (d, o_ref, l_ref) in enumerate(((4, o4_ref, l4_ref), (16, o16_ref, l16_ref))):
        for r in range(d):
            ls_ref[g, pl.ds(r, T // d, stride=d), :] = l_ref[0, r]
            for c in range(nc):
                os_ref[g, c, pl.ds(r, T // d, stride=d), :] = o_ref[0, r, :, c * LANES:(c + 1) * LANES].astype(F32)
    l1, l4, l16 = l1_ref[0, 0], ls_ref[0], ls_ref[1]
    m = jnp.maximum(jnp.maximum(l1, l4), l16)
    e1, e4, e16 = jnp.exp(l1 - m), jnp.exp(l4 - m), jnp.exp(l16 - m)
    inv = 1.0 / (e1 + e4 + e16)
    w1, w4, w16 = e1 * inv, e4 * inv, e16 * inv
    for h in range(N_HEADS):
        c0, c1 = h * HEAD_DIM, (h + 1) * HEAD_DIM
        lh = h * LSE_LANES_PER_HEAD
        ya = (w1[:, lh:lh + 1] * o1_ref[0, 0, :, c0:c1].astype(F32)
              + w4[:, lh:lh + 1] * os_ref[0, h] + w16[:, lh:lh + 1] * os_ref[1, h])
        ya_ref[0, :, c0:c1] = ya.astype(BF16)


def _combine(o1, o4, o16, l1, l4, l16):
    B, _, L, W = o1.shape
    T = PERM_TILE

    def spec(d, c):
        return pl.BlockSpec((1, d, T // d, c), lambda b, i: (b, 0, i, 0))

    return pl.pallas_call(
        _combine_kernel,
        name="combine",
        grid=(B, L // T),
        in_specs=[spec(1, W), spec(4, W), spec(16, W), spec(1, LANES), spec(4, LANES), spec(16, LANES)],
        out_specs=pl.BlockSpec((1, T, W), lambda b, i: (b, i, 0)),
        out_shape=jax.ShapeDtypeStruct((B, L, W), BF16),
        scratch_shapes=[pltpu.VMEM((2, W // LANES, T, LANES), F32), pltpu.VMEM((2, T, LANES), F32)],
        compiler_params=_params("parallel", "parallel"),
    )(o1, o4, o16, l1, l4, l16)


def _memkv_kernel(mem_ref, g_ref, w_ref, kv_ref):
    x = mem_ref[0]
    ms = jnp.mean(x * x, axis=-1, keepdims=True)
    xn = (x * lax.rsqrt(ms + EPS) * g_ref[...]).astype(BF16)
    kv_ref[0] = _dot(xn, w_ref[...]).astype(BF16)


def _memkv(mem, gain, w):
    B, N, D = mem.shape
    W2 = w.shape[1]
    return pl.pallas_call(
        _memkv_kernel,
        name="memkv",
        grid=(B,),
        in_specs=[pl.BlockSpec((1, N, D), lambda b: (b, 0, 0)), pl.BlockSpec((1, D), lambda b: (0, 0)),
                  _resident((D, W2))],
        out_specs=pl.BlockSpec((1, N, W2), lambda b: (b, 0, 0)),
        out_shape=jax.ShapeDtypeStruct((B, N, W2), BF16),
        compiler_params=_params("parallel"),
    )(mem, gain.reshape(1, D), w)


def _proj_bd_kernel(h_ref, w_ref, kv_ref, xb_ref, yd_ref):
    W = BRANCH_WIDTH
    z = _dot(h_ref[0], w_ref[...])
    xb_ref[0] = z[:, :W]
    qd = z[:, W:].astype(BF16)
    scale = HEAD_DIM ** -0.5
    for h in range(N_HEADS):
        c0, c1 = h * HEAD_DIM, (h + 1) * HEAD_DIM
        s = _dot_nt(qd[:, c0:c1], kv_ref[0, :, c0:c1])
        m = jnp.max(s, axis=-1, keepdims=True)
        p = jnp.exp2((s - m) * (scale * LOG2_E))
        l = jnp.sum(p, axis=-1, keepdims=True)
        o = _dot(p.astype(BF16), kv_ref[0, :, W + c0:W + c1]) * (1.0 / l)
        yd_ref[0, :, c0:c1] = o.astype(BF16)


def _proj_bd(h, w_bd, kv):
    B, L, D = h.shape
    T, W = PROJ_TILE, BRANCH_WIDTH
    return pl.pallas_call(
        _proj_bd_kernel,
        name="proj_bd",
        grid=(B, L // T),
        in_specs=[pl.BlockSpec((1, T, D), lambda b, i: (b, i, 0)), _resident((D, 2 * W)),
                  pl.BlockSpec((1, N_MEM, 2 * W), lambda b, i: (b, 0, 0))],
        out_specs=[pl.BlockSpec((1, T, W), lambda b, i: (b, i, 0)), pl.BlockSpec((1, T, W), lambda b, i: (b, i, 0))],
        out_shape=[jax.ShapeDtypeStruct((B, L, W), F32), jax.ShapeDtypeStruct((B, L, W), BF16)],
        compiler_params=_params("parallel", "parallel"),
    )(h, w_bd, kv)


def _fft_factors(L):
    n1 = 1 << ((L.bit_length() - 1 + 1) // 2)
    return n1, L // n1


def _dft_cos_sin(n):
    ang = 2.0 * np.pi * ((np.arange(n)[:, None] * np.arange(n)[None, :]) % n) / n
    return np.cos(ang), np.sin(ang)


def _fft1_kernel(x_ref, f_ref, tc_ref, ts_ref, yr_ref, yi_ref, xs_ref, *, n1):
    W = BRANCH_WIDTH
    for j in range(FFT_ROWS):
        xs_ref[:, j * W:(j + 1) * W] = x_ref[0, :, j, :]
    y = _dot(f_ref[...], xs_ref[...].astype(BF16))
    reps = W // FNET_GROUP_WIDTH
    for j in range(FFT_ROWS):
        yr, yi = y[:n1, j * W:(j + 1) * W], y[n1:, j * W:(j + 1) * W]
        tc = jnp.concatenate([tc_ref[j]] * reps, axis=1)
        ts = jnp.concatenate([ts_ref[j]] * reps, axis=1)
        yr_ref[0, j] = yr * tc + yi * ts
        yi_ref[0, j] = yi * tc - yr * ts


def _fft2_kernel(yr_ref, yi_ref, g1_ref, g2_ref, cs_ref, out_ref, ys_ref, *, n2, norm):
    W, gw = BRANCH_WIDTH, FNET_GROUP_WIDTH
    for j in range(FFT_ROWS):
        ys_ref[0, :, j * W:(j + 1) * W] = yr_ref[0, :, j, :]
        ys_ref[1, :, j * W:(j + 1) * W] = yi_ref[0, :, j, :]
    u = _dot(g1_ref[...], ys_ref[0].astype(BF16)) + _dot(g2_ref[...], ys_ref[1].astype(BF16))
    top, bot = u[:n2].astype(BF16), u[n2:].astype(BF16)
    ng = FFT_ROWS * W // gw
    lhs = jnp.concatenate(
        [jnp.concatenate([top[:, j * gw:(j + 1) * gw], bot[:, j * gw:(j + 1) * gw]], axis=1) for j in range(ng)],
        axis=0)
    res = _dot(lhs, cs_ref[...]) * norm
    for j in range(ng):
        k1, c = divmod(j, W // gw)
        out_ref[0, :, k1, c * gw:(c + 1) * gw] = res[j * n2:(j + 1) * n2]


def _fnet(xb, tables):
    B, L, W = xb.shape
    n1, n2 = _fft_factors(L)
    f1, twc, tws, g1, g2, cs = tables
    R = FFT_ROWS
    yr, yi = pl.pallas_call(
        functools.partial(_fft1_kernel, n1=n1),
        name="fft1",
        grid=(B, n2 // R),
        in_specs=[pl.BlockSpec((1, n1, R, W), lambda b, j: (b, 0, j, 0)), _resident((2 * n1, n1)),
                  pl.BlockSpec((R, n1, FNET_GROUP_WIDTH), lambda b, j: (j, 0, 0)),
                  pl.BlockSpec((R, n1, FNET_GROUP_WIDTH), lambda b, j: (j, 0, 0))],
        out_specs=[pl.BlockSpec((1, R, n1, W), lambda b, j: (b, j, 0, 0))] * 2,
        out_shape=[jax.ShapeDtypeStruct((B, n2, n1, W), F32)] * 2,
        scratch_shapes=[pltpu.VMEM((n1, R * W), F32)],
        compiler_params=_params("parallel", "parallel"),
    )(xb.reshape(B, n1, n2, W), f1, twc, tws)
    out = pl.pallas_call(
        functools.partial(_fft2_kernel, n2=n2, norm=float(1.0 / math.sqrt(L * FNET_GROUP_WIDTH))),
        name="fft2",
        grid=(B, n1 // R),
        in_specs=[pl.BlockSpec((1, n2, R, W), lambda b, j: (b, 0, j, 0))] * 2
        + [_resident((2 * n2, n2)), _resident((2 * n2, n2)), _resident((2 * FNET_GROUP_WIDTH, FNET_GROUP_WIDTH))],
        out_specs=pl.BlockSpec((1, n2, R, W), lambda b, j: (b, 0, j, 0)),
        out_shape=jax.ShapeDtypeStruct((B, n2, n1, W), F32),
        scratch_shapes=[pltpu.VMEM((2, n2, R * W), F32)],
        compiler_params=_params("parallel", "parallel"),
    )(yr, yi, g1, g2, cs)
    return out.reshape(B, L, W)


def _fnet_tables(L):
    n1, n2 = _fft_factors(L)
    c1, s1 = _dft_cos_sin(n1)
    c2, s2 = _dft_cos_sin(n2)
    cc, sc = _dft_cos_sin(FNET_GROUP_WIDTH)
    ang = 2.0 * np.pi * (np.arange(n2)[:, None] * np.arange(n1)[None, :]) / L
    bcast = lambda t: jnp.asarray(np.broadcast_to(t[:, :, None], (n2, n1, FNET_GROUP_WIDTH)), F32)
    return (jnp.asarray(np.concatenate([c1, -s1], axis=0), BF16), bcast(np.cos(ang)), bcast(np.sin(ang)),
            jnp.asarray(np.concatenate([c2, -s2], axis=0), BF16), jnp.asarray(np.concatenate([s2, c2], axis=0), BF16),
            jnp.asarray(np.concatenate([cc, sc], axis=0), BF16))


def _proj_c_kernel(h_ref, hp_ref, hn_ref, wqk_ref, wvo_ref, wg_ref, conv_ref, gb_ref,
                   q_ref, k_ref, v_ref, o_ref, g_ref):
    i = pl.program_id(1)
    T, W, halo = PROJ_TILE, BRANCH_WIDTH, CONV_HALO
    h = h_ref[0]
    has_prev = (i > 0).astype(F32)
    has_next = (i < pl.num_programs(1) - 1).astype(F32)
    pad = MLSTM_CONV // 2
    n_ext = T + 2 * halo
    for c in range(2 * W // CONV_COLS):
        cs = slice(c * CONV_COLS, (c + 1) * CONV_COLS)
        w = wqk_ref[:, cs]
        ext = jnp.concatenate([_dot(hp_ref[0], w) * has_prev, _dot(h, w), _dot(hn_ref[0], w) * has_next], axis=0)
        acc = ext[halo:halo + T] * conv_ref[pad:pad + 1, cs]
        for j in range(MLSTM_CONV):
            if j != pad:
                acc = acc + pltpu.roll(ext, (pad - j) % n_ext, 0)[halo:halo + T] * conv_ref[j:j + 1, cs]
        y = acc * (jnp.tanh(acc) + 1.0)
        if c < W // CONV_COLS:
            q_ref[0, :, cs] = (y * (HEAD_DIM ** -0.5)).astype(BF16)
        else:
            k_ref[0, :, c * CONV_COLS - W:(c + 1) * CONV_COLS - W] = y.astype(BF16)
    vo_t = _dot_nt(wvo_ref[...], h)
    v_ref[0] = vo_t[:W].astype(BF16)
    o_ref[0] = vo_t[W:].astype(BF16)
    g_ref[0] = _dot(h, wg_ref[...]) + gb_ref[...]


def _proj_c(h, w_qk, w_vo_t, w_g, conv, gate_bias):
    B, L, D = h.shape
    T, W, halo = PROJ_TILE, BRANCH_WIDTH, CONV_HALO
    step = T // halo
    nhb = L // halo
    tile = lambda c: pl.BlockSpec((1, T, c), lambda b, i: (b, i, 0))
    tile_t = pl.BlockSpec((1, W, T), lambda b, i: (b, 0, i))
    rows, cols = jax.ShapeDtypeStruct((B, L, W), BF16), jax.ShapeDtypeStruct((B, W, L), BF16)
    return pl.pallas_call(
        _proj_c_kernel,
        name="proj_c",
        grid=(B, L // T),
        in_specs=[tile(D),
                  pl.BlockSpec((1, halo, D), lambda b, i: (b, jnp.maximum(i * step - 1, 0), 0)),
                  pl.BlockSpec((1, halo, D), lambda b, i: (b, jnp.minimum((i + 1) * step, nhb - 1), 0)),
                  _resident((D, 2 * W)), _resident((2 * W, D)), _resident((D, MLSTM_GATES)),
                  _resident((MLSTM_CONV, 2 * W)), _resident((1, MLSTM_GATES))],
        out_specs=[tile(W), tile(W), tile_t, tile_t, tile(MLSTM_GATES)],
        out_shape=[rows, rows, cols, cols, jax.ShapeDtypeStruct((B, L, MLSTM_GATES), F32)],
        compiler_params=_params("parallel", "arbitrary"),
    )(h, h, h, w_qk, w_vo_t, w_g, conv, gate_bias.reshape(1, MLSTM_GATES))


GATE_U, GATE_M, GATE_W_INTER, GATE_FLOOR, GATE_WA, GATE_S_OLD = range(6)
N_GATE_TERMS = 6
STATE_ROWS = HEAD_DIM + 16


def _gate_terms_kernel(x_ref, out_ref, g_ref, pe_ref, mp_ref, *, reverse):
    nc, T, H = x_ref.shape[1], MLSTM_CHUNK, N_HEADS
    x = x_ref[0].reshape(nc * 8, T)
    sub = lax.broadcasted_iota(jnp.int32, (nc * 8, T), 0) % 8
    lane = lax.broadcasted_iota(jnp.int32, (nc * 8, T), 1)
    src = lax.broadcasted_iota(jnp.int32, (T, T), 0)
    tgt = lax.broadcasted_iota(jnp.int32, (T, T), 1)
    tri = ((src >= tgt) if reverse else (src <= tgt)).astype(F32)
    cs = jnp.dot(jnp.where(sub >= H, _log_sigmoid(x), x), tri,
                 preferred_element_type=F32, precision=lax.Precision.HIGHEST)
    b = pltpu.roll(cs, nc * 8 - H, 0)
    u = x - b
    p = u
    shift = 1
    while shift < T:
        if reverse:
            p = jnp.maximum(p, jnp.where(lane < T - shift, pltpu.roll(p, T - shift, 1), NEG))
        else:
            p = jnp.maximum(p, jnp.where(lane >= shift, pltpu.roll(p, shift, 1), NEG))
        shift *= 2
    last = 0 if reverse else T - 1
    g = jnp.broadcast_to(b[:, last:last + 1], (nc * 8, T))
    p_end = jnp.broadcast_to(p[:, last:last + 1], (nc * 8, T))
    g_ref[...] = g[:, :LANES].reshape(nc, 8, LANES)
    pe_ref[...] = p_end[:, :LANES].reshape(nc, 8, LANES)

    def step(c, m):
        cc = nc - 1 - c if reverse else c
        mp_ref[cc] = m
        return g_ref[cc] + jnp.maximum(m, pe_ref[cc])

    lax.fori_loop(0, nc, step, jnp.zeros((8, LANES), F32))
    m = jnp.concatenate([mp_ref[...].reshape(nc * 8, LANES)] * (T // LANES), axis=1)
    big_m = jnp.maximum(m, p)
    m_next = g + jnp.maximum(m, p_end)
    terms = {GATE_U: u, GATE_M: big_m, GATE_W_INTER: jnp.exp(m - big_m), GATE_FLOOR: jnp.exp(-(b + big_m)),
             GATE_WA: jnp.exp(g + u - m_next), GATE_S_OLD: jnp.exp(g + m - m_next)}
    for j, val in terms.items():
        out_ref[0, :, j] = jnp.where(sub < H, val, 0.0).reshape(nc, 8, T)


def _gate_terms(gates_dir, *, reverse):
    B, nc, _, T = gates_dir.shape
    return pl.pallas_call(
        functools.partial(_gate_terms_kernel, reverse=reverse),
        name="gate_terms",
        grid=(B,),
        in_specs=[pl.BlockSpec((1, nc, 8, T), lambda b: (b, 0, 0, 0))],
        out_specs=pl.BlockSpec((1, nc, N_GATE_TERMS, 8, T), lambda b: (b, 0, 0, 0, 0)),
        out_shape=jax.ShapeDtypeStruct((B, nc, N_GATE_TERMS, 8, T), F32),
        scratch_shapes=[pltpu.VMEM((nc, 8, LANES), F32)] * 3,
        compiler_params=_params("parallel"),
    )(gates_dir)


def _mlstm_kernel(*refs, reverse, finalize):
    if finalize:
        (q_ref, k_ref, v_ref, r_ref, hf_ref, oc_ref, gain_ref, out_ref, c_ref) = refs
    else:
        (q_ref, k_ref, v_ref, r_ref, out_ref, c_ref) = refs
    T, dh, H = MLSTM_CHUNK, HEAD_DIM, N_HEADS

    @pl.when(pl.program_id(1) == 0)
    def _():
        c_ref[...] = jnp.zeros_like(c_ref)

    src = lax.broadcasted_iota(jnp.int32, (T, T), 0)
    tgt = lax.broadcasted_iota(jnp.int32, (T, T), 1)
    feeds = (src >= tgt) if reverse else (src <= tgt)
    sub = lax.broadcasted_iota(jnp.int32, (STATE_ROWS - dh, T), 0)
    if finalize:
        eye = (src == tgt).astype(BF16)

    chunks = range(MLSTM_CHUNKS_PER_STEP)
    for ci in (reversed(chunks) if reverse else chunks):
        t0, t1 = ci * T, (ci + 1) * T
        u_cols = jnp.transpose(r_ref[0, ci, GATE_U])
        for h in range(H):
            c0, c1 = h * dh, (h + 1) * dh
            q, k, v_t = q_ref[0, t0:t1, c0:c1], k_ref[0, t0:t1, c0:c1], v_ref[0, c0:c1, t0:t1]
            row = lambda j: r_ref[0, ci, j, h:h + 1, :]
            w_inter, wa = row(GATE_W_INTER), row(GATE_WA)
            state = c_ref[h]

            logw = jnp.where(feeds, u_cols[:, h:h + 1] - row(GATE_M), NEG)
            sqk_t = jnp.exp(logw) * _dot_nt(k, q)
            inter = _dot_nt(state.astype(BF16), q)
            num_t = w_inter * inter[:dh] + _dot(v_t, sqk_t.astype(BF16))
            den = w_inter * inter[dh:dh + 1] + jnp.sum(sqk_t, axis=0, keepdims=True)
            hout_t = num_t * (1.0 / jnp.maximum(jnp.abs(den), row(GATE_FLOOR)))

            vw = jnp.concatenate([(v_t.astype(F32) * wa).astype(BF16),
                                  jnp.where(sub == 0, wa, 0.0).astype(BF16)], axis=0)
            c_ref[h] = row(GATE_S_OLD)[:, :dh] * state + _dot(vw, k)

            if finalize:
                hc = _sigmoid(oc_ref[0, c0:c1, t0:t1].astype(F32)) * (hout_t + hf_ref[0, c0:c1, t0:t1])
                mu = jnp.mean(hc, axis=0, keepdims=True)
                cen = hc - mu
                var = jnp.mean(cen * cen, axis=0, keepdims=True)
                gain = jnp.concatenate([gain_ref[c0:c1]] * (T // LANES), axis=1)
                yc_t = cen * lax.rsqrt(var + EPS) * gain
                out_ref[0, t0:t1, c0:c1] = jnp.transpose(yc_t).astype(BF16)
            else:
                out_ref[0, c0:c1, t0:t1] = hout_t


def _mlstm_pass(q, k, v_t, terms, extra, *, reverse):
    B, L, W = k.shape
    cps = MLSTM_CHUNKS_PER_STEP
    T = MLSTM_CHUNK * cps
    n = L // T
    pos = (lambda i: n - 1 - i) if reverse else (lambda i: i)
    rows = pl.BlockSpec((1, T, W), lambda b, i: (b, pos(i), 0))
    cols = pl.BlockSpec((1, W, T), lambda b, i: (b, 0, pos(i)))
    in_specs = [rows, rows, cols,
                pl.BlockSpec((1, cps, N_GATE_TERMS, 8, MLSTM_CHUNK), lambda b, i: (b, pos(i), 0, 0, 0))]
    finalize = extra is not None
    args = [q, k, v_t, terms]
    if finalize:
        in_specs += [cols, cols, _resident((W, LANES))]
        args += list(extra)
    return pl.pallas_call(
        functools.partial(_mlstm_kernel, reverse=reverse, finalize=finalize),
        name="mlstm_bwd" if reverse else "mlstm_fwd",
        grid=(B, n),
        in_specs=in_specs,
        out_specs=rows if finalize else cols,
        out_shape=jax.ShapeDtypeStruct((B, L, W), BF16) if finalize else jax.ShapeDtypeStruct((B, W, L), F32),
        scratch_shapes=[pltpu.VMEM((N_HEADS, STATE_ROWS, HEAD_DIM), F32)],
        compiler_params=_params("parallel", "arbitrary"),
    )(*args)


def _merge_kernel(x_ref, h_ref, ya_ref, yb_ref, yc_ref, yd_ref, wgp_ref, wmg_ref, wbr_ref, wout_ref, g_ref, out_ref):
    W, D = BRANCH_WIDTH, D_MODEL
    h_half = h_ref[...] * 0.5
    acc = None
    for g, br_ref in enumerate((ya_ref, yb_ref, yc_ref, yd_ref)):
        gp_half = _dot(h_half, wgp_ref[:, g * W:(g + 1) * W])
        gated = (br_ref[...].astype(F32) * (gp_half * (jnp.tanh(gp_half) + 1.0))).astype(BF16)
        proj = _dot(gated, wbr_ref[g])
        term = (jnp.tanh(_dot(h_half, wmg_ref[:, g * D:(g + 1) * D])) + 1.0) * proj
        acc = term if acc is None else acc + term
    out = _dot((0.5 * acc).astype(BF16), wout_ref[...])
    ms = jnp.mean(out * out, axis=-1, keepdims=True)
    out_ref[...] = x_ref[...] + out * lax.rsqrt(ms + EPS) * g_ref[...]


def _merge(x_rows, h_rows, branches, w_gp, w_mg, w_br, w_out, gain):
    R, D = x_rows.shape
    T, W = ROW_TILE, BRANCH_WIDTH
    rows = lambda c: pl.BlockSpec((T, c), lambda i: (i, 0))
    return pl.pallas_call(
        _merge_kernel,
        name="merge",
        grid=(R // T,),
        in_specs=[rows(D), rows(D)] + [rows(W)] * N_BRANCHES
        + [_resident(w_gp.shape), _resident(w_mg.shape), _resident(w_br.shape), _resident(w_out.shape),
           _resident((1, D))],
        out_specs=rows(D),
        out_shape=jax.ShapeDtypeStruct((R, D), F32),
        compiler_params=_params("parallel"),
    )(x_rows, h_rows, *branches, w_gp, w_mg, w_br, w_out, gain.reshape(1, D))


def _t5_bucket(rel):
    nb = REL_BUCKETS // 2
    max_exact = nb // 2
    ret = (rel > 0).astype(np.int32) * nb
    n = np.abs(rel)
    large = max_exact + (np.log(np.maximum(n, 1) / max_exact) / np.log(REL_MAX_DIST / max_exact)
                         * (nb - max_exact)).astype(np.int32)
    large = np.minimum(large, nb - 1)
    return (ret + np.where(n < max_exact, n, large)).astype(np.int32)


def _band_bias(rel_bias, group, dilation):
    qb, hw = ATTN_Q_BLOCK, HALF_WINDOW
    rel = np.arange(qb + 2 * hw)[None, :] - hw - np.arange(qb)[:, None]
    heads = rel_bias[:, group * N_HEADS:(group + 1) * N_HEADS].astype(F32)
    onehot = (jnp.asarray(_t5_bucket(dilation * rel))[None, :, :] == jnp.arange(REL_BUCKETS)[:, None, None])
    bias = jnp.sum(jnp.where(onehot[:, None], heads[:, :, None, None], 0.0), axis=0)
    return jnp.where(jnp.asarray(np.abs(rel) <= hw)[None], bias * (HEAD_DIM ** 0.5), NEG)


def _split_weights(w_in, conv_qk, w_mem_kv, w_branch, w_out):
    W, D, A = BRANCH_WIDTH, D_MODEL, A_QKV_WIDTH
    off = np.cumsum([0, A, A, A, W, W, W, W, W, MLSTM_GATES, W, N_BRANCHES * W, N_BRANCHES * D])
    seg = lambda j: w_in[:, off[j]:off[j + 1]]
    qa, ka, va = seg(0), seg(1), seg(2)
    w_a = [jnp.concatenate([t[:, g * W:(g + 1) * W] for t in (qa, ka, va)], axis=1).astype(BF16)
           for g in range(len(DILATIONS))]
    return dict(
        a=w_a,
        bd=jnp.concatenate([seg(3), seg(9)], axis=1).astype(BF16),
        qk=jnp.concatenate([seg(4), seg(5)], axis=1).astype(BF16),
        vo_t=jnp.concatenate([seg(6), seg(7)], axis=1).T.astype(BF16),
        g=seg(8).astype(BF16),
        gp=seg(10).astype(BF16),
        mg=seg(11).astype(BF16),
        conv=0.5 * conv_qk.astype(F32),
        memkv=w_mem_kv.astype(BF16),
        br=w_branch.astype(BF16),
        out=w_out.astype(BF16),
    )


def _layer(x, mem, biases, fft_tables, wts, norm_pre, gate_bias, head_gain, mem_norm, norm_post):
    B, L, D = x.shape
    W = BRANCH_WIDTH
    h, h4, h16 = _prenorm(x, norm_pre)

    outs, lses = [], []
    for g, (d, hp) in enumerate(zip(DILATIONS, (h, h4, h16))):
        q, kv = _proj_a(hp.reshape(B * L, D), wts["a"][g])
        o, lse = _attention(q.reshape(B * d, L // d, W), kv.reshape(B * d, L // d, 2 * W), biases[g])
        outs.append(o.reshape(B, d, L // d, W))
        lses.append(lse.reshape(B, d, L // d, LANES))
    ya = _combine(*outs, *lses)

    kv_mem = _memkv(mem, mem_norm, wts["memkv"])
    xb, yd = _proj_bd(h, wts["bd"], kv_mem)
    yb = _fnet(xb, fft_tables)

    qm, km, v_t, oc_t, gates = _proj_c(h, wts["qk"], wts["vo_t"], wts["g"], wts["conv"], gate_bias.reshape(-1))
    T = MLSTM_CHUNK
    gates_t = jnp.swapaxes(gates.reshape(B, L // T, T, MLSTM_GATES), 2, 3)
    terms_fwd = _gate_terms(gates_t[:, :, :8], reverse=False)
    terms_bwd = _gate_terms(gates_t[:, :, 8:], reverse=True)
    gain = jnp.broadcast_to(head_gain.astype(F32)[:, None], (W, LANES))
    h_fwd_t = _mlstm_pass(qm, km, v_t, terms_fwd, None, reverse=False)
    yc = _mlstm_pass(qm, km, v_t, terms_bwd, (h_fwd_t, oc_t, gain), reverse=True)

    rows = lambda t: t.reshape(B * L, t.shape[-1])
    y = _merge(rows(x), rows(h), [rows(ya), rows(yb), rows(yc), rows(yd)],
               wts["gp"], wts["mg"], wts["br"], wts["out"], norm_post)
    return y.reshape(B, L, D)


def kernel(x_prompt, x_sample, mem_prompt, mem_sample, rel_bias, norm_pre, w_in, conv_qk, mlstm_gate_bias,
           mlstm_head_gain, mem_norm, w_mem_kv, w_branch, w_out, norm_post):
    depth = w_in.shape[0]
    biases = [_band_bias(rel_bias, g, d) for g, d in enumerate(DILATIONS)]
    layer_wts = [_split_weights(w_in[l], conv_qk[l], w_mem_kv[l], w_branch[l], w_out[l]) for l in range(depth)]
    outs = []
    for x, mem in ((x_prompt, mem_prompt), (x_sample, mem_sample)):
        tables = _fnet_tables(x.shape[1])
        for l in range(depth):
            x = _layer(x, mem, biases, tables, layer_wts[l], norm_pre[l], mlstm_gate_bias[l],
                       mlstm_head_gain[l], mem_norm[l], norm_post[l])
        outs.append(x)
    return tuple(outs)
```
